```python
import jax, jax.numpy as jnp
from jax import lax
import numpy as np

D_MODEL = 1024
BATCH = 32
SEQ = 2048
DEPTH = 1
DEC_BATCH = 128
DEC_SEQ = 4
PAST_LEN = 8192
PAGE_SIZE = 128

NORM_EPS = 1e-5
RWKV_HEAD = 64
RWKV_HEADS = D_MODEL // RWKV_HEAD
DECAY_LORA = max(32, int(round(1.8 * D_MODEL ** 0.5 / 32)) * 32)
AAA_LORA = max(32, int(round(1.8 * D_MODEL ** 0.5 / 32)) * 32)
GATE_LORA = max(32, int(round(0.6 * D_MODEL ** 0.8 / 32)) * 32)
GN_EPS = 64e-5
C_RW = 3 * D_MODEL + DECAY_LORA + AAA_LORA + GATE_LORA
ATT_HEAD = 128
ATT_HEADS_PER_GROUP = 4
DILATION_GROUPS = ((128, 1), (512, 4), (2048, 16))
N_GROUPS = len(DILATION_GROUPS)
ATT_DIM = N_GROUPS * ATT_HEADS_PER_GROUP * ATT_HEAD
ATT_OUT = ATT_HEADS_PER_GROUP * ATT_HEAD
ROPE_THETA = 10000.0
P_TOTAL = C_RW + 3 * ATT_DIM + 2 * D_MODEL
N_EXPERTS = 32
TOP_K = 4
D_EXPERT = D_MODEL
SWIGLU_ALPHA = 1.702
SWIGLU_LIMIT = 7.0
MOE_BLOCK = 128

kernel_name = 'hybrid_rwkv7_dilated_swa_moe_step'

F32 = jnp.float32


def rms_norm(x, w, eps=NORM_EPS):
    x32 = x.astype(F32)
    y = x32 * lax.rsqrt(jnp.mean(x32 * x32, axis=-1, keepdims=True) + eps)
    return (y * w.astype(F32)).astype(x.dtype)


def rotary(x, pos):
    e = x.shape[-1]
    inv = ROPE_THETA ** (-jnp.arange(0, e, 2, dtype=F32) / e)
    ang = pos.astype(F32)[:, None] * inv[None, :]
    cos = jnp.cos(ang)[None, :, None, None, :]
    sin = jnp.sin(ang)[None, :, None, None, :]
    x32 = x.astype(F32)
    x1, x2 = x32[..., : e // 2], x32[..., e // 2:]
    return jnp.concatenate([x1 * cos - x2 * sin, x2 * cos + x1 * sin], axis=-1).astype(x.dtype)


def rwkv7_branch(p_rw, prev_rw, s0, mu, w0, w2, a0, a2, g2, k_k, k_a, r_k, lnx_w, lnx_b):
    b, t, _ = p_rw.shape
    h, n, d = RWKV_HEADS, RWKV_HEAD, D_MODEL
    shifted = jnp.concatenate([prev_rw[:, None], p_rw[:, :-1]], axis=1)
    ps = p_rw + (shifted - p_rw) * mu
    r, k, v, xw, xa, xg = jnp.split(ps, [d, 2 * d, 3 * d, 3 * d + DECAY_LORA, 3 * d + DECAY_LORA + AAA_LORA], axis=-1)
    w = -jax.nn.softplus(-(w0 + jnp.tanh(xw) @ w2)) - 0.5
    decay = jnp.exp(-jnp.exp(w.astype(F32)))
    a = jax.nn.sigmoid(a0 + xa @ a2)
    g = jax.nn.sigmoid(xg) @ g2
    kk = (k * k_k).astype(F32).reshape(b, t, h, n)
    kk = kk / jnp.maximum(jnp.sqrt(jnp.sum(kk * kk, axis=-1, keepdims=True)), 1e-12)
    k = k * (1 + (a - 1) * k_a)
    heads = lambda z: z.astype(F32).reshape(b, t, h, n)
    rh, kh, vh, ah, wh = heads(r), heads(k), heads(v), heads(a), heads(decay)
    xs = tuple(jnp.moveaxis(z, 1, 0) for z in (rh, wh, kh, vh, -kk, kk * ah))

    def step(s, inp):
        r_t, w_t, k_t, v_t, a_t, b_t = inp
        sa = jnp.einsum('bhvk,bhk->bhv', s, a_t)
        s = s * w_t[:, :, None, :] + sa[..., None] * b_t[:, :, None, :] + v_t[..., None] * k_t[:, :, None, :]
        return s, jnp.einsum('bhvk,bhk->bhv', s, r_t)

    s_fin, y = lax.scan(step, s0.astype(F32), xs)
    y = jnp.moveaxis(y, 0, 1)
    mean = jnp.mean(y, axis=-1, keepdims=True)
    var = jnp.mean(jnp.square(y - mean), axis=-1, keepdims=True)
    y = ((y - mean) * lax.rsqrt(var + GN_EPS)).reshape(b, t, d) * lnx_w.astype(F32) + lnx_b.astype(F32)
    bonus = jnp.sum(rh * kh * r_k.astype(F32), axis=-1, keepdims=True) * vh
    y = y + bonus.reshape(b, t, d)
    return (y * g.astype(F32)).astype(p_rw.dtype), s_fin


def band_dilated_attention(q, k, v, dil, nw):
    b, t, h, e = q.shape
    blk = nw
    unit = dil * blk
    tp = -(-t // unit) * unit
    nb = tp // unit
    pad = ((0, 0), (0, tp - t), (0, 0), (0, 0))

    def blocks(z):
        return jnp.pad(z, pad).reshape(b, nb, blk, dil, h, e)

    def with_prev(z):
        prev = jnp.pad(z[:, :-1], ((0, 0), (1, 0), (0, 0), (0, 0), (0, 0), (0, 0)))
        return jnp.concatenate([prev, z], axis=2)

    qb = blocks(q)
    kb, vb = with_prev(blocks(k)), with_prev(blocks(v))
    s = jnp.einsum('bnqrhe,bnkrhe->bnrhqk', qb, kb, preferred_element_type=F32) * (e ** -0.5)
    qi = jnp.arange(blk)[:, None]
    ki = jnp.arange(2 * blk)[None, :]
    off = qi + blk - ki
    valid = (off >= 0) & (off <= nw)
    has_rows = (jnp.arange(nb)[:, None, None] > 0) | (ki >= blk)[None]
    mask = valid[None] & has_rows
    s = jnp.where(mask[None, :, None, None], s, -jnp.inf)
    m = jnp.max(s, axis=-1, keepdims=True)
    p = jnp.exp(s - m)
    den = jnp.sum(p, axis=-1)
    o = jnp.einsum('bnrhqk,bnkrhe->bnrhqe', p, vb.astype(F32)) / den[..., None]
    lse = m[..., 0] + jnp.log(den)
    o = jnp.transpose(o, (0, 1, 4, 2, 3, 5)).reshape(b, tp, h, e)[:, :t]
    lse = jnp.transpose(lse, (0, 1, 4, 2, 3)).reshape(b, tp, h)[:, :t]
    return o, lse


def gather_dilated_attention(q, kv_all, dil, nw):
    b, t, h, e = q.shape
    past = kv_all.shape[1] - t
    idx = past + jnp.arange(t)[:, None] - dil * jnp.arange(nw + 1)[None, :]
    valid = idx >= 0
    g = jnp.take(kv_all, jnp.maximum(idx, 0), axis=1)
    s = jnp.einsum('bthe,btjhe->bthj', q, g[:, :, :, 0], preferred_element_type=F32) * (e ** -0.5)
    s = jnp.where(valid[None, :, None, :], s, -jnp.inf)
    m = jnp.max(s, axis=-1, keepdims=True)
    p = jnp.exp(s - m)
    den = jnp.sum(p, axis=-1)
    o = jnp.einsum('bthj,btjhe->bthe', p, g[:, :, :, 1].astype(F32)) / den[..., None]
    return o, m[..., 0] + jnp.log(den)


def clamped_swiglu(hdn):
    glu, lin = hdn[..., ::2], hdn[..., 1::2]
    glu = jnp.minimum(glu, SWIGLU_LIMIT)
    lin = jnp.clip(lin, -SWIGLU_LIMIT, SWIGLU_LIMIT)
    return glu * jax.nn.sigmoid(SWIGLU_ALPHA * glu) * (lin + 1)


def moe_ffn(xt, router_w, router_b, w1, b1, w2, b2):
    n, d = xt.shape
    logits = (xt @ router_w).astype(F32) + router_b.astype(F32)
    top_val, top_idx = lax.top_k(logits, TOP_K)
    gates = jax.nn.softmax(top_val, axis=-1)
    nk = n * TOP_K
    flat_e = top_idx.reshape(nk)
    flat_tok = jnp.repeat(jnp.arange(n, dtype=jnp.int32), TOP_K)
    flat_g = gates.reshape(nk)
    order = jnp.argsort(flat_e)
    se, stok, sg = flat_e[order], flat_tok[order], flat_g[order]
    counts = jnp.bincount(flat_e, length=N_EXPERTS)
    starts = jnp.cumsum(counts) - counts
    blocks_e = (counts + MOE_BLOCK - 1) // MOE_BLOCK
    block_end = jnp.cumsum(blocks_e)
    row_start = (block_end - blocks_e) * MOE_BLOCK
    dest = row_start[se] + jnp.arange(nk, dtype=jnp.int32) - starts[se]
    n_blocks = -(-nk // MOE_BLOCK) + N_EXPERTS
    rows = n_blocks * MOE_BLOCK
    tok_buf = jnp.zeros((rows,), jnp.int32).at[dest].set(stok)
    gate_buf = jnp.zeros((rows,), F32).at[dest].set(sg)
    blk_expert = jnp.minimum(jnp.searchsorted(block_end, jnp.arange(n_blocks), side='right'), N_EXPERTS - 1)
    xb = xt[tok_buf].reshape(n_blocks, MOE_BLOCK, d)

    def expert_block(args):
        xblk, e = args
        return clamped_swiglu(xblk @ w1[e] + b1[e]) @ w2[e] + b2[e]

    yb = lax.map(expert_block, (xb, blk_expert))
    contrib = yb.reshape(rows, d).astype(F32) * gate_buf[:, None]
    return jnp.zeros((n, d), F32).at[tok_buf].add(contrib)


def layer_forward(x, pos, shift_prev, wkv0, kv_caches, lw):
    b, t, _ = x.shape
    xn = rms_norm(x, lw['attn_norm_w'])
    w_in = lw['w_in']
    proj = xn @ w_in
    prev_rw = shift_prev.astype(x.dtype) @ w_in[:, :C_RW]
    p_rw = proj[..., :C_RW]
    p_att = proj[..., C_RW:C_RW + 3 * ATT_DIM]
    p_gate = proj[..., C_RW + 3 * ATT_DIM:]

    y_rw, s_fin = rwkv7_branch(p_rw, prev_rw, wkv0, lw['rw_mu'], lw['rw_w0'], lw['rw_w2'], lw['rw_a0'],
                               lw['rw_a2'], lw['rw_g2'], lw['rw_kk'], lw['rw_ka'], lw['rw_rk'],
                               lw['rw_lnx_w'], lw['rw_lnx_b'])

    q, k, v = jnp.split(p_att, 3, axis=-1)
    hshape = (b, t, N_GROUPS, ATT_HEADS_PER_GROUP, ATT_HEAD)
    q = rotary(rms_norm(q.reshape(hshape), lw['q_norm_w']), pos)
    k = rotary(rms_norm(k.reshape(hshape), lw['k_norm_w']), pos)
    v = v.reshape(hshape)
    outs, lses, new_kv = [], [], []
    for gi, (window, dil) in enumerate(DILATION_GROUPS):
        nw = window // dil
        kv_new = jnp.stack([k[:, :, gi], v[:, :, gi]], axis=2)
        if kv_caches is None:
            o, lse = band_dilated_attention(q[:, :, gi], k[:, :, gi], v[:, :, gi], dil, nw)
            new_kv.append(kv_new[:, t - min(window, t):])
        else:
            buf_len = kv_caches[gi].shape[1]
            kv_all = jnp.concatenate([kv_caches[gi].astype(x.dtype), kv_new], axis=1)
            o, lse = gather_dilated_attention(q[:, :, gi], kv_all, dil, nw)
            new_kv.append(kv_all[:, kv_all.shape[1] - buf_len:])
        outs.append(o)
        lses.append(lse)
    alpha = jax.nn.softmax(jnp.stack(lses, axis=0), axis=0)
    o_att = jnp.sum(alpha[..., None] * jnp.stack(outs, axis=0), axis=0).reshape(b, t, ATT_OUT).astype(x.dtype)

    gate_a, gate_b = jnp.split(jax.nn.sigmoid(p_gate), 2, axis=-1)
    w_br = lw['w_br']
    merged = gate_a * (y_rw @ w_br[:D_MODEL]) + gate_b * (o_att @ w_br[D_MODEL:])
    h = x + merged @ lw['w_out']
    hn = rms_norm(h, lw['ffn_norm_w'])
    ffn = moe_ffn(hn.reshape(b * t, D_MODEL), lw['router_w'], lw['router_b'], lw['moe_w1'], lw['moe_b1'],
                  lw['moe_w2'], lw['moe_b2'])
    y = (h.astype(F32) + ffn.reshape(b, t, D_MODEL)).astype(x.dtype)
    return y, xn[:, -1], s_fin.astype(x.dtype), new_kv


def setup_inputs(seed: int = 0) -> dict:
    key = jax.random.key(seed)
    ks = jax.random.split(key, 32)
    nrm = lambda kk, shape, s: s * jax.random.normal(kk, shape, F32)
    hg, hd = ATT_HEADS_PER_GROUP, ATT_HEAD
    w1_w, w2_w, w3_w = (DILATION_GROUPS[0][0], DILATION_GROUPS[1][0], DILATION_GROUPS[2][0])
    return {
        'x_prompt': nrm(ks[0], (BATCH, SEQ, D_MODEL), 1.0),
        'x_sample': nrm(ks[1], (DEC_BATCH, DEC_SEQ, D_MODEL), 1.0),
        'state_wkv': nrm(ks[2], (DEPTH, DEC_BATCH, RWKV_HEADS, RWKV_HEAD, RWKV_HEAD), 0.3),
        'state_shift': nrm(ks[3], (DEPTH, DEC_BATCH, D_MODEL), 1.0),
        'cache_kv_w128': nrm(ks[4], (DEPTH, DEC_BATCH, min(w1_w, PAST_LEN), 2, hg, hd), 1.0),
        'cache_kv_w512': nrm(ks[5], (DEPTH, DEC_BATCH, min(w2_w, PAST_LEN), 2, hg, hd), 1.0),
        'cache_kv_w2048': nrm(ks[6], (DEPTH, DEC_BATCH, min(w3_w, PAST_LEN), 2, hg, hd), 1.0),
        'attn_norm_w': 1.0 + nrm(ks[7], (DEPTH, D_MODEL), 0.02),
        'w_in': nrm(ks[8], (DEPTH, D_MODEL, P_TOTAL), D_MODEL ** -0.5),
        'rw_mu': jax.random.uniform(ks[9], (DEPTH, C_RW), F32, 0.0, 1.0),
        'rw_w0': jax.random.uniform(ks[10], (DEPTH, D_MODEL), F32, -6.0, -0.5),
        'rw_w2': nrm(ks[11], (DEPTH, DECAY_LORA, D_MODEL), 0.1),
        'rw_a0': nrm(ks[12], (DEPTH, D_MODEL), 0.1),
        'rw_a2': nrm(ks[13], (DEPTH, AAA_LORA, D_MODEL), 0.1),
        'rw_g2': nrm(ks[14], (DEPTH, GATE_LORA, D_MODEL), GATE_LORA ** -0.5),
        'rw_kk': 0.85 + nrm(ks[15], (DEPTH, D_MODEL), 0.05),
        'rw_ka': 1.0 + nrm(ks[16], (DEPTH, D_MODEL), 0.05),
        'rw_rk': nrm(ks[17], (DEPTH, RWKV_HEADS, RWKV_HEAD), 0.1),
        'rw_lnx_w': 1.0 + nrm(ks[18], (DEPTH, D_MODEL), 0.02),
        'rw_lnx_b': nrm(ks[19], (DEPTH, D_MODEL), 0.02),
        'q_norm_w': 1.0 + nrm(ks[20], (DEPTH, ATT_HEAD), 0.02),
        'k_norm_w': 1.0 + nrm(ks[21], (DEPTH, ATT_HEAD), 0.02),
        'w_br': nrm(ks[22], (DEPTH, D_MODEL + ATT_OUT, D_MODEL), (D_MODEL + ATT_OUT) ** -0.5),
        'w_out': nrm(ks[23], (DEPTH, D_MODEL, D_MODEL), D_MODEL ** -0.5),
        'ffn_norm_w': 1.0 + nrm(ks[24], (DEPTH, D_MODEL), 0.02),
        'router_w': nrm(ks[25], (DEPTH, D_MODEL, N_EXPERTS), D_MODEL ** -0.5),
        'router_b': nrm(ks[26], (DEPTH, N_EXPERTS), 0.01),
        'moe_w1': nrm(ks[27], (DEPTH, N_EXPERTS, D_MODEL, 2 * D_EXPERT), D_MODEL ** -0.5),
        'moe_b1': nrm(ks[28], (DEPTH, N_EXPERTS, 2 * D_EXPERT), 0.01),
        'moe_w2': nrm(ks[29], (DEPTH, N_EXPERTS, D_EXPERT, D_MODEL), D_EXPERT ** -0.5),
        'moe_b2': nrm(ks[30], (DEPTH, N_EXPERTS, D_MODEL), 0.01),
    }


def reference(x_prompt, x_sample, state_wkv, state_shift, cache_kv_w128, cache_kv_w512, cache_kv_w2048,
              attn_norm_w, w_in, rw_mu, rw_w0, rw_w2, rw_a0, rw_a2, rw_g2, rw_kk, rw_ka, rw_rk,
              rw_lnx_w, rw_lnx_b, q_norm_w, k_norm_w, w_br, w_out, ffn_norm_w, router_w, router_b,
              moe_w1, moe_b1, moe_w2, moe_b2):
    bp, tp, _ = x_prompt.shape
    ts = x_sample.shape[1]
    pos_p = jnp.arange(tp, dtype=jnp.int32)
    pos_s = PAST_LEN + jnp.arange(ts, dtype=jnp.int32)
    hp, hs = x_prompt, x_sample
    p_wkv, p_shift, s_wkv, s_shift = [], [], [], []
    p_kv = [[] for _ in DILATION_GROUPS]
    s_kv = [[] for _ in DILATION_GROUPS]
    for l in range(DEPTH):
        lw = {'attn_norm_w': attn_norm_w[l], 'w_in': w_in[l], 'rw_mu': rw_mu[l], 'rw_w0': rw_w0[l],
              'rw_w2': rw_w2[l], 'rw_a0': rw_a0[l], 'rw_a2': rw_a2[l], 'rw_g2': rw_g2[l], 'rw_kk': rw_kk[l],
              'rw_ka': rw_ka[l], 'rw_rk': rw_rk[l], 'rw_lnx_w': rw_lnx_w[l], 'rw_lnx_b': rw_lnx_b[l],
              'q_norm_w': q_norm_w[l], 'k_norm_w': k_norm_w[l], 'w_br': w_br[l], 'w_out': w_out[l],
              'ffn_norm_w': ffn_norm_w[l], 'router_w': router_w[l], 'router_b': router_b[l],
              'moe_w1': moe_w1[l], 'moe_b1': moe_b1[l], 'moe_w2': moe_w2[l], 'moe_b2': moe_b2[l]}
        zero_shift = jnp.zeros((bp, D_MODEL), x_prompt.dtype)
        zero_wkv = jnp.zeros((bp, RWKV_HEADS, RWKV_HEAD, RWKV_HEAD), F32)
        hp, sh_p, wkv_p, kv_p = layer_forward(hp, pos_p, zero_shift, zero_wkv, None, lw)
        hs, sh_s, wkv_s, kv_s = layer_forward(hs, pos_s, state_shift[l], state_wkv[l],
                                              (cache_kv_w128[l], cache_kv_w512[l], cache_kv_w2048[l]), lw)
        p_wkv.append(wkv_p)
        p_shift.append(sh_p)
        s_wkv.append(wkv_s)
        s_shift.append(sh_s)
        for gi in range(N_GROUPS):
            p_kv[gi].append(kv_p[gi])
            s_kv[gi].append(kv_s[gi])
    return (hp, hs,
            jnp.stack(p_wkv), jnp.stack(p_shift), jnp.stack(p_kv[0]), jnp.stack(p_kv[1]), jnp.stack(p_kv[2]),
            jnp.stack(s_wkv), jnp.stack(s_shift), jnp.stack(s_kv[0]), jnp.stack(s_kv[1]), jnp.stack(s_kv[2]))
```

```python
import functools

import jax
import jax.numpy as jnp
from jax import lax
from jax.experimental import pallas as pl
from jax.experimental.pallas import tpu as pltpu

F32 = jnp.float32
BF16 = jnp.bfloat16

D_MODEL = 1024
NORM_EPS = 1e-5
RWKV_HEAD = 64
RWKV_HEADS = D_MODEL // RWKV_HEAD
DECAY_LORA = 64
AAA_LORA = 64
GATE_LORA = 160
GN_EPS = 64e-5
C_RW = 3 * D_MODEL + DECAY_LORA + AAA_LORA + GATE_LORA
ATT_HEAD = 128
ATT_HPG = 4
DILATION_GROUPS = ((128, 1), (512, 4), (2048, 16))
N_GROUPS = 3
ATT_DIM = N_GROUPS * ATT_HPG * ATT_HEAD
ATT_OUT = ATT_HPG * ATT_HEAD
ROPE_THETA = 10000.0
N_EXPERTS = 32
TOP_K = 4
SWIGLU_ALPHA = 1.702
SWIGLU_LIMIT = 7.0
PAST_LEN = 8192

LANES = 128
SUBLANES = 8
VMEM_LIMIT = 56 * 1024 * 1024

PROJ_TN = 512
T_RKV = 0
T_LORA = 6
T_Q = 7
T_KV = 10
T_GATE = 16
N_TILES = 20
P_PAD = N_TILES * PROJ_TN
RW_PAD = (T_LORA + 1) * PROJ_TN

SCAN_CHUNK = 64
MOE_TM = 512


def _cparams(sem, vmem=VMEM_LIMIT):
    return pltpu.CompilerParams(dimension_semantics=sem, vmem_limit_bytes=vmem)


def _proj_kernel(x_ref, nw_ref, w_ref, qn_ref, kn_ref, cs_ref, sn_ref, o_ref, xn_ref, *, do_norm):
    j = pl.program_id(1)

    @pl.when(j == 0)
    def _():
        x = x_ref[...]
        if do_norm:
            x = x * lax.rsqrt(jnp.mean(x * x, axis=-1, keepdims=True) + NORM_EPS) * nw_ref[...]
        xn_ref[...] = x.astype(BF16)

    acc = jnp.dot(xn_ref[...], w_ref[...], preferred_element_type=F32)
    is_q = (j >= T_Q) & (j < T_KV)
    is_k = (j >= T_KV) & (j < T_GATE) & ((j - T_KV) % 2 == 0)
    is_qk = is_q | is_k
    is_gate = j >= T_GATE

    @pl.when(is_qk)
    def _():
        nwh = jnp.where(j < T_KV, qn_ref[...], kn_ref[...])
        cs = cs_ref[...]
        sn = sn_ref[...]
        for h in range(ATT_HPG):
            xh = acc[:, h * ATT_HEAD:(h + 1) * ATT_HEAD]
            y = xh * lax.rsqrt(jnp.mean(xh * xh, axis=-1, keepdims=True) + NORM_EPS) * nwh
            o_ref[:, h * ATT_HEAD:(h + 1) * ATT_HEAD] = y * cs + pltpu.roll(y, ATT_HEAD // 2, axis=1) * sn

    @pl.when(is_gate)
    def _():
        o_ref[...] = jax.nn.sigmoid(acc)

    @pl.when(jnp.logical_not(is_qk | is_gate))
    def _():
        o_ref[...] = acc


def _proj(x2d, nw, w_pad, qn, kn, cs_tab, sn_tab, *, tm, n_tiles, do_norm):
    n = x2d.shape[0]
    tab_blocks = cs_tab.shape[0] // tm
    return pl.pallas_call(
        functools.partial(_proj_kernel, do_norm=do_norm),
        grid=(n // tm, n_tiles),
        in_specs=[
            pl.BlockSpec((tm, D_MODEL), lambda i, j: (i, 0)),
            pl.BlockSpec((1, D_MODEL), lambda i, j: (0, 0)),
            pl.BlockSpec((D_MODEL, PROJ_TN), lambda i, j: (0, j)),
            pl.BlockSpec((1, ATT_HEAD), lambda i, j: (0, 0)),
            pl.BlockSpec((1, ATT_HEAD), lambda i, j: (0, 0)),
            pl.BlockSpec((tm, ATT_HEAD), lambda i, j: (i % tab_blocks, 0)),
            pl.BlockSpec((tm, ATT_HEAD), lambda i, j: (i % tab_blocks, 0)),
        ],
        out_specs=pl.BlockSpec((tm, PROJ_TN), lambda i, j: (i, j)),
        out_shape=jax.ShapeDtypeStruct((n, n_tiles * PROJ_TN), F32),
        scratch_shapes=[pltpu.VMEM((tm, D_MODEL), BF16)],
        compiler_params=_cparams(("parallel", "arbitrary")),
        name="proj",
    )(x2d, nw, w_pad, qn, kn, cs_tab, sn_tab)


def _rmsnorm_kernel(x_ref, w_ref, o_ref):
    x = x_ref[...]
    o_ref[...] = x * lax.rsqrt(jnp.mean(x * x, axis=-1, keepdims=True) + NORM_EPS) * w_ref[...]


def _rmsnorm_rows(x2d, w):
    return pl.pallas_call(
        _rmsnorm_kernel,
        out_shape=jax.ShapeDtypeStruct(x2d.shape, F32),
        name="rmsnorm_rows",
    )(x2d, w)


def _split3(x):
    h1 = x.astype(BF16)
    r1 = x - h1.astype(F32)
    h2 = r1.astype(BF16)
    h3 = (r1 - h2.astype(F32)).astype(BF16)
    return h1, h2, h3


def _seg_sum(x, seg_ones):
    outs = []
    for p in range(D_MODEL // LANES):
        xt = x[:, p * LANES:(p + 1) * LANES]
        hi = xt.astype(BF16)
        lo = (xt - hi.astype(F32)).astype(BF16)
        outs.append(jnp.dot(hi, seg_ones, preferred_element_type=F32)
                    + jnp.dot(lo, seg_ones, preferred_element_type=F32))
    return jnp.concatenate(outs, axis=1)


def _nt_dot(a, b):
    return lax.dot_general(a, b, (((1,), (1,)), ((), ())), preferred_element_type=F32)


def _rwkv_kernel(p_ref, prev0_ref, s0_ref, mu_ref, par_ref, w2_ref, a2_ref, g2_ref,
                 y_ref, st_ref, prev_ref, *, chunk, n_valid):
    c = pl.program_id(1)
    C = chunk

    @pl.when(c == 0)
    def _():
        st_ref[...] = s0_ref[...]
        prev_ref[...] = prev0_ref[0]

    p = p_ref[0]
    row = lax.broadcasted_iota(jnp.int32, (C, 1), 0)
    shifted = jnp.where(row == 0, prev_ref[...], pltpu.roll(p, 1, axis=0))
    prev_ref[...] = p[C - 1:C, :]
    ps = p + (shifted - p) * mu_ref[...]

    w0 = par_ref[0:1, :]
    a0 = par_ref[1:2, :]
    k_k = par_ref[2:3, :]
    k_a = par_ref[3:4, :]
    r_k = par_ref[4:5, :]
    lnx_w = par_ref[5:6, :]
    lnx_b = par_ref[6:7, :]

    r = ps[:, 0:D_MODEL]
    k = ps[:, D_MODEL:2 * D_MODEL]
    v = ps[:, 2 * D_MODEL:3 * D_MODEL]
    xw = ps[:, 3 * D_MODEL:3 * D_MODEL + LANES]
    xa = ps[:, 3 * D_MODEL + LANES:3 * D_MODEL + 2 * LANES]
    xg = ps[:, 3 * D_MODEL + 2 * LANES:RW_PAD]

    zw = w0 + jnp.dot(jnp.tanh(xw).astype(BF16), w2_ref[...], preferred_element_type=F32)
    nz = -zw
    softplus = jnp.log(1.0 + jnp.exp(-jnp.abs(nz))) + jnp.maximum(nz, 0.0)
    lw = -jnp.exp(-softplus - 0.5)
    a = jax.nn.sigmoid(a0 + jnp.dot(xa.astype(BF16), a2_ref[...], preferred_element_type=F32))
    g = jnp.dot(jax.nn.sigmoid(xg).astype(BF16), g2_ref[...], preferred_element_type=F32)

    li = lax.broadcasted_iota(jnp.int32, (LANES, LANES), 0)
    lj = lax.broadcasted_iota(jnp.int32, (LANES, LANES), 1)
    same_head = (li < RWKV_HEAD) == (lj < RWKV_HEAD)
    seg_ones = jnp.where(same_head, 1.0, 0.0).astype(BF16)
    blk_mask = jnp.where(same_head, 1.0, 0.0)

    kkr = k * k_k
    kk = kkr / jnp.maximum(jnp.sqrt(_seg_sum(kkr * kkr, seg_ones)), 1e-12)
    k2 = k * (1.0 + (a - 1.0) * k_a)
    bonus = _seg_sum(r * k2 * r_k, seg_ones) * v

    if n_valid is not None:
        valid = (c * C + row) < n_valid
        lw = jnp.where(valid, lw, 0.0)
        kk = jnp.where(valid, kk, 0.0)
        k2 = jnp.where(valid, k2, 0.0)

    ti = lax.broadcasted_iota(jnp.int32, (C, C), 0)
    tj = lax.broadcasted_iota(jnp.int32, (C, C), 1)
    tril_incl = jnp.where(ti >= tj, 1.0, 0.0)
    tril_strict = jnp.where(ti > tj, 1.0, 0.0)
    tril_b = tril_incl.astype(BF16)
    l1, l2, l3 = _split3(lw)
    cl = (jnp.dot(tril_b, l1, preferred_element_type=F32)
          + jnp.dot(tril_b, l2, preferred_element_type=F32)
          + jnp.dot(tril_b, l3, preferred_element_type=F32))
    cl_end = cl[C - 1:C, :]
    kb = kk * a
    e_inv = jnp.exp(-cl)
    e_tail = jnp.exp(cl_end - cl)
    a_t = (-kk) * jnp.exp(cl - lw)
    r_t = r * jnp.exp(cl)
    b_t = kb * e_inv
    k_t = k2 * e_inv
    b_h = kb * e_tail
    k_h = k2 * e_tail
    w_end = jnp.exp(cl_end)

    lane = lax.broadcasted_iota(jnp.int32, (1, LANES), 1)
    first = lane < RWKV_HEAD
    n_lev = max(1, (C - 1).bit_length())

    y_tiles = []
    for pr in range(D_MODEL // LANES):
        sl = slice(pr * LANES, (pr + 1) * LANES)
        at_p, rt_p, bt_p, kt_p = a_t[:, sl], r_t[:, sl], b_t[:, sl], k_t[:, sl]
        v_p = v[:, sl]
        vb = v_p.astype(BF16)
        z = st_ref[0, pr]
        ar = jnp.concatenate([at_p, rt_p], axis=0)
        q0 = _nt_dot(ar.astype(BF16), z.astype(BF16))
        btb = bt_p.astype(BF16)
        ktb = kt_p.astype(BF16)
        lab, lak, mrb, mrk = [], [], [], []
        for h in range(2):
            mh = first if h == 0 else jnp.logical_not(first)
            arh = jnp.where(mh, ar, 0.0).astype(BF16)
            gb = _nt_dot(arh, btb)
            gk = _nt_dot(arh, ktb)
            lab.append((gb[:C] * tril_strict).astype(BF16))
            lak.append((gk[:C] * tril_strict).astype(BF16))
            mrb.append((gb[C:] * tril_incl).astype(BF16))
            mrk.append((gk[C:] * tril_incl).astype(BF16))
        u = q0[:C] + jnp.where(first, jnp.dot(lak[0], vb, preferred_element_type=F32),
                               jnp.dot(lak[1], vb, preferred_element_type=F32))
        xs = lab
        for lev in range(n_lev):
            ub = u.astype(BF16)
            u = u + jnp.where(first, jnp.dot(xs[0], ub, preferred_element_type=F32),
                              jnp.dot(xs[1], ub, preferred_element_type=F32))
            if lev < n_lev - 1:
                xs = [jnp.dot(xh, xh, preferred_element_type=F32).astype(BF16) for xh in xs]
        ub = u.astype(BF16)
        y_p = q0[C:] + jnp.where(
            first,
            jnp.dot(mrb[0], ub, preferred_element_type=F32) + jnp.dot(mrk[0], vb, preferred_element_type=F32),
            jnp.dot(mrb[1], ub, preferred_element_type=F32) + jnp.dot(mrk[1], vb, preferred_element_type=F32))
        y_tiles.append(y_p)
        uv_t = jnp.concatenate([u, v_p], axis=0).T.astype(BF16)
        bk_h = jnp.concatenate([b_h[:, sl], k_h[:, sl]], axis=0).astype(BF16)
        st_ref[0, pr] = z * w_end[:, sl] + blk_mask * jnp.dot(uv_t, bk_h, preferred_element_type=F32)

    y = jnp.concatenate(y_tiles, axis=1)
    inv_n = 1.0 / RWKV_HEAD
    mean = _seg_sum(y, seg_ones) * inv_n
    yc = y - mean
    var = _seg_sum(yc * yc, seg_ones) * inv_n
    yn = yc * lax.rsqrt(var + GN_EPS) * lnx_w + lnx_b
    y_ref[0] = ((yn + bonus) * g).astype(y_ref.dtype)


def _rwkv(p3, prev0, s0z, mu_pad, par, w2p, a2p, g2p, *, chunk, n_valid):
    b, t, _ = p3.shape
    n_pairs = D_MODEL // LANES
    full = lambda shape: pl.BlockSpec(shape, lambda i, c: (0,) * len(shape))
    return pl.pallas_call(
        functools.partial(_rwkv_kernel, chunk=chunk, n_valid=n_valid),
        grid=(b, t // chunk),
        in_specs=[
            pl.BlockSpec((1, chunk, RW_PAD), lambda i, c: (i, c, 0)),
            pl.BlockSpec((1, 1, RW_PAD), lambda i, c: (i, 0, 0)),
            pl.BlockSpec((1, n_pairs, LANES, LANES), lambda i, c: (i, 0, 0, 0)),
            full((1, RW_PAD)),
            full((SUBLANES, D_MODEL)),
            full((LANES, D_MODEL)),
            full((LANES, D_MODEL)),
            full((2 * LANES, D_MODEL)),
        ],
        out_specs=[
            pl.BlockSpec((1, chunk, D_MODEL), lambda i, c: (i, c, 0)),
            pl.BlockSpec((1, n_pairs, LANES, LANES), lambda i, c: (i, 0, 0, 0)),
        ],
        out_shape=[
            jax.ShapeDtypeStruct((b, t, D_MODEL), BF16),
            jax.ShapeDtypeStruct((b, n_pairs, LANES, LANES), F32),
        ],
        scratch_shapes=[pltpu.VMEM((1, RW_PAD), F32)],
        compiler_params=_cparams(("parallel", "arbitrary")),
        name="rwkv",
    )(p3, prev0, s0z, mu_pad, par, w2p, a2p, g2p)


def _attn_kernel(q_ref, kvp_ref, kvo_ref, o_ref, l_ref, *, blk):
    i = pl.program_id(2)
    qi = lax.broadcasted_iota(jnp.int32, (blk, 2 * blk), 0)
    ki = lax.broadcasted_iota(jnp.int32, (blk, 2 * blk), 1)
    lo = jnp.where(i == 0, blk, 0)
    valid = (ki >= qi) & (ki <= qi + blk) & (ki >= lo)
    scale = ATT_HEAD ** -0.5
    for h in range(ATT_HPG):
        ks = slice(h * ATT_HEAD, (h + 1) * ATT_HEAD)
        vs = slice(ATT_OUT + h * ATT_HEAD, ATT_OUT + (h + 1) * ATT_HEAD)
        q = q_ref[0, :, ks].astype(BF16)
        kc = jnp.concatenate([kvp_ref[0, :, ks], kvo_ref[0, :, ks]], axis=0).astype(BF16)
        vc = jnp.concatenate([kvp_ref[0, :, vs], kvo_ref[0, :, vs]], axis=0).astype(BF16)
        s = _nt_dot(q, kc) * scale
        s = jnp.where(valid, s, -jnp.inf)
        m = jnp.max(s, axis=-1, keepdims=True)
        pexp = jnp.exp(s - m)
        den = jnp.sum(pexp, axis=-1, keepdims=True)
        o = jnp.dot(pexp.astype(BF16), vc, preferred_element_type=F32) / den
        o_ref[0, :, ks] = o
        l_ref[0, :, ks] = jnp.broadcast_to(m + jnp.log(den), (blk, ATT_HEAD))


def _attn_prompt(p3, g, dil, blk):
    b, t, _ = p3.shape
    ld = t // dil
    pv = p3.reshape(b, ld, dil * P_PAD)
    q_tiles = P_PAD // ATT_OUT
    kv_tiles = P_PAD // (2 * ATT_OUT)
    kv_base = (T_KV * PROJ_TN) // (2 * ATT_OUT)
    out_sd = jax.ShapeDtypeStruct((b, ld, dil * ATT_OUT), F32)
    o, l = pl.pallas_call(
        functools.partial(_attn_kernel, blk=blk),
        grid=(b, dil, ld // blk),
        in_specs=[
            pl.BlockSpec((1, blk, ATT_OUT), lambda bi, r, i: (bi, i, r * q_tiles + T_Q + g)),
            pl.BlockSpec((1, blk, 2 * ATT_OUT), lambda bi, r, i: (bi, jnp.maximum(i - 1, 0), r * kv_tiles + kv_base + g)),
            pl.BlockSpec((1, blk, 2 * ATT_OUT), lambda bi, r, i: (bi, i, r * kv_tiles + kv_base + g)),
        ],
        out_specs=[
            pl.BlockSpec((1, blk, ATT_OUT), lambda bi, r, i: (bi, i, r)),
            pl.BlockSpec((1, blk, ATT_OUT), lambda bi, r, i: (bi, i, r)),
        ],
        out_shape=[out_sd, out_sd],
        compiler_params=_cparams(("parallel", "parallel", "arbitrary")),
        name=f"attn_prompt_g{g}",
    )(pv, pv, pv)
    return o.reshape(b * t, ATT_OUT), l.reshape(b * t, ATT_OUT)


def _attn_sample_kernel(*refs, dils, ts):
    q_refs = refs[0:3]
    kvn_refs = refs[3:6]
    c_refs = refs[6:9]
    o_refs = refs[9:12]
    l_refs = refs[12:15]
    scale = ATT_HEAD ** -0.5
    for g in range(N_GROUPS):
        dil = dils[g]
        rows = c_refs[g].shape[1]
        ci = lax.broadcasted_iota(jnp.int32, (rows, 1), 0)
        ni = lax.broadcasted_iota(jnp.int32, (ts, 1), 0)
        for t in range(ts):
            cls = (t % dil) * 2 * ATT_OUT
            for h in range(ATT_HPG):
                ks = slice(h * ATT_HEAD, (h + 1) * ATT_HEAD)
                q = q_refs[g][0, t:t + 1, ks]
                kc = c_refs[g][0, :, cls + h * ATT_HEAD:cls + (h + 1) * ATT_HEAD]
                vc = c_refs[g][0, :, cls + ATT_OUT + h * ATT_HEAD:cls + ATT_OUT + (h + 1) * ATT_HEAD]
                kn = kvn_refs[g][0, :, ks]
                vn = kvn_refs[g][0, :, ATT_OUT + h * ATT_HEAD:ATT_OUT + (h + 1) * ATT_HEAD]
                sc = jnp.sum(kc * q, axis=-1, keepdims=True) * scale
                sn = jnp.sum(kn * q, axis=-1, keepdims=True) * scale
                c_ok = ci >= (t // dil)
                n_ok = functools.reduce(jnp.logical_or,
                                        [ni == tn for tn in range(t + 1) if (t - tn) % dil == 0])
                sc = jnp.where(c_ok, sc, -jnp.inf)
                sn = jnp.where(n_ok, sn, -jnp.inf)
                m = jnp.maximum(jnp.max(sc, axis=0, keepdims=True), jnp.max(sn, axis=0, keepdims=True))
                pc = jnp.exp(sc - m)
                pn = jnp.exp(sn - m)
                den = jnp.sum(pc, axis=0, keepdims=True) + jnp.sum(pn, axis=0, keepdims=True)
                o = (jnp.sum(pc * vc, axis=0, keepdims=True) + jnp.sum(pn * vn, axis=0, keepdims=True)) / den
                o_refs[g][0, t:t + 1, ks] = o
                l_refs[g][0, t:t + 1, ks] = jnp.broadcast_to(m + jnp.log(den), (1, ATT_HEAD))


def _attn_sample(p3, caches):
    b, ts, _ = p3.shape
    dils = tuple(d for _, d in DILATION_GROUPS)
    kv_base = (T_KV * PROJ_TN) // (2 * ATT_OUT)
    in_specs, args = [], []
    for g in range(N_GROUPS):
        in_specs.append(pl.BlockSpec((1, ts, ATT_OUT), lambda bi, g=g: (bi, 0, T_Q + g)))
        args.append(p3)
    for g in range(N_GROUPS):
        in_specs.append(pl.BlockSpec((1, ts, 2 * ATT_OUT), lambda bi, g=g: (bi, 0, kv_base + g)))
        args.append(p3)
    for g, (w, dil) in enumerate(DILATION_GROUPS):
        cv = caches[g].reshape(b, w // dil, dil * 2 * ATT_OUT)
        ncls = min(dil, ts)
        in_specs.append(pl.BlockSpec((1, w // dil, ncls * 2 * ATT_OUT), lambda bi: (bi, 0, 0)))
        args.append(cv)
    out_sd = jax.ShapeDtypeStruct((b, ts, ATT_OUT), F32)
    outs = pl.pallas_call(
        functools.partial(_attn_sample_kernel, dils=dils, ts=ts),
        grid=(b,),
        in_specs=in_specs,
        out_specs=[pl.BlockSpec((1, ts, ATT_OUT), lambda bi: (bi, 0, 0))] * 6,
        out_shape=[out_sd] * 6,
        compiler_params=_cparams(("parallel",)),
        name="attn_sample",
    )(*args)
    o = [x.reshape(b * ts, ATT_OUT) for x in outs[:3]]
    l = [x.reshape(b * ts, ATT_OUT) for x in outs[3:]]
    return o, l


def _roll_kernel(c_ref, n_ref, o_ref):
    j = pl.program_id(0)
    last = pl.num_programs(0) - 1

    @pl.when(j < last)
    def _():
        o_ref[...] = c_ref[...]

    @pl.when(j == last)
    def _():
        o_ref[...] = n_ref[...]


def _cache_roll(cache, new_rows):
    b, w, rw = cache.shape
    ts = new_rows.shape[1]
    blk = ts * rw
    nj = w // ts
    cf = cache.reshape(b, w * rw)
    nf = new_rows.reshape(b, blk)
    out = pl.pallas_call(
        _roll_kernel,
        grid=(nj,),
        in_specs=[
            pl.BlockSpec((b, blk), lambda j: (0, jnp.minimum(j + 1, nj - 1))),
            pl.BlockSpec((b, blk), lambda j: (0, 0)),
        ],
        out_specs=pl.BlockSpec((b, blk), lambda j: (0, j)),
        out_shape=jax.ShapeDtypeStruct((b, w * rw), F32),
        compiler_params=_cparams(("arbitrary",)),
        name="cache_roll",
    )(cf, nf)
    return out.reshape(b, w, rw)


def _merge_kernel(x_ref, yrw_ref, gate_ref, o0_ref, o1_ref, o2_ref, l0_ref, l1_ref, l2_ref,
                  wba_ref, wbb_ref, wo_ref, fnw_ref, rw_ref, rb_ref,
                  h_ref, hn_ref, ti_ref, tg_ref):
    l0, l1, l2 = l0_ref[...], l1_ref[...], l2_ref[...]
    m = jnp.maximum(jnp.maximum(l0, l1), l2)
    e0, e1, e2 = jnp.exp(l0 - m), jnp.exp(l1 - m), jnp.exp(l2 - m)
    o_att = (e0 * o0_ref[...] + e1 * o1_ref[...] + e2 * o2_ref[...]) / (e0 + e1 + e2)
    br_a = jnp.dot(yrw_ref[...], wba_ref[...], preferred_element_type=F32)
    br_b = jnp.dot(o_att.astype(BF16), wbb_ref[...], preferred_element_type=F32)
    merged = gate_ref[:, 0:D_MODEL] * br_a + gate_ref[:, D_MODEL:2 * D_MODEL] * br_b
    h = x_ref[...] + jnp.dot(merged.astype(BF16), wo_ref[...], preferred_element_type=F32)
    h_ref[...] = h
    hn = h * lax.rsqrt(jnp.mean(h * h, axis=-1, keepdims=True) + NORM_EPS) * fnw_ref[...]
    hn_ref[...] = hn
    logits = jnp.dot(hn.astype(BF16), rw_ref[...], preferred_element_type=F32) + rb_ref[...]
    lane_i = lax.broadcasted_iota(jnp.int32, logits.shape, 1)
    lane = lane_i.astype(F32)
    vals, idxs = [], []
    cur = logits
    for _ in range(TOP_K):
        mx = jnp.max(cur, axis=-1, keepdims=True)
        ix = jnp.min(jnp.where(cur == mx, lane, float(LANES)), axis=-1, keepdims=True)
        vals.append(mx)
        idxs.append(ix)
        cur = jnp.where(lane == ix, -jnp.inf, cur)
    es = [jnp.exp(vv - vals[0]) for vv in vals]
    tot = es[0] + es[1] + es[2] + es[3]
    ti = jnp.zeros(logits.shape, F32)
    tg = jnp.zeros(logits.shape, F32)
    for kq in range(TOP_K):
        ti = jnp.where(lane_i == kq, idxs[kq], ti)
        tg = jnp.where(lane_i == kq, es[kq] / tot, tg)
    ti_ref[...] = ti.astype(jnp.int32)
    tg_ref[...] = tg


def _merge(x2d, yrw, p2d, o_list, l_list, wba, wbb, wo, fnw, rw_pad, rb_pad, *, tm):
    n = x2d.shape[0]
    row = lambda wdt: pl.BlockSpec((tm, wdt), lambda i: (i, 0))
    full = lambda a: pl.BlockSpec(a.shape, lambda i: (0,) * a.ndim)
    gate_blk = (T_GATE * PROJ_TN) // (2 * D_MODEL)
    return pl.pallas_call(
        _merge_kernel,
        grid=(n // tm,),
        in_specs=[row(D_MODEL), row(D_MODEL),
                  pl.BlockSpec((tm, 2 * D_MODEL), lambda i: (i, gate_blk)),
                  row(ATT_OUT), row(ATT_OUT), row(ATT_OUT), row(ATT_OUT), row(ATT_OUT), row(ATT_OUT),
                  full(wba), full(wbb), full(wo), full(fnw), full(rw_pad), full(rb_pad)],
        out_specs=[row(D_MODEL), row(D_MODEL), row(LANES), row(LANES)],
        out_shape=[jax.ShapeDtypeStruct((n, D_MODEL), F32), jax.ShapeDtypeStruct((n, D_MODEL), F32),
                   jax.ShapeDtypeStruct((n, LANES), jnp.int32), jax.ShapeDtypeStruct((n, LANES), F32)],
        compiler_params=_cparams(("parallel",)),
        name="merge",
    )(x2d, yrw, p2d, *o_list, *l_list, wba, wbb, wo, fnw, rw_pad, rb_pad)


def _row_copy(src_ref, s_row, dst_ref, d_row, sem):
    return pltpu.make_async_copy(src_ref.at[pl.ds(s_row, 1), :], dst_ref.at[pl.ds(d_row, 1), :], sem)


def _dispatch_kernel(pad_start_ref, pad_len_ref, pos_ref, hn_ref, xs_ref, zrow_ref, sem, *, tt):
    i = pl.program_id(0)

    @pl.when(i == 0)
    def _():
        zrow_ref[...] = jnp.zeros(zrow_ref.shape, F32)

        def per_expert(e, carry):
            s = pad_start_ref[e]
            cnt = pad_len_ref[e]

            def issue(rr, cc):
                _row_copy(zrow_ref, 0, xs_ref, s + rr, sem).start()
                return cc

            def drain(rr, cc):
                _row_copy(zrow_ref, 0, xs_ref, s + rr, sem).wait()
                return cc

            lax.fori_loop(0, cnt, issue, 0)
            lax.fori_loop(0, cnt, drain, 0)
            return carry

        lax.fori_loop(0, N_EXPERTS, per_expert, 0)

    def issue(nn, cc):
        for kq in range(TOP_K):
            _row_copy(hn_ref, nn, xs_ref, pos_ref[0, 0, nn * TOP_K + kq], sem).start()
        return cc

    def drain(nn, cc):
        for kq in range(TOP_K):
            _row_copy(hn_ref, nn, xs_ref, pos_ref[0, 0, nn * TOP_K + kq], sem).wait()
        return cc

    lax.fori_loop(0, tt, issue, 0)
    lax.fori_loop(0, tt, drain, 0)


def _dispatch(hn, pos3, pad_start, pad_len, rows, *, tt):
    n = hn.shape[0]
    return pl.pallas_call(
        functools.partial(_dispatch_kernel, tt=tt),
        grid_spec=pltpu.PrefetchScalarGridSpec(
            num_scalar_prefetch=2,
            grid=(n // tt,),
            in_specs=[
                pl.BlockSpec((1, 1, tt * TOP_K), lambda i, ps, plen: (i, 0, 0), memory_space=pltpu.SMEM),
                pl.BlockSpec((tt, D_MODEL), lambda i, ps, plen: (i, 0)),
            ],
            out_specs=pl.BlockSpec(memory_space=pl.ANY),
            scratch_shapes=[pltpu.VMEM((SUBLANES, D_MODEL), F32), pltpu.SemaphoreType.DMA(())],
        ),
        out_shape=jax.ShapeDtypeStruct((rows, D_MODEL), F32),
        compiler_params=_cparams(("arbitrary",)),
        name="moe_dispatch",
    )(pad_start, pad_len, pos3, hn)


def _moe_kernel(be_ref, nu_ref, xs_ref, w1_ref, b1_ref, w2_ref, b2_ref, y_ref):
    i = pl.program_id(0)

    @pl.when(i < nu_ref[0])
    def _():
        x = xs_ref[...].astype(BF16)
        hdn = jnp.dot(x, w1_ref[0], preferred_element_type=F32) + b1_ref[0]
        glu = jnp.minimum(hdn[:, 0:D_MODEL], SWIGLU_LIMIT)
        lin = jnp.clip(hdn[:, D_MODEL:2 * D_MODEL], -SWIGLU_LIMIT, SWIGLU_LIMIT)
        act = glu * jax.nn.sigmoid(SWIGLU_ALPHA * glu) * (lin + 1.0)
        y_ref[...] = jnp.dot(act.astype(BF16), w2_ref[0], preferred_element_type=F32) + b2_ref[0]

    @pl.when(i >= nu_ref[0])
    def _():
        y_ref[...] = jnp.zeros(y_ref.shape, F32)


def _moe(xs, blk_expert, n_used, w1, b1, w2, b2, *, tm):
    rows = xs.shape[0]
    nb = rows // tm
    return pl.pallas_call(
        _moe_kernel,
        grid_spec=pltpu.PrefetchScalarGridSpec(
            num_scalar_prefetch=2,
            grid=(nb,),
            in_specs=[
                pl.BlockSpec((tm, D_MODEL), lambda i, be, nu: (jnp.minimum(i, nu[0] - 1), 0)),
                pl.BlockSpec((1, D_MODEL, 2 * D_MODEL), lambda i, be, nu: (be[i], 0, 0)),
                pl.BlockSpec((1, 1, 2 * D_MODEL), lambda i, be, nu: (be[i], 0, 0)),
                pl.BlockSpec((1, D_MODEL, D_MODEL), lambda i, be, nu: (be[i], 0, 0)),
                pl.BlockSpec((1, 1, D_MODEL), lambda i, be, nu: (be[i], 0, 0)),
            ],
            out_specs=pl.BlockSpec((tm, D_MODEL), lambda i, be, nu: (i, 0)),
        ),
        out_shape=jax.ShapeDtypeStruct((rows, D_MODEL), F32),
        compiler_params=_cparams(("arbitrary",)),
        name="moe_experts",
    )(blk_expert, n_used, xs, w1, b1, w2, b2)


def _combine_kernel(pos_ref, h_ref, tg_ref, yb_ref, o_ref, buf_ref, sem, *, tt):
    def issue(nn, cc):
        for kq in range(TOP_K):
            _row_copy(yb_ref, pos_ref[0, 0, nn * TOP_K + kq], buf_ref.at[kq], nn, sem).start()
        return cc

    def drain(nn, cc):
        for kq in range(TOP_K):
            _row_copy(yb_ref, pos_ref[0, 0, nn * TOP_K + kq], buf_ref.at[kq], nn, sem).wait()
        return cc

    lax.fori_loop(0, tt, issue, 0)
    lax.fori_loop(0, tt, drain, 0)
    acc = h_ref[...]
    for kq in range(TOP_K):
        acc = acc + tg_ref[:, kq:kq + 1] * buf_ref[kq]
    o_ref[...] = acc


def _combine(h, tg, pos3, yb, *, tt):
    n = h.shape[0]
    return pl.pallas_call(
        functools.partial(_combine_kernel, tt=tt),
        grid=(n // tt,),
        in_specs=[
            pl.BlockSpec((1, 1, tt * TOP_K), lambda i: (i, 0, 0), memory_space=pltpu.SMEM),
            pl.BlockSpec((tt, D_MODEL), lambda i: (i, 0)),
            pl.BlockSpec((tt, LANES), lambda i: (i, 0)),
            pl.BlockSpec(memory_space=pl.ANY),
        ],
        out_specs=pl.BlockSpec((tt, D_MODEL), lambda i: (i, 0)),
        out_shape=jax.ShapeDtypeStruct((n, D_MODEL), F32),
        scratch_shapes=[pltpu.VMEM((TOP_K, tt, D_MODEL), F32), pltpu.SemaphoreType.DMA(())],
        compiler_params=_cparams(("arbitrary",)),
        name="moe_combine",
    )(pos3, h, tg, yb)


def _routing(top_idx, tm):
    n = top_idx.shape[0]
    nk = n * TOP_K
    flat_e = top_idx.reshape(nk)
    onehot = (flat_e[:, None] == jnp.arange(N_EXPERTS, dtype=jnp.int32)[None, :]).astype(jnp.int32)
    csum = jnp.cumsum(onehot, axis=0)
    counts = csum[-1]
    rank = jnp.sum(csum * onehot, axis=1) - 1
    blocks_e = (counts + tm - 1) // tm
    block_end = jnp.cumsum(blocks_e)
    row_start = (block_end - blocks_e) * tm
    pos = (row_start[flat_e] + rank).astype(jnp.int32)
    n_blocks = -(-nk // tm) + N_EXPERTS
    blk_expert = jnp.minimum(
        jnp.searchsorted(block_end, jnp.arange(n_blocks, dtype=jnp.int32), side='right'), N_EXPERTS - 1
    ).astype(jnp.int32)
    n_used = block_end[-1:].astype(jnp.int32)
    pad_start = (row_start + counts).astype(jnp.int32)
    pad_len = (blocks_e * tm - counts).astype(jnp.int32)
    return pos, blk_expert, n_used, pad_start, pad_len, n_blocks * tm


def _pad_cols(a, width):
    return jnp.pad(a, ((0, 0), (0, width - a.shape[1])))


def _pad_rows(a, height):
    return jnp.pad(a, ((0, height - a.shape[0]), (0, 0)))


def _proj_columns(a):
    d = D_MODEL
    xw = a[:, 3 * d:3 * d + DECAY_LORA]
    xa = a[:, 3 * d + DECAY_LORA:3 * d + DECAY_LORA + AAA_LORA]
    xg = a[:, 3 * d + DECAY_LORA + AAA_LORA:C_RW]
    att = a[:, C_RW:C_RW + 3 * ATT_DIM]
    q, k, v = att[:, :ATT_DIM], att[:, ATT_DIM:2 * ATT_DIM], att[:, 2 * ATT_DIM:]
    parts = [a[:, :3 * d], _pad_cols(xw, LANES), _pad_cols(xa, LANES), _pad_cols(xg, 2 * LANES), q]
    for g in range(N_GROUPS):
        parts.append(k[:, g * ATT_OUT:(g + 1) * ATT_OUT])
        parts.append(v[:, g * ATT_OUT:(g + 1) * ATT_OUT])
    parts.append(a[:, C_RW + 3 * ATT_DIM:])
    return jnp.concatenate(parts, axis=1)


def _rope_tables(pos):
    inv = ROPE_THETA ** (-jnp.arange(0, ATT_HEAD, 2, dtype=F32) / ATT_HEAD)
    ang = pos.astype(F32)[:, None] * inv[None, :]
    cos, sin = jnp.cos(ang), jnp.sin(ang)
    return jnp.concatenate([cos, cos], axis=1), jnp.concatenate([-sin, sin], axis=1)


def _prep_weights(attn_norm_w, w_in, rw_mu, rw_w0, rw_w2, rw_a0, rw_a2, rw_g2, rw_kk, rw_ka, rw_rk,
                  rw_lnx_w, rw_lnx_b, q_norm_w, k_norm_w, w_br, w_out, ffn_norm_w, router_w, router_b,
                  moe_w1, moe_b1, moe_w2, moe_b2):
    w = {}
    w['nw'] = attn_norm_w.reshape(1, D_MODEL)
    w['w_pad'] = _proj_columns(w_in).astype(BF16)
    w['mu_pad'] = _proj_columns(_pad_cols(rw_mu.reshape(1, C_RW), C_RW + 3 * ATT_DIM + 2 * D_MODEL))[:, :RW_PAD]
    w['par'] = jnp.concatenate([rw_w0.reshape(1, -1), rw_a0.reshape(1, -1), rw_kk.reshape(1, -1),
                                rw_ka.reshape(1, -1), rw_rk.reshape(1, -1), rw_lnx_w.reshape(1, -1),
                                rw_lnx_b.reshape(1, -1), jnp.zeros((1, D_MODEL), F32)], axis=0)
    w['w2p'] = _pad_rows(rw_w2, LANES).astype(BF16)
    w['a2p'] = _pad_rows(rw_a2, LANES).astype(BF16)
    w['g2p'] = _pad_rows(rw_g2, 2 * LANES).astype(BF16)
    w['qn'] = q_norm_w.reshape(1, ATT_HEAD)
    w['kn'] = k_norm_w.reshape(1, ATT_HEAD)
    w['wba'] = w_br[:D_MODEL].astype(BF16)
    w['wbb'] = w_br[D_MODEL:].astype(BF16)
    w['wo'] = w_out.astype(BF16)
    w['fnw'] = ffn_norm_w.reshape(1, D_MODEL)
    w['rw_pad'] = _pad_cols(router_w, LANES).astype(BF16)
    w['rb_pad'] = jnp.concatenate([router_b.reshape(1, N_EXPERTS),
                                   jnp.full((1, LANES - N_EXPERTS), -1e30, F32)], axis=1)
    w['w1'] = jnp.concatenate([moe_w1[:, :, 0::2], moe_w1[:, :, 1::2]], axis=2).astype(BF16)
    w['b1'] = jnp.concatenate([moe_b1[:, 0::2], moe_b1[:, 1::2]], axis=1).reshape(N_EXPERTS, 1, 2 * D_MODEL)
    w['w2'] = moe_w2.astype(BF16)
    w['b2'] = moe_b2.reshape(N_EXPERTS, 1, D_MODEL)
    return w


def _state_to_pairs(s):
    b = s.shape[0]
    sr = s.reshape(b, RWKV_HEADS // 2, 2, RWKV_HEAD, RWKV_HEAD)
    z = jnp.zeros_like(sr[:, :, 0])
    top = jnp.concatenate([sr[:, :, 0], z], axis=-1)
    bot = jnp.concatenate([z, sr[:, :, 1]], axis=-1)
    return jnp.concatenate([top, bot], axis=-2)


def _pairs_to_state(zp):
    b = zp.shape[0]
    zr = zp.reshape(b, RWKV_HEADS // 2, 2, RWKV_HEAD, 2, RWKV_HEAD)
    return jnp.stack([zr[:, :, 0, :, 0, :], zr[:, :, 1, :, 1, :]], axis=2).reshape(b, RWKV_HEADS, RWKV_HEAD, RWKV_HEAD)


def _layer(x, pos, shift_prev, wkv0, caches, w, *, proj_tm, merge_tm, moe_tm, tok_tt):
    b, t, _ = x.shape
    n = b * t
    x2d = x.reshape(n, D_MODEL)
    is_prompt = caches is None

    cs_tab, sn_tab = _rope_tables(pos)
    if not is_prompt:
        cs_tab = jnp.tile(cs_tab, (proj_tm // t, 1))
        sn_tab = jnp.tile(sn_tab, (proj_tm // t, 1))
    p2d = _proj(x2d, w['nw'], w['w_pad'], w['qn'], w['kn'], cs_tab, sn_tab,
                tm=proj_tm, n_tiles=N_TILES, do_norm=True)
    p3 = p2d.reshape(b, t, P_PAD)
    shift_out = _rmsnorm_rows(x[:, t - 1, :], w['nw'])

    if is_prompt:
        prev0 = jnp.zeros((b, 1, RW_PAD), F32)
        s0z = jnp.zeros((b, RWKV_HEADS // 2, LANES, LANES), F32)
        yrw, sfin = _rwkv(p3, prev0, s0z, w['mu_pad'], w['par'], w['w2p'], w['a2p'], w['g2p'],
                          chunk=SCAN_CHUNK, n_valid=None)
        yrw = yrw.reshape(n, D_MODEL)
    else:
        one = jnp.ones((1, ATT_HEAD), F32)
        zero_tab = jnp.zeros((b, ATT_HEAD), F32)
        prev_rw = _proj(shift_prev, w['nw'], w['w_pad'], one, one, zero_tab, zero_tab,
                        tm=b, n_tiles=T_LORA + 1, do_norm=False)
        t_pad = SCAN_CHUNK
        p_rw = jnp.pad(p3[:, :, :RW_PAD], ((0, 0), (0, t_pad - t), (0, 0)))
        yrw, sfin = _rwkv(p_rw, prev_rw.reshape(b, 1, RW_PAD), _state_to_pairs(wkv0), w['mu_pad'], w['par'],
                          w['w2p'], w['a2p'], w['g2p'], chunk=SCAN_CHUNK, n_valid=t)
        yrw = yrw[:, :t].reshape(n, D_MODEL)
    s_fin = _pairs_to_state(sfin)

    kv_col = T_KV * PROJ_TN
    new_kv = []
    if is_prompt:
        o_list, l_list = [], []
        for g, (window, dil) in enumerate(DILATION_GROUPS):
            o_g, l_g = _attn_prompt(p3, g, dil, window // dil)
            o_list.append(o_g)
            l_list.append(l_g)
            keep = min(window, t)
            kvg = p3[:, t - keep:, kv_col + g * 2 * ATT_OUT:kv_col + (g + 1) * 2 * ATT_OUT]
            new_kv.append(kvg.reshape(b, keep, 2, ATT_HPG, ATT_HEAD))
    else:
        cache2d = [c.reshape(b, c.shape[1], 2 * ATT_OUT) for c in caches]
        o_list, l_list = _attn_sample(p3, cache2d)
        for g in range(N_GROUPS):
            kvg = p3[:, :, kv_col + g * 2 * ATT_OUT:kv_col + (g + 1) * 2 * ATT_OUT]
            rolled = _cache_roll(cache2d[g], kvg)
            new_kv.append(rolled.reshape(caches[g].shape))

    h, hn, ti, tg = _merge(x2d, yrw, p2d, o_list, l_list, w['wba'], w['wbb'], w['wo'], w['fnw'],
                           w['rw_pad'], w['rb_pad'], tm=merge_tm)

    pos_rows, blk_expert, n_used, pad_start, pad_len, rows = _routing(ti[:, :TOP_K], moe_tm)
    pos3 = pos_rows.reshape(n // tok_tt, 1, tok_tt * TOP_K)
    xs = _dispatch(hn, pos3, pad_start, pad_len, rows, tt=tok_tt)
    yb = _moe(xs, blk_expert, n_used, w['w1'], w['b1'], w['w2'], w['b2'], tm=moe_tm)
    y = _combine(h, tg, pos3, yb, tt=tok_tt)
    return y.reshape(b, t, D_MODEL), shift_out, s_fin, new_kv


def kernel(x_prompt, x_sample, state_wkv, state_shift, cache_kv_w128, cache_kv_w512, cache_kv_w2048,
           attn_norm_w, w_in, rw_mu, rw_w0, rw_w2, rw_a0, rw_a2, rw_g2, rw_kk, rw_ka, rw_rk,
           rw_lnx_w, rw_lnx_b, q_norm_w, k_norm_w, w_br, w_out, ffn_norm_w, router_w, router_b,
           moe_w1, moe_b1, moe_w2, moe_b2):
    depth = w_in.shape[0]
    assert depth == 1
    bp, tp, _ = x_prompt.shape
    bs, ts, _ = x_sample.shape
    l = 0
    w = _prep_weights(attn_norm_w[l], w_in[l], rw_mu[l], rw_w0[l], rw_w2[l], rw_a0[l], rw_a2[l], rw_g2[l],
                      rw_kk[l], rw_ka[l], rw_rk[l], rw_lnx_w[l], rw_lnx_b[l], q_norm_w[l], k_norm_w[l],
                      w_br[l], w_out[l], ffn_norm_w[l], router_w[l], router_b[l],
                      moe_w1[l], moe_b1[l], moe_w2[l], moe_b2[l])
    pos_p = jnp.arange(tp, dtype=jnp.int32)
    pos_s = PAST_LEN + jnp.arange(ts, dtype=jnp.int32)
    yp, sh_p, wkv_p, kv_p = _layer(x_prompt, pos_p, None, None, None, w,
                                   proj_tm=min(1024, tp), merge_tm=256, moe_tm=MOE_TM, tok_tt=256)
    ns = bs * ts
    ys, sh_s, wkv_s, kv_s = _layer(x_sample, pos_s, state_shift[l], state_wkv[l],
                                   (cache_kv_w128[l], cache_kv_w512[l], cache_kv_w2048[l]), w,
                                   proj_tm=ns, merge_tm=min(256, ns), moe_tm=128, tok_tt=min(256, ns))
    return (yp, ys,
            wkv_p[None], sh_p[None], kv_p[0][None], kv_p[1][None], kv_p[2][None],
            wkv_s[None], sh_s[None], kv_s[0][None], kv_s[1][None], kv_s[2][None])
```

```python
import functools

import jax
import jax.numpy as jnp
from jax import lax
from jax.experimental import pallas as pl
from jax.experimental.pallas import tpu as pltpu

F32 = jnp.float32
BF16 = jnp.bfloat16

D_MODEL = 1024
NORM_EPS = 1e-5
RWKV_HEAD = 64
RWKV_HEADS = D_MODEL // RWKV_HEAD
DECAY_LORA = 64
AAA_LORA = 64
GATE_LORA = 160
GN_EPS = 64e-5
C_RW = 3 * D_MODEL + DECAY_LORA + AAA_LORA + GATE_LORA
ATT_HEAD = 128
ATT_HPG = 4
DILATION_GROUPS = ((128, 1), (512, 4), (2048, 16))
N_GROUPS = 3
ATT_DIM = N_GROUPS * ATT_HPG * ATT_HEAD
ATT_OUT = ATT_HPG * ATT_HEAD
ROPE_THETA = 10000.0
N_EXPERTS = 32
TOP_K = 4
SWIGLU_ALPHA = 1.702
SWIGLU_LIMIT = 7.0
PAST_LEN = 8192

LANES = 128
SUBLANES = 8
VMEM_LIMIT = 56 * 1024 * 1024

PROJ_TN = 512
T_RKV = 0
T_LORA = 6
T_Q = 7
T_KV = 10
T_GATE = 16
N_TILES = 20
P_PAD = N_TILES * PROJ_TN
RW_PAD = (T_LORA + 1) * PROJ_TN

SCAN_CHUNK = 64
MOE_TM = 512


def _cparams(sem, vmem=VMEM_LIMIT):
    return pltpu.CompilerParams(dimension_semantics=sem, vmem_limit_bytes=vmem)


def _proj_kernel(x_ref, nw_ref, w_ref, qn_ref, kn_ref, cs_ref, sn_ref, o_ref, xn_ref, *, do_norm):
    j = pl.program_id(1)

    @pl.when(j == 0)
    def _():
        x = x_ref[...]
        if do_norm:
            x = x * lax.rsqrt(jnp.mean(x * x, axis=-1, keepdims=True) + NORM_EPS) * nw_ref[...]
        xn_ref[...] = x.astype(BF16)

    acc = jnp.dot(xn_ref[...], w_ref[...], preferred_element_type=F32)
    is_q = (j >= T_Q) & (j < T_KV)
    is_k = (j >= T_KV) & (j < T_GATE) & ((j - T_KV) % 2 == 0)
    is_qk = is_q | is_k
    is_gate = j >= T_GATE

    @pl.when(is_qk)
    def _():
        nwh = jnp.where(j < T_KV, qn_ref[...], kn_ref[...])
        cs = cs_ref[...]
        sn = sn_ref[...]
        for h in range(ATT_HPG):
            xh = acc[:, h * ATT_HEAD:(h + 1) * ATT_HEAD]
            y = xh * lax.rsqrt(jnp.mean(xh * xh, axis=-1, keepdims=True) + NORM_EPS) * nwh
            o_ref[:, h * ATT_HEAD:(h + 1) * ATT_HEAD] = y * cs + pltpu.roll(y, ATT_HEAD // 2, axis=1) * sn

    @pl.when(is_gate)
    def _():
        o_ref[...] = jax.nn.sigmoid(acc)

    @pl.when(jnp.logical_not(is_qk | is_gate))
    def _():
        o_ref[...] = acc


def _proj(x2d, nw, w_pad, qn, kn, cs_tab, sn_tab, *, tm, n_tiles, do_norm):
    n = x2d.shape[0]
    tab_blocks = cs_tab.shape[0] // tm
    return pl.pallas_call(
        functools.partial(_proj_kernel, do_norm=do_norm),
        grid=(n // tm, n_tiles),
        in_specs=[
            pl.BlockSpec((tm, D_MODEL), lambda i, j: (i, 0)),
            pl.BlockSpec((1, D_MODEL), lambda i, j: (0, 0)),
            pl.BlockSpec((D_MODEL, PROJ_TN), lambda i, j: (0, j)),
            pl.BlockSpec((1, ATT_HEAD), lambda i, j: (0, 0)),
            pl.BlockSpec((1, ATT_HEAD), lambda i, j: (0, 0)),
            pl.BlockSpec((tm, ATT_HEAD), lambda i, j: (i % tab_blocks, 0)),
            pl.BlockSpec((tm, ATT_HEAD), lambda i, j: (i % tab_blocks, 0)),
        ],
        out_specs=pl.BlockSpec((tm, PROJ_TN), lambda i, j: (i, j)),
        out_shape=jax.ShapeDtypeStruct((n, n_tiles * PROJ_TN), F32),
        scratch_shapes=[pltpu.VMEM((tm, D_MODEL), BF16)],
        compiler_params=_cparams(("parallel", "arbitrary")),
        name="proj",
    )(x2d, nw, w_pad, qn, kn, cs_tab, sn_tab)


def _rmsnorm_kernel(x_ref, w_ref, o_ref):
    x = x_ref[...]
    o_ref[...] = x * lax.rsqrt(jnp.mean(x * x, axis=-1, keepdims=True) + NORM_EPS) * w_ref[...]


def _rmsnorm_rows(x2d, w):
    return pl.pallas_call(
        _rmsnorm_kernel,
        out_shape=jax.ShapeDtypeStruct(x2d.shape, F32),
        name="rmsnorm_rows",
    )(x2d, w)


def _split3(x):
    h1 = x.astype(BF16)
    r1 = x - h1.astype(F32)
    h2 = r1.astype(BF16)
    h3 = (r1 - h2.astype(F32)).astype(BF16)
    return h1, h2, h3


def _seg_sum(x, seg_ones):
    rows = x.shape[0]
    hi = x.astype(BF16)
    lo = (x - hi.astype(F32)).astype(BF16)
    hl = jnp.concatenate([hi, lo], axis=0)
    outs = []
    for p in range(D_MODEL // LANES):
        s = jnp.dot(hl[:, p * LANES:(p + 1) * LANES], seg_ones, preferred_element_type=F32)
        outs.append(s[:rows] + s[rows:])
    return jnp.concatenate(outs, axis=1)


def _nt_dot(a, b):
    return lax.dot_general(a, b, (((1,), (1,)), ((), ())), preferred_element_type=F32)


def _rwkv_kernel(p_ref, prev0_ref, s0_ref, mu_ref, par_ref, w2_ref, a2_ref, g2_ref,
                 y_ref, st_ref, prev_ref, *, chunk, n_valid):
    c = pl.program_id(1)
    C = chunk

    @pl.when(c == 0)
    def _():
        st_ref[...] = s0_ref[...]
        prev_ref[...] = prev0_ref[0]

    p = p_ref[0]
    row = lax.broadcasted_iota(jnp.int32, (C, 1), 0)
    shifted = jnp.where(row == 0, prev_ref[...], pltpu.roll(p, 1, axis=0))
    prev_ref[...] = p[C - 1:C, :]
    ps = p + (shifted - p) * mu_ref[...]

    w0 = par_ref[0:1, :]
    a0 = par_ref[1:2, :]
    k_k = par_ref[2:3, :]
    k_a = par_ref[3:4, :]
    r_k = par_ref[4:5, :]
    lnx_w = par_ref[5:6, :]
    lnx_b = par_ref[6:7, :]

    r = ps[:, 0:D_MODEL]
    k = ps[:, D_MODEL:2 * D_MODEL]
    v = ps[:, 2 * D_MODEL:3 * D_MODEL]
    xw = ps[:, 3 * D_MODEL:3 * D_MODEL + LANES]
    xa = ps[:, 3 * D_MODEL + LANES:3 * D_MODEL + 2 * LANES]
    xg = ps[:, 3 * D_MODEL + 2 * LANES:RW_PAD]

    zw = w0 + jnp.dot(jnp.tanh(xw).astype(BF16), w2_ref[...], preferred_element_type=F32)
    nz = -zw
    softplus = jnp.log(1.0 + jnp.exp(-jnp.abs(nz))) + jnp.maximum(nz, 0.0)
    lw = -jnp.exp(-softplus - 0.5)
    a = jax.nn.sigmoid(a0 + jnp.dot(xa.astype(BF16), a2_ref[...], preferred_element_type=F32))
    g = jnp.dot(jax.nn.sigmoid(xg).astype(BF16), g2_ref[...], preferred_element_type=F32)

    li = lax.broadcasted_iota(jnp.int32, (LANES, LANES), 0)
    lj = lax.broadcasted_iota(jnp.int32, (LANES, LANES), 1)
    same_head = (li < RWKV_HEAD) == (lj < RWKV_HEAD)
    seg_ones = jnp.where(same_head, 1.0, 0.0).astype(BF16)
    blk_mask = jnp.where(same_head, 1.0, 0.0)

    kkr = k * k_k
    kk = kkr / jnp.maximum(jnp.sqrt(_seg_sum(kkr * kkr, seg_ones)), 1e-12)
    k2 = k * (1.0 + (a - 1.0) * k_a)
    bonus = _seg_sum(r * k2 * r_k, seg_ones) * v

    if n_valid is not None:
        valid = (c * C + row) < n_valid
        lw = jnp.where(valid, lw, 0.0)
        kk = jnp.where(valid, kk, 0.0)
        k2 = jnp.where(valid, k2, 0.0)

    ti = lax.broadcasted_iota(jnp.int32, (C, C), 0)
    tj = lax.broadcasted_iota(jnp.int32, (C, C), 1)
    tril_incl = jnp.where(ti >= tj, 1.0, 0.0)
    tril_strict = jnp.where(ti > tj, 1.0, 0.0)
    tril_b = tril_incl.astype(BF16)
    l1, l2, l3 = _split3(lw)
    cl = (jnp.dot(tril_b, l1, preferred_element_type=F32)
          + jnp.dot(tril_b, l2, preferred_element_type=F32)
          + jnp.dot(tril_b, l3, preferred_element_type=F32))
    cl_end = cl[C - 1:C, :]
    kb = kk * a
    e_inv = jnp.exp(-cl)
    e_tail = jnp.exp(cl_end - cl)
    a_t = (-kk) * jnp.exp(cl - lw)
    r_t = r * jnp.exp(cl)
    b_t = kb * e_inv
    k_t = k2 * e_inv
    b_h = kb * e_tail
    k_h = k2 * e_tail
    w_end = jnp.exp(cl_end)

    lane = lax.broadcasted_iota(jnp.int32, (1, LANES), 1)
    first = lane < RWKV_HEAD
    n_lev = max(1, (C - 1).bit_length())

    pairs = range(D_MODEL // LANES)
    sls = [slice(pr * LANES, (pr + 1) * LANES) for pr in pairs]
    dot = functools.partial(jnp.dot, preferred_element_type=F32)
    head_sel = lambda x0, x1: jnp.where(first, x0, x1)

    zs = [st_ref[0, pr] for pr in pairs]
    ars = [jnp.concatenate([a_t[:, sl], r_t[:, sl]], axis=0) for sl in sls]
    vbs = [v[:, sl].astype(BF16) for sl in sls]
    q0s = [_nt_dot(ars[pr].astype(BF16), zs[pr].astype(BF16)) for pr in pairs]
    lab, lak, mrb, mrk = {}, {}, {}, {}
    for pr in pairs:
        btb = b_t[:, sls[pr]].astype(BF16)
        ktb = k_t[:, sls[pr]].astype(BF16)
        for h in range(2):
            mh = first if h == 0 else jnp.logical_not(first)
            arh = jnp.where(mh, ars[pr], 0.0).astype(BF16)
            gb = _nt_dot(arh, btb)
            gk = _nt_dot(arh, ktb)
            lab[pr, h] = (gb[:C] * tril_strict).astype(BF16)
            lak[pr, h] = (gk[:C] * tril_strict).astype(BF16)
            mrb[pr, h] = (gb[C:] * tril_incl).astype(BF16)
            mrk[pr, h] = (gk[C:] * tril_incl).astype(BF16)
    us = [q0s[pr][:C] + head_sel(dot(lak[pr, 0], vbs[pr]), dot(lak[pr, 1], vbs[pr])) for pr in pairs]
    xs = lab
    for lev in range(n_lev):
        ubs = [us[pr].astype(BF16) for pr in pairs]
        us = [us[pr] + head_sel(dot(xs[pr, 0], ubs[pr]), dot(xs[pr, 1], ubs[pr])) for pr in pairs]
        if lev < n_lev - 1:
            xs = {key: dot(xh, xh).astype(BF16) for key, xh in xs.items()}
    ubs = [us[pr].astype(BF16) for pr in pairs]
    y_tiles = [q0s[pr][C:] + head_sel(dot(mrb[pr, 0], ubs[pr]) + dot(mrk[pr, 0], vbs[pr]),
                                      dot(mrb[pr, 1], ubs[pr]) + dot(mrk[pr, 1], vbs[pr])) for pr in pairs]
    for pr in pairs:
        sl = sls[pr]
        uv_t = jnp.concatenate([us[pr], v[:, sl]], axis=0).T.astype(BF16)
        bk_h = jnp.concatenate([b_h[:, sl], k_h[:, sl]], axis=0).astype(BF16)
        st_ref[0, pr] = zs[pr] * w_end[:, sl] + blk_mask * dot(uv_t, bk_h)

    y = jnp.concatenate(y_tiles, axis=1)
    inv_n = 1.0 / RWKV_HEAD
    mean = _seg_sum(y, seg_ones) * inv_n
    yc = y - mean
    var = _seg_sum(yc * yc, seg_ones) * inv_n
    yn = yc * lax.rsqrt(var + GN_EPS) * lnx_w + lnx_b
    y_ref[0] = ((yn + bonus) * g).astype(y_ref.dtype)


def _rwkv(p3, prev0, s0z, mu_pad, par, w2p, a2p, g2p, *, chunk, n_valid):
    b, t, _ = p3.shape
    n_pairs = D_MODEL // LANES
    full = lambda shape: pl.BlockSpec(shape, lambda i, c: (0,) * len(shape))
    return pl.pallas_call(
        functools.partial(_rwkv_kernel, chunk=chunk, n_valid=n_valid),
        grid=(b, t // chunk),
        in_specs=[
            pl.BlockSpec((1, chunk, RW_PAD), lambda i, c: (i, c, 0)),
            pl.BlockSpec((1, 1, RW_PAD), lambda i, c: (i, 0, 0)),
            pl.BlockSpec((1, n_pairs, LANES, LANES), lambda i, c: (i, 0, 0, 0)),
            full((1, RW_PAD)),
            full((SUBLANES, D_MODEL)),
            full((LANES, D_MODEL)),
            full((LANES, D_MODEL)),
            full((2 * LANES, D_MODEL)),
        ],
        out_specs=[
            pl.BlockSpec((1, chunk, D_MODEL), lambda i, c: (i, c, 0)),
            pl.BlockSpec((1, n_pairs, LANES, LANES), lambda i, c: (i, 0, 0, 0)),
        ],
        out_shape=[
            jax.ShapeDtypeStruct((b, t, D_MODEL), BF16),
            jax.ShapeDtypeStruct((b, n_pairs, LANES, LANES), F32),
        ],
        scratch_shapes=[pltpu.VMEM((1, RW_PAD), F32)],
        compiler_params=_cparams(("parallel", "arbitrary")),
        name="rwkv",
    )(p3, prev0, s0z, mu_pad, par, w2p, a2p, g2p)


def _attn_kernel(*refs, blk, dil, has_prev):
    if has_prev:
        q_ref, ko_ref, vo_ref, kp_ref, vp_ref, o_ref, l_ref = refs
    else:
        q_ref, ko_ref, vo_ref, o_ref, l_ref = refs
    n = pl.program_id(1)
    nk = 2 * blk if has_prev else blk
    qi = lax.broadcasted_iota(jnp.int32, (blk, nk), 0)
    ki = lax.broadcasted_iota(jnp.int32, (blk, nk), 1)
    if has_prev:
        lo = jnp.where(n == 0, blk, 0)
        valid = (ki >= qi) & (ki <= qi + blk) & (ki >= lo)
    else:
        valid = ki <= qi
    scale = ATT_HEAD ** -0.5
    for r in range(dil):
        rows = pl.ds(r, blk, stride=dil) if dil > 1 else pl.ds(0, blk)
        q = q_ref[0, rows, :].astype(BF16)
        kc = ko_ref[0, rows, :]
        vc = vo_ref[0, rows, :]
        if has_prev:
            kc = jnp.concatenate([kp_ref[0, rows, :], kc], axis=0)
            vc = jnp.concatenate([vp_ref[0, rows, :], vc], axis=0)
        s = _nt_dot(q, kc.astype(BF16)) * scale
        s = jnp.where(valid, s, -jnp.inf)
        m = jnp.max(s, axis=-1, keepdims=True)
        pexp = jnp.exp(s - m)
        den = jnp.sum(pexp, axis=-1, keepdims=True)
        o = jnp.dot(pexp.astype(BF16), vc.astype(BF16), preferred_element_type=F32) / den
        o_ref[0, rows, :] = o
        l_ref[0, rows, :] = jnp.broadcast_to(m + jnp.log(den), (blk, ATT_HEAD))


def _attn_prompt(p3, g, dil, blk):
    b, t, _ = p3.shape
    unit = dil * blk
    assert t % unit == 0
    nb = t // unit
    has_prev = nb > 1
    q_col = (T_Q * PROJ_TN + g * ATT_OUT) // ATT_HEAD
    k_col = (T_KV * PROJ_TN + g * 2 * ATT_OUT) // ATT_HEAD
    v_col = k_col + ATT_HPG
    cur = lambda col: pl.BlockSpec((1, unit, ATT_HEAD), lambda bi, n, h: (bi, n, col + h))
    prev = lambda col: pl.BlockSpec((1, unit, ATT_HEAD), lambda bi, n, h: (bi, jnp.maximum(n - 1, 0), col + h))
    in_specs = [cur(q_col), cur(k_col), cur(v_col)]
    if has_prev:
        in_specs += [prev(k_col), prev(v_col)]
    out_sd = jax.ShapeDtypeStruct((b, t, ATT_OUT), F32)
    out_spec = pl.BlockSpec((1, unit, ATT_HEAD), lambda bi, n, h: (bi, n, h))
    o, l = pl.pallas_call(
        functools.partial(_attn_kernel, blk=blk, dil=dil, has_prev=has_prev),
        grid=(b, nb, ATT_HPG),
        in_specs=in_specs,
        out_specs=[out_spec, out_spec],
        out_shape=[out_sd, out_sd],
        compiler_params=_cparams(("parallel", "parallel", "arbitrary")),
        name=f"attn_prompt_g{g}",
    )(*([p3] * len(in_specs)))
    return o.reshape(b * t, ATT_OUT), l.reshape(b * t, ATT_OUT)


def _kv_rows_kernel(x_ref, o_ref):
    for s in range(2):
        for h in range(ATT_HPG):
            c0 = (s * ATT_HPG + h) * ATT_HEAD
            o_ref[0, :, s, h, :] = x_ref[0, :, c0:c0 + ATT_HEAD]


def _kv_rows(p3, g, keep, tr):
    b, t, _ = p3.shape
    col = (T_KV * PROJ_TN) // (2 * ATT_OUT) + g
    r0 = (t - keep) // tr
    return pl.pallas_call(
        _kv_rows_kernel,
        grid=(b, keep // tr),
        in_specs=[pl.BlockSpec((1, tr, 2 * ATT_OUT), lambda bi, i: (bi, r0 + i, col))],
        out_specs=pl.BlockSpec((1, tr, 2, ATT_HPG, ATT_HEAD), lambda bi, i: (bi, i, 0, 0, 0)),
        out_shape=jax.ShapeDtypeStruct((b, keep, 2, ATT_HPG, ATT_HEAD), F32),
        compiler_params=_cparams(("parallel", "parallel")),
        name=f"kv_rows_g{g}",
    )(p3)


def _attn_sample_kernel(q_ref, n0_ref, n1_ref, n2_ref, c0_ref, c1_ref, c2_ref, o_ref, l_ref, *, dils, ts):
    n_refs = (n0_ref, n1_ref, n2_ref)
    c_refs = (c0_ref, c1_ref, c2_ref)
    scale = ATT_HEAD ** -0.5
    for g in range(N_GROUPS):
        dil = dils[g]
        cache = c_refs[g][0, 0]
        kc, vc = cache[:, 0], cache[:, 1]
        new = n_refs[g][0]
        kn, vn = new[:, 0], new[:, 1]
        w = kc.shape[0]
        wi = lax.broadcasted_iota(jnp.int32, (w, ATT_HPG, 1), 0)
        ni = lax.broadcasted_iota(jnp.int32, (ts, ATT_HPG, 1), 0)
        for t in range(ts):
            q = q_ref[0, t, g]
            sc = jnp.sum(kc * q[None], axis=-1, keepdims=True) * scale
            sn = jnp.sum(kn * q[None], axis=-1, keepdims=True) * scale
            c_ok = ((wi & (dil - 1)) == (t % dil)) & (wi >= t)
            n_ok = functools.reduce(jnp.logical_or, [ni == tn for tn in range(t + 1) if (t - tn) % dil == 0])
            sc = jnp.where(c_ok, sc, -jnp.inf)
            sn = jnp.where(n_ok, sn, -jnp.inf)
            m = jnp.maximum(jnp.max(sc, axis=0, keepdims=True), jnp.max(sn, axis=0, keepdims=True))
            pc = jnp.exp(sc - m)
            pn = jnp.exp(sn - m)
            den = jnp.sum(pc, axis=0, keepdims=True) + jnp.sum(pn, axis=0, keepdims=True)
            o = (jnp.sum(pc * vc, axis=0, keepdims=True) + jnp.sum(pn * vn, axis=0, keepdims=True)) / den
            o_ref[0, t, g] = o[0]
            l_ref[0, t, g] = jnp.broadcast_to((m + jnp.log(den))[0], (ATT_HPG, ATT_HEAD))


def _attn_sample(q5, new_kv, caches):
    b, ts = q5.shape[:2]
    dils = tuple(d for _, d in DILATION_GROUPS)
    in_specs = [pl.BlockSpec((1, ts, N_GROUPS, ATT_HPG, ATT_HEAD), lambda bi: (bi, 0, 0, 0, 0))]
    for g in range(N_GROUPS):
        in_specs.append(pl.BlockSpec((1, ts, 2, ATT_HPG, ATT_HEAD), lambda bi: (bi, 0, 0, 0, 0)))
    for g in range(N_GROUPS):
        w = caches[g].shape[2]
        in_specs.append(pl.BlockSpec((1, 1, w, 2, ATT_HPG, ATT_HEAD), lambda bi: (0, bi, 0, 0, 0, 0)))
    out_sd = jax.ShapeDtypeStruct((b, ts, N_GROUPS, ATT_HPG, ATT_HEAD), F32)
    out_spec = pl.BlockSpec((1, ts, N_GROUPS, ATT_HPG, ATT_HEAD), lambda bi: (bi, 0, 0, 0, 0))
    o, l = pl.pallas_call(
        functools.partial(_attn_sample_kernel, dils=dils, ts=ts),
        grid=(b,),
        in_specs=in_specs,
        out_specs=[out_spec, out_spec],
        out_shape=[out_sd, out_sd],
        compiler_params=_cparams(("parallel",)),
        name="attn_sample",
    )(q5, *new_kv, *caches)
    o_list = [o[:, :, g].reshape(b * ts, ATT_OUT) for g in range(N_GROUPS)]
    l_list = [l[:, :, g].reshape(b * ts, ATT_OUT) for g in range(N_GROUPS)]
    return o_list, l_list


ROLL_SPLIT = 8


def _roll_copies(c_refs, n_refs, o_refs, sems):
    copies = []
    for g, (c_ref, n_ref, o_ref) in enumerate(zip(c_refs, n_refs, o_refs)):
        b, w, ts = c_ref.shape[1], c_ref.shape[2], n_ref.shape[2]
        step = b // ROLL_SPLIT
        for j in range(ROLL_SPLIT):
            bs = pl.ds(j * step, step)
            copies.append(pltpu.make_async_copy(c_ref.at[0, bs, pl.ds(ts, w - ts)],
                                                o_ref.at[0, bs, pl.ds(0, w - ts)], sems.at[g, j]))
        copies.append(pltpu.make_async_copy(n_ref.at[0], o_ref.at[0, :, pl.ds(w - ts, ts)], sems.at[g, ROLL_SPLIT]))
    return copies


def _roll_kernel(c0, c1, c2, n0, n1, n2, o0, o1, o2, sems):
    copies = _roll_copies((c0, c1, c2), (n0, n1, n2), (o0, o1, o2), sems)
    for cp in copies:
        cp.start()
    for cp in copies:
        cp.wait()


def _cache_roll(caches, new_kv):
    assert all(c.shape[1] % ROLL_SPLIT == 0 for c in caches)
    any_spec = pl.BlockSpec(memory_space=pl.ANY)
    return pl.pallas_call(
        _roll_kernel,
        in_specs=[any_spec] * 6,
        out_specs=[any_spec] * 3,
        out_shape=[jax.ShapeDtypeStruct(c.shape, c.dtype) for c in caches],
        scratch_shapes=[pltpu.SemaphoreType.DMA((N_GROUPS, ROLL_SPLIT + 1))],
        name="cache_roll",
    )(*caches, *new_kv)


def _merge_kernel(x_ref, yrw_ref, gate_ref, o0_ref, o1_ref, o2_ref, l0_ref, l1_ref, l2_ref,
                  wba_ref, wbb_ref, wo_ref, fnw_ref, rw_ref, rb_ref,
                  h_ref, hn_ref, ti_ref, tg_ref, cnt_ref, run_ref):
    @pl.when(pl.program_id(0) == 0)
    def _():
        run_ref[...] = jnp.zeros(run_ref.shape, F32)

    l0, l1, l2 = l0_ref[...], l1_ref[...], l2_ref[...]
    m = jnp.maximum(jnp.maximum(l0, l1), l2)
    e0, e1, e2 = jnp.exp(l0 - m), jnp.exp(l1 - m), jnp.exp(l2 - m)
    o_att = (e0 * o0_ref[...] + e1 * o1_ref[...] + e2 * o2_ref[...]) / (e0 + e1 + e2)
    br_a = jnp.dot(yrw_ref[...], wba_ref[...], preferred_element_type=F32)
    br_b = jnp.dot(o_att.astype(BF16), wbb_ref[...], preferred_element_type=F32)
    merged = gate_ref[:, 0:D_MODEL] * br_a + gate_ref[:, D_MODEL:2 * D_MODEL] * br_b
    h = x_ref[...] + jnp.dot(merged.astype(BF16), wo_ref[...], preferred_element_type=F32)
    h_ref[...] = h
    hn = h * lax.rsqrt(jnp.mean(h * h, axis=-1, keepdims=True) + NORM_EPS) * fnw_ref[...]
    hn_ref[...] = hn
    logits = jnp.dot(hn.astype(BF16), rw_ref[...], preferred_element_type=F32) + rb_ref[...]
    lane_i = lax.broadcasted_iota(jnp.int32, logits.shape, 1)
    lane = lane_i.astype(F32)
    vals, idxs = [], []
    cur = logits
    for _ in range(TOP_K):
        mx = jnp.max(cur, axis=-1, keepdims=True)
        ix = jnp.min(jnp.where(cur == mx, lane, float(LANES)), axis=-1, keepdims=True)
        vals.append(mx)
        idxs.append(ix)
        cur = jnp.where(lane == ix, -jnp.inf, cur)
    es = [jnp.exp(vv - vals[0]) for vv in vals]
    tot = es[0] + es[1] + es[2] + es[3]
    tm = logits.shape[0]
    ri = lax.broadcasted_iota(jnp.int32, (tm, tm), 0)
    rj = lax.broadcasted_iota(jnp.int32, (tm, tm), 1)
    before = jnp.where(ri > rj, 1.0, 0.0).astype(BF16)
    run = run_ref[0:1, :]
    ranks = []
    for kq in range(TOP_K):
        onehot = jnp.where(lane == idxs[kq], 1.0, 0.0)
        prior = jnp.dot(before, onehot.astype(BF16), preferred_element_type=F32) + run
        ranks.append(jnp.sum(onehot * prior, axis=-1, keepdims=True))
        run = run + jnp.sum(onehot, axis=0, keepdims=True)
    run_ref[...] = jnp.broadcast_to(run, run_ref.shape)
    cnt_ref[...] = jnp.broadcast_to(run, cnt_ref.shape)

    ti = jnp.zeros(logits.shape, F32)
    tg = jnp.zeros(logits.shape, F32)
    for kq in range(TOP_K):
        ti = jnp.where(lane_i == kq, idxs[kq], ti)
        ti = jnp.where(lane_i == TOP_K + kq, ranks[kq], ti)
        tg = jnp.where(lane_i == kq, es[kq] / tot, tg)
    ti_ref[...] = ti.astype(jnp.int32)
    tg_ref[...] = tg


def _merge(x2d, yrw, p2d, o_list, l_list, wba, wbb, wo, fnw, rw_pad, rb_pad, *, tm):
    n = x2d.shape[0]
    row = lambda wdt: pl.BlockSpec((tm, wdt), lambda i: (i, 0))
    full = lambda a: pl.BlockSpec(a.shape, lambda i: (0,) * a.ndim)
    gate_blk = (T_GATE * PROJ_TN) // (2 * D_MODEL)
    return pl.pallas_call(
        _merge_kernel,
        grid=(n // tm,),
        in_specs=[row(D_MODEL), row(D_MODEL),
                  pl.BlockSpec((tm, 2 * D_MODEL), lambda i: (i, gate_blk)),
                  row(ATT_OUT), row(ATT_OUT), row(ATT_OUT), row(ATT_OUT), row(ATT_OUT), row(ATT_OUT),
                  full(wba), full(wbb), full(wo), full(fnw), full(rw_pad), full(rb_pad)],
        out_specs=[row(D_MODEL), row(D_MODEL), row(LANES), row(LANES),
                   pl.BlockSpec((SUBLANES, LANES), lambda i: (0, 0))],
        out_shape=[jax.ShapeDtypeStruct((n, D_MODEL), F32), jax.ShapeDtypeStruct((n, D_MODEL), F32),
                   jax.ShapeDtypeStruct((n, LANES), jnp.int32), jax.ShapeDtypeStruct((n, LANES), F32),
                   jax.ShapeDtypeStruct((SUBLANES, LANES), F32)],
        scratch_shapes=[pltpu.VMEM((SUBLANES, LANES), F32)],
        compiler_params=_cparams(("arbitrary",)),
        name="merge",
    )(x2d, yrw, p2d, *o_list, *l_list, wba, wbb, wo, fnw, rw_pad, rb_pad)


def _row_copy(src_ref, s_row, dst_ref, d_row, sem):
    return pltpu.make_async_copy(src_ref.at[pl.ds(s_row, 1), :], dst_ref.at[pl.ds(d_row, 1), :], sem)


def _dispatch_kernel(pad_start_ref, pad_len_ref, pos_ref, hn_ref, xs_ref, zrow_ref, sem, *, tt):
    i = pl.program_id(0)

    @pl.when(i == 0)
    def _():
        zrow_ref[...] = jnp.zeros(zrow_ref.shape, F32)

        def per_expert(e, carry):
            s = pad_start_ref[e]
            cnt = pad_len_ref[e]

            def issue(rr, cc):
                _row_copy(zrow_ref, 0, xs_ref, s + rr, sem).start()
                return cc

            def drain(rr, cc):
                _row_copy(zrow_ref, 0, xs_ref, s + rr, sem).wait()
                return cc

            lax.fori_loop(0, cnt, issue, 0)
            lax.fori_loop(0, cnt, drain, 0)
            return carry

        lax.fori_loop(0, N_EXPERTS, per_expert, 0)

    def issue(nn, cc):
        for kq in range(TOP_K):
            _row_copy(hn_ref, nn, xs_ref, pos_ref[0, 0, nn * TOP_K + kq], sem).start()
        return cc

    def drain(nn, cc):
        for kq in range(TOP_K):
            _row_copy(hn_ref, nn, xs_ref, pos_ref[0, 0, nn * TOP_K + kq], sem).wait()
        return cc

    lax.fori_loop(0, tt, issue, 0)
    lax.fori_loop(0, tt, drain, 0)


def _dispatch(hn, pos3, pad_start, pad_len, rows, *, tt):
    n = hn.shape[0]
    return pl.pallas_call(
        functools.partial(_dispatch_kernel, tt=tt),
        grid_spec=pltpu.PrefetchScalarGridSpec(
            num_scalar_prefetch=2,
            grid=(n // tt,),
            in_specs=[
                pl.BlockSpec((1, 1, tt * TOP_K), lambda i, ps, plen: (i, 0, 0), memory_space=pltpu.SMEM),
                pl.BlockSpec((tt, D_MODEL), lambda i, ps, plen: (i, 0)),
            ],
            out_specs=pl.BlockSpec(memory_space=pl.ANY),
            scratch_shapes=[pltpu.VMEM((SUBLANES, D_MODEL), F32), pltpu.SemaphoreType.DMA(())],
        ),
        out_shape=jax.ShapeDtypeStruct((rows, D_MODEL), F32),
        compiler_params=_cparams(("arbitrary",)),
        name="moe_dispatch",
    )(pad_start, pad_len, pos3, hn)


def _moe_kernel(be_ref, nu_ref, xs_ref, w1_ref, b1_ref, w2_ref, b2_ref, y_ref):
    i = pl.program_id(0)

    @pl.when(i < nu_ref[0])
    def _():
        x = xs_ref[...].astype(BF16)
        hdn = jnp.dot(x, w1_ref[0], preferred_element_type=F32) + b1_ref[0]
        glu = jnp.minimum(hdn[:, 0:D_MODEL], SWIGLU_LIMIT)
        lin = jnp.clip(hdn[:, D_MODEL:2 * D_MODEL], -SWIGLU_LIMIT, SWIGLU_LIMIT)
        act = glu * jax.nn.sigmoid(SWIGLU_ALPHA * glu) * (lin + 1.0)
        y_ref[...] = jnp.dot(act.astype(BF16), w2_ref[0], preferred_element_type=F32) + b2_ref[0]

    @pl.when(i >= nu_ref[0])
    def _():
        y_ref[...] = jnp.zeros(y_ref.shape, F32)


def _moe(xs, blk_expert, n_used, w1, b1, w2, b2, *, tm):
    rows = xs.shape[0]
    nb = rows // tm
    return pl.pallas_call(
        _moe_kernel,
        grid_spec=pltpu.PrefetchScalarGridSpec(
            num_scalar_prefetch=2,
            grid=(nb,),
            in_specs=[
                pl.BlockSpec((tm, D_MODEL), lambda i, be, nu: (jnp.minimum(i, nu[0] - 1), 0)),
                pl.BlockSpec((1, D_MODEL, 2 * D_MODEL), lambda i, be, nu: (be[i], 0, 0)),
                pl.BlockSpec((1, 1, 2 * D_MODEL), lambda i, be, nu: (be[i], 0, 0)),
                pl.BlockSpec((1, D_MODEL, D_MODEL), lambda i, be, nu: (be[i], 0, 0)),
                pl.BlockSpec((1, 1, D_MODEL), lambda i, be, nu: (be[i], 0, 0)),
            ],
            out_specs=pl.BlockSpec((tm, D_MODEL), lambda i, be, nu: (i, 0)),
        ),
        out_shape=jax.ShapeDtypeStruct((rows, D_MODEL), F32),
        compiler_params=_cparams(("arbitrary",)),
        name="moe_experts",
    )(blk_expert, n_used, xs, w1, b1, w2, b2)


def _combine_kernel(pos_ref, h_ref, tg_ref, yb_ref, o_ref, buf_ref, sem, *, tt):
    def issue(nn, cc):
        for kq in range(TOP_K):
            _row_copy(yb_ref, pos_ref[0, 0, nn * TOP_K + kq], buf_ref.at[kq], nn, sem).start()
        return cc

    def drain(nn, cc):
        for kq in range(TOP_K):
            _row_copy(yb_ref, pos_ref[0, 0, nn * TOP_K + kq], buf_ref.at[kq], nn, sem).wait()
        return cc

    lax.fori_loop(0, tt, issue, 0)
    lax.fori_loop(0, tt, drain, 0)
    acc = h_ref[...]
    for kq in range(TOP_K):
        acc = acc + tg_ref[:, kq:kq + 1] * buf_ref[kq]
    o_ref[...] = acc


def _combine(h, tg, pos3, yb, *, tt):
    n = h.shape[0]
    return pl.pallas_call(
        functools.partial(_combine_kernel, tt=tt),
        grid=(n // tt,),
        in_specs=[
            pl.BlockSpec((1, 1, tt * TOP_K), lambda i: (i, 0, 0), memory_space=pltpu.SMEM),
            pl.BlockSpec((tt, D_MODEL), lambda i: (i, 0)),
            pl.BlockSpec((tt, LANES), lambda i: (i, 0)),
            pl.BlockSpec(memory_space=pl.ANY),
        ],
        out_specs=pl.BlockSpec((tt, D_MODEL), lambda i: (i, 0)),
        out_shape=jax.ShapeDtypeStruct((n, D_MODEL), F32),
        scratch_shapes=[pltpu.VMEM((TOP_K, tt, D_MODEL), F32), pltpu.SemaphoreType.DMA(())],
        compiler_params=_cparams(("arbitrary",)),
        name="moe_combine",
    )(pos3, h, tg, yb)


def _routing(top_idx, rank, counts, tm):
    n = top_idx.shape[0]
    nk = n * TOP_K
    experts = jnp.arange(N_EXPERTS, dtype=jnp.int32)
    blocks_e = (counts + tm - 1) // tm
    upper = (experts[:, None] <= experts[None, :]).astype(jnp.int32)
    block_end = jnp.sum(blocks_e[:, None] * upper, axis=0)
    row_start = (block_end - blocks_e) * tm
    onehot = (top_idx[:, :, None] == experts[None, None, :]).astype(jnp.int32)
    pos = (jnp.sum(onehot * row_start[None, None, :], axis=-1) + rank).reshape(nk).astype(jnp.int32)
    n_blocks = -(-nk // tm) + N_EXPERTS
    blk_ids = jnp.arange(n_blocks, dtype=jnp.int32)
    blk_expert = jnp.minimum(
        jnp.sum((block_end[None, :] <= blk_ids[:, None]).astype(jnp.int32), axis=1), N_EXPERTS - 1
    ).astype(jnp.int32)
    n_used = block_end[-1:].astype(jnp.int32)
    pad_start = (row_start + counts).astype(jnp.int32)
    pad_len = (blocks_e * tm - counts).astype(jnp.int32)
    return pos, blk_expert, n_used, pad_start, pad_len, n_blocks * tm


def _pad_cols(a, width):
    return jnp.pad(a, ((0, 0), (0, width - a.shape[1])))


def _pad_rows(a, height):
    return jnp.pad(a, ((0, height - a.shape[0]), (0, 0)))


def _proj_columns(a):
    d = D_MODEL
    xw = a[:, 3 * d:3 * d + DECAY_LORA]
    xa = a[:, 3 * d + DECAY_LORA:3 * d + DECAY_LORA + AAA_LORA]
    xg = a[:, 3 * d + DECAY_LORA + AAA_LORA:C_RW]
    att = a[:, C_RW:C_RW + 3 * ATT_DIM]
    q, k, v = att[:, :ATT_DIM], att[:, ATT_DIM:2 * ATT_DIM], att[:, 2 * ATT_DIM:]
    parts = [a[:, :3 * d], _pad_cols(xw, LANES), _pad_cols(xa, LANES), _pad_cols(xg, 2 * LANES), q]
    for g in range(N_GROUPS):
        parts.append(k[:, g * ATT_OUT:(g + 1) * ATT_OUT])
        parts.append(v[:, g * ATT_OUT:(g + 1) * ATT_OUT])
    parts.append(a[:, C_RW + 3 * ATT_DIM:])
    return jnp.concatenate(parts, axis=1)


def _rope_tables(pos):
    inv = ROPE_THETA ** (-jnp.arange(0, ATT_HEAD, 2, dtype=F32) / ATT_HEAD)
    ang = pos.astype(F32)[:, None] * inv[None, :]
    cos, sin = jnp.cos(ang), jnp.sin(ang)
    return jnp.concatenate([cos, cos], axis=1), jnp.concatenate([-sin, sin], axis=1)


def _prep_weights(attn_norm_w, w_in, rw_mu, rw_w0, rw_w2, rw_a0, rw_a2, rw_g2, rw_kk, rw_ka, rw_rk,
                  rw_lnx_w, rw_lnx_b, q_norm_w, k_norm_w, w_br, w_out, ffn_norm_w, router_w, router_b,
                  moe_w1, moe_b1, moe_w2, moe_b2):
    w = {}
    w['nw'] = attn_norm_w.reshape(1, D_MODEL)
    w['w_pad'] = _proj_columns(w_in).astype(BF16)
    w['mu_pad'] = _proj_columns(_pad_cols(rw_mu.reshape(1, C_RW), C_RW + 3 * ATT_DIM + 2 * D_MODEL))[:, :RW_PAD]
    w['par'] = jnp.concatenate([rw_w0.reshape(1, -1), rw_a0.reshape(1, -1), rw_kk.reshape(1, -1),
                                rw_ka.reshape(1, -1), rw_rk.reshape(1, -1), rw_lnx_w.reshape(1, -1),
                                rw_lnx_b.reshape(1, -1), jnp.zeros((1, D_MODEL), F32)], axis=0)
    w['w2p'] = _pad_rows(rw_w2, LANES).astype(BF16)
    w['a2p'] = _pad_rows(rw_a2, LANES).astype(BF16)
    w['g2p'] = _pad_rows(rw_g2, 2 * LANES).astype(BF16)
    w['qn'] = q_norm_w.reshape(1, ATT_HEAD)
    w['kn'] = k_norm_w.reshape(1, ATT_HEAD)
    w['wba'] = w_br[:D_MODEL].astype(BF16)
    w['wbb'] = w_br[D_MODEL:].astype(BF16)
    w['wo'] = w_out.astype(BF16)
    w['fnw'] = ffn_norm_w.reshape(1, D_MODEL)
    w['rw_pad'] = _pad_cols(router_w, LANES).astype(BF16)
    w['rb_pad'] = jnp.concatenate([router_b.reshape(1, N_EXPERTS),
                                   jnp.full((1, LANES - N_EXPERTS), -1e30, F32)], axis=1)
    cols = jnp.arange(2 * D_MODEL, dtype=jnp.int32)
    src = jnp.where(cols < D_MODEL, 2 * cols, 2 * (cols - D_MODEL) + 1)
    perm = (jnp.arange(2 * D_MODEL, dtype=jnp.int32)[:, None] == src[None, :]).astype(BF16)
    w['w1'] = jnp.einsum('edh,hk->edk', moe_w1.astype(BF16), perm, preferred_element_type=BF16)
    w['b1'] = jnp.concatenate([moe_b1[:, 0::2], moe_b1[:, 1::2]], axis=1).reshape(N_EXPERTS, 1, 2 * D_MODEL)
    w['w2'] = moe_w2.astype(BF16)
    w['b2'] = moe_b2.reshape(N_EXPERTS, 1, D_MODEL)
    return w


def _state_to_pairs(s):
    b = s.shape[0]
    sr = s.reshape(b, RWKV_HEADS // 2, 2, RWKV_HEAD, RWKV_HEAD)
    z = jnp.zeros_like(sr[:, :, 0])
    top = jnp.concatenate([sr[:, :, 0], z], axis=-1)
    bot = jnp.concatenate([z, sr[:, :, 1]], axis=-1)
    return jnp.concatenate([top, bot], axis=-2)


def _pairs_to_state(zp):
    b = zp.shape[0]
    zr = zp.reshape(b, RWKV_HEADS // 2, 2, RWKV_HEAD, 2, RWKV_HEAD)
    return jnp.stack([zr[:, :, 0, :, 0, :], zr[:, :, 1, :, 1, :]], axis=2).reshape(b, RWKV_HEADS, RWKV_HEAD, RWKV_HEAD)


def _layer(x, pos, shift_prev, wkv0, caches, w, *, proj_tm, merge_tm, moe_tm, tok_tt):
    b, t, _ = x.shape
    n = b * t
    x2d = x.reshape(n, D_MODEL)
    is_prompt = caches is None

    cs_tab, sn_tab = _rope_tables(pos)
    if not is_prompt:
        cs_tab = jnp.tile(cs_tab, (proj_tm // t, 1))
        sn_tab = jnp.tile(sn_tab, (proj_tm // t, 1))
    p2d = _proj(x2d, w['nw'], w['w_pad'], w['qn'], w['kn'], cs_tab, sn_tab,
                tm=proj_tm, n_tiles=N_TILES, do_norm=True)
    p3 = p2d.reshape(b, t, P_PAD)
    shift_out = _rmsnorm_rows(x[:, t - 1, :], w['nw'])

    if is_prompt:
        prev0 = jnp.zeros((b, 1, RW_PAD), F32)
        s0z = jnp.zeros((b, RWKV_HEADS // 2, LANES, LANES), F32)
        yrw, sfin = _rwkv(p3, prev0, s0z, w['mu_pad'], w['par'], w['w2p'], w['a2p'], w['g2p'],
                          chunk=SCAN_CHUNK, n_valid=None)
        yrw = yrw.reshape(n, D_MODEL)
    else:
        one = jnp.ones((1, ATT_HEAD), F32)
        zero_tab = jnp.zeros((b, ATT_HEAD), F32)
        prev_rw = _proj(shift_prev, w['nw'], w['w_pad'], one, one, zero_tab, zero_tab,
                        tm=b, n_tiles=T_LORA + 1, do_norm=False)
        t_pad = SCAN_CHUNK
        p_rw = jnp.pad(p3[:, :, :RW_PAD], ((0, 0), (0, t_pad - t), (0, 0)))
        yrw, sfin = _rwkv(p_rw, prev_rw.reshape(b, 1, RW_PAD), _state_to_pairs(wkv0), w['mu_pad'], w['par'],
                          w['w2p'], w['a2p'], w['g2p'], chunk=SCAN_CHUNK, n_valid=t)
        yrw = yrw[:, :t].reshape(n, D_MODEL)
    s_fin = _pairs_to_state(sfin)

    new_kv = []
    if is_prompt:
        o_list, l_list = [], []
        for g, (window, dil) in enumerate(DILATION_GROUPS):
            o_g, l_g = _attn_prompt(p3, g, dil, window // dil)
            o_list.append(o_g)
            l_list.append(l_g)
            keep = min(window, t)
            new_kv.append(_kv_rows(p3, g, keep, min(keep, 256))[None])
    else:
        kv_col = T_KV * PROJ_TN
        q5 = p3[:, :, T_Q * PROJ_TN:T_KV * PROJ_TN].reshape(b, t, N_GROUPS, ATT_HPG, ATT_HEAD)
        kv_new = [p3[:, :, kv_col + g * 2 * ATT_OUT:kv_col + (g + 1) * 2 * ATT_OUT].reshape(b, t, 2, ATT_HPG, ATT_HEAD)
                  for g in range(N_GROUPS)]
        o_list, l_list = _attn_sample(q5, kv_new, caches)
        new_kv = _cache_roll(caches, [kvn[None] for kvn in kv_new])

    h, hn, ti, tg, cnt = _merge(x2d, yrw, p2d, o_list, l_list, w['wba'], w['wbb'], w['wo'], w['fnw'],
                                w['rw_pad'], w['rb_pad'], tm=merge_tm)

    counts = cnt[0, :N_EXPERTS].astype(jnp.int32)
    pos_rows, blk_expert, n_used, pad_start, pad_len, rows = _routing(
        ti[:, :TOP_K], ti[:, TOP_K:2 * TOP_K], counts, moe_tm)
    pos3 = pos_rows.reshape(n // tok_tt, 1, tok_tt * TOP_K)
    xs = _dispatch(hn, pos3, pad_start, pad_len, rows, tt=tok_tt)
    yb = _moe(xs, blk_expert, n_used, w['w1'], w['b1'], w['w2'], w['b2'], tm=moe_tm)
    y = _combine(h, tg, pos3, yb, tt=tok_tt)
    return y.reshape(b, t, D_MODEL), shift_out, s_fin, new_kv


def kernel(x_prompt, x_sample, state_wkv, state_shift, cache_kv_w128, cache_kv_w512, cache_kv_w2048,
           attn_norm_w, w_in, rw_mu, rw_w0, rw_w2, rw_a0, rw_a2, rw_g2, rw_kk, rw_ka, rw_rk,
           rw_lnx_w, rw_lnx_b, q_norm_w, k_norm_w, w_br, w_out, ffn_norm_w, router_w, router_b,
           moe_w1, moe_b1, moe_w2, moe_b2):
    depth = w_in.shape[0]
    assert depth == 1
    bp, tp, _ = x_prompt.shape
    bs, ts, _ = x_sample.shape
    l = 0
    w = _prep_weights(attn_norm_w[l], w_in[l], rw_mu[l], rw_w0[l], rw_w2[l], rw_a0[l], rw_a2[l], rw_g2[l],
                      rw_kk[l], rw_ka[l], rw_rk[l], rw_lnx_w[l], rw_lnx_b[l], q_norm_w[l], k_norm_w[l],
                      w_br[l], w_out[l], ffn_norm_w[l], router_w[l], router_b[l],
                      moe_w1[l], moe_b1[l], moe_w2[l], moe_b2[l])
    pos_p = jnp.arange(tp, dtype=jnp.int32)
    pos_s = PAST_LEN + jnp.arange(ts, dtype=jnp.int32)
    yp, sh_p, wkv_p, kv_p = _layer(x_prompt, pos_p, None, None, None, w,
                                   proj_tm=min(1024, tp), merge_tm=256, moe_tm=MOE_TM, tok_tt=256)
    ns = bs * ts
    ys, sh_s, wkv_s, kv_s = _layer(x_sample, pos_s, state_shift[l], state_wkv[l],
                                   (cache_kv_w128, cache_kv_w512, cache_kv_w2048), w,
                                   proj_tm=ns, merge_tm=min(256, ns), moe_tm=128, tok_tt=min(256, ns))
    return (yp, ys,
            wkv_p[None], sh_p[None], kv_p[0], kv_p[1], kv_p[2],
            wkv_s[None], sh_s[None], kv_s[0], kv_s[1], kv_s[2])
```

```python
import functools

import jax
import jax.numpy as jnp
from jax import lax
from jax.experimental import pallas as pl
from jax.experimental.pallas import tpu as pltpu

F32 = jnp.float32
BF16 = jnp.bfloat16

D_MODEL = 1024
NORM_EPS = 1e-5
RWKV_HEAD = 64
RWKV_HEADS = D_MODEL // RWKV_HEAD
DECAY_LORA = 64
AAA_LORA = 64
GATE_LORA = 160
GN_EPS = 64e-5
C_RW = 3 * D_MODEL + DECAY_LORA + AAA_LORA + GATE_LORA
ATT_HEAD = 128
ATT_HPG = 4
DILATION_GROUPS = ((128, 1), (512, 4), (2048, 16))
N_GROUPS = 3
ATT_DIM = N_GROUPS * ATT_HPG * ATT_HEAD
ATT_OUT = ATT_HPG * ATT_HEAD
ROPE_THETA = 10000.0
N_EXPERTS = 32
TOP_K = 4
SWIGLU_ALPHA = 1.702
SWIGLU_LIMIT = 7.0
PAST_LEN = 8192

LANES = 128
SUBLANES = 8
VMEM_LIMIT = 56 * 1024 * 1024

PROJ_TN = 512
T_RKV = 0
T_LORA = 6
T_Q = 7
T_KV = 10
T_GATE = 16
N_TILES = 20
P_PAD = N_TILES * PROJ_TN
RW_PAD = (T_LORA + 1) * PROJ_TN

SCAN_CHUNK = 64
MOE_TM = 512


def _cparams(sem, vmem=VMEM_LIMIT):
    return pltpu.CompilerParams(dimension_semantics=sem, vmem_limit_bytes=vmem)


def _proj_kernel(x_ref, nw_ref, w_ref, qn_ref, kn_ref, cs_ref, sn_ref, o_ref, xn_ref, *, do_norm):
    j = pl.program_id(1)

    @pl.when(j == 0)
    def _():
        x = x_ref[...]
        if do_norm:
            x = x * lax.rsqrt(jnp.mean(x * x, axis=-1, keepdims=True) + NORM_EPS) * nw_ref[...]
        xn_ref[...] = x.astype(BF16)

    acc = jnp.dot(xn_ref[...], w_ref[...], preferred_element_type=F32)
    is_q = (j >= T_Q) & (j < T_KV)
    is_k = (j >= T_KV) & (j < T_GATE) & ((j - T_KV) % 2 == 0)
    is_qk = is_q | is_k
    is_gate = j >= T_GATE

    @pl.when(is_qk)
    def _():
        nwh = jnp.where(j < T_KV, qn_ref[...], kn_ref[...])
        cs = cs_ref[...]
        sn = sn_ref[...]
        for h in range(ATT_HPG):
            xh = acc[:, h * ATT_HEAD:(h + 1) * ATT_HEAD]
            y = xh * lax.rsqrt(jnp.mean(xh * xh, axis=-1, keepdims=True) + NORM_EPS) * nwh
            o_ref[:, h * ATT_HEAD:(h + 1) * ATT_HEAD] = y * cs + pltpu.roll(y, ATT_HEAD // 2, axis=1) * sn

    @pl.when(is_gate)
    def _():
        o_ref[...] = jax.nn.sigmoid(acc)

    @pl.when(jnp.logical_not(is_qk | is_gate))
    def _():
        o_ref[...] = acc


def _proj(x2d, nw, w_pad, qn, kn, cs_tab, sn_tab, *, tm, n_tiles, do_norm):
    n = x2d.shape[0]
    tab_blocks = cs_tab.shape[0] // tm
    return pl.pallas_call(
        functools.partial(_proj_kernel, do_norm=do_norm),
        grid=(n // tm, n_tiles),
        in_specs=[
            pl.BlockSpec((tm, D_MODEL), lambda i, j: (i, 0)),
            pl.BlockSpec((1, D_MODEL), lambda i, j: (0, 0)),
            pl.BlockSpec((D_MODEL, PROJ_TN), lambda i, j: (0, j)),
            pl.BlockSpec((1, ATT_HEAD), lambda i, j: (0, 0)),
            pl.BlockSpec((1, ATT_HEAD), lambda i, j: (0, 0)),
            pl.BlockSpec((tm, ATT_HEAD), lambda i, j: (i % tab_blocks, 0)),
            pl.BlockSpec((tm, ATT_HEAD), lambda i, j: (i % tab_blocks, 0)),
        ],
        out_specs=pl.BlockSpec((tm, PROJ_TN), lambda i, j: (i, j)),
        out_shape=jax.ShapeDtypeStruct((n, n_tiles * PROJ_TN), F32),
        scratch_shapes=[pltpu.VMEM((tm, D_MODEL), BF16)],
        compiler_params=_cparams(("parallel", "arbitrary")),
        name="proj",
    )(x2d, nw, w_pad, qn, kn, cs_tab, sn_tab)


def _rmsnorm_kernel(x_ref, w_ref, o_ref):
    x = x_ref[...]
    o_ref[...] = x * lax.rsqrt(jnp.mean(x * x, axis=-1, keepdims=True) + NORM_EPS) * w_ref[...]


def _rmsnorm_rows(x2d, w):
    return pl.pallas_call(
        _rmsnorm_kernel,
        out_shape=jax.ShapeDtypeStruct(x2d.shape, F32),
        name="rmsnorm_rows",
    )(x2d, w)


def _split3(x):
    h1 = x.astype(BF16)
    r1 = x - h1.astype(F32)
    h2 = r1.astype(BF16)
    h3 = (r1 - h2.astype(F32)).astype(BF16)
    return h1, h2, h3


def _seg_sum(x, seg_ones):
    rows = x.shape[0]
    hi = x.astype(BF16)
    lo = (x - hi.astype(F32)).astype(BF16)
    hl = jnp.concatenate([hi, lo], axis=0)
    outs = []
    for p in range(D_MODEL // LANES):
        s = jnp.dot(hl[:, p * LANES:(p + 1) * LANES], seg_ones, preferred_element_type=F32)
        outs.append(s[:rows] + s[rows:])
    return jnp.concatenate(outs, axis=1)


def _nt_dot(a, b):
    return lax.dot_general(a, b, (((1,), (1,)), ((), ())), preferred_element_type=F32)


def _rwkv_kernel(p_ref, prev0_ref, s0_ref, mu_ref, par_ref, w2_ref, a2_ref, g2_ref,
                 y_ref, st_ref, prev_ref, *, chunk, n_valid):
    c = pl.program_id(1)
    C = chunk

    @pl.when(c == 0)
    def _():
        st_ref[...] = s0_ref[...]
        prev_ref[...] = prev0_ref[0]

    p = p_ref[0]
    row = lax.broadcasted_iota(jnp.int32, (C, 1), 0)
    shifted = jnp.where(row == 0, prev_ref[...], pltpu.roll(p, 1, axis=0))
    prev_ref[...] = p[C - 1:C, :]
    ps = p + (shifted - p) * mu_ref[...]

    w0 = par_ref[0:1, :]
    a0 = par_ref[1:2, :]
    k_k = par_ref[2:3, :]
    k_a = par_ref[3:4, :]
    r_k = par_ref[4:5, :]
    lnx_w = par_ref[5:6, :]
    lnx_b = par_ref[6:7, :]

    r = ps[:, 0:D_MODEL]
    k = ps[:, D_MODEL:2 * D_MODEL]
    v = ps[:, 2 * D_MODEL:3 * D_MODEL]
    xw = ps[:, 3 * D_MODEL:3 * D_MODEL + LANES]
    xa = ps[:, 3 * D_MODEL + LANES:3 * D_MODEL + 2 * LANES]
    xg = ps[:, 3 * D_MODEL + 2 * LANES:RW_PAD]

    zw = w0 + jnp.dot(jnp.tanh(xw).astype(BF16), w2_ref[...], preferred_element_type=F32)
    nz = -zw
    softplus = jnp.log(1.0 + jnp.exp(-jnp.abs(nz))) + jnp.maximum(nz, 0.0)
    lw = -jnp.exp(-softplus - 0.5)
    a = jax.nn.sigmoid(a0 + jnp.dot(xa.astype(BF16), a2_ref[...], preferred_element_type=F32))
    g = jnp.dot(jax.nn.sigmoid(xg).astype(BF16), g2_ref[...], preferred_element_type=F32)

    li = lax.broadcasted_iota(jnp.int32, (LANES, LANES), 0)
    lj = lax.broadcasted_iota(jnp.int32, (LANES, LANES), 1)
    same_head = (li < RWKV_HEAD) == (lj < RWKV_HEAD)
    seg_ones = jnp.where(same_head, 1.0, 0.0).astype(BF16)
    blk_mask = jnp.where(same_head, 1.0, 0.0)

    kkr = k * k_k
    kk = kkr / jnp.maximum(jnp.sqrt(_seg_sum(kkr * kkr, seg_ones)), 1e-12)
    k2 = k * (1.0 + (a - 1.0) * k_a)
    bonus = _seg_sum(r * k2 * r_k, seg_ones) * v

    if n_valid is not None:
        valid = (c * C + row) < n_valid
        lw = jnp.where(valid, lw, 0.0)
        kk = jnp.where(valid, kk, 0.0)
        k2 = jnp.where(valid, k2, 0.0)

    ti = lax.broadcasted_iota(jnp.int32, (C, C), 0)
    tj = lax.broadcasted_iota(jnp.int32, (C, C), 1)
    tril_incl = jnp.where(ti >= tj, 1.0, 0.0)
    tril_strict = jnp.where(ti > tj, 1.0, 0.0)
    tril_b = tril_incl.astype(BF16)
    l1, l2, l3 = _split3(lw)
    cl = (jnp.dot(tril_b, l1, preferred_element_type=F32)
          + jnp.dot(tril_b, l2, preferred_element_type=F32)
          + jnp.dot(tril_b, l3, preferred_element_type=F32))
    cl_end = cl[C - 1:C, :]
    kb = kk * a
    e_inv = jnp.exp(-cl)
    e_tail = jnp.exp(cl_end - cl)
    a_t = (-kk) * jnp.exp(cl - lw)
    r_t = r * jnp.exp(cl)
    b_t = kb * e_inv
    k_t = k2 * e_inv
    b_h = kb * e_tail
    k_h = k2 * e_tail
    w_end = jnp.exp(cl_end)

    lane = lax.broadcasted_iota(jnp.int32, (1, LANES), 1)
    first = lane < RWKV_HEAD
    n_lev = max(1, (C - 1).bit_length())

    pairs = range(D_MODEL // LANES)
    sls = [slice(pr * LANES, (pr + 1) * LANES) for pr in pairs]
    dot = functools.partial(jnp.dot, preferred_element_type=F32)
    head_sel = lambda x0, x1: jnp.where(first, x0, x1)

    zs = [st_ref[0, pr] for pr in pairs]
    ars = [jnp.concatenate([a_t[:, sl], r_t[:, sl]], axis=0) for sl in sls]
    vbs = [v[:, sl].astype(BF16) for sl in sls]
    q0s = [_nt_dot(ars[pr].astype(BF16), zs[pr].astype(BF16)) for pr in pairs]
    lab, lak, mrb, mrk = {}, {}, {}, {}
    for pr in pairs:
        btb = b_t[:, sls[pr]].astype(BF16)
        ktb = k_t[:, sls[pr]].astype(BF16)
        for h in range(2):
            mh = first if h == 0 else jnp.logical_not(first)
            arh = jnp.where(mh, ars[pr], 0.0).astype(BF16)
            gb = _nt_dot(arh, btb)
            gk = _nt_dot(arh, ktb)
            lab[pr, h] = (gb[:C] * tril_strict).astype(BF16)
            lak[pr, h] = (gk[:C] * tril_strict).astype(BF16)
            mrb[pr, h] = (gb[C:] * tril_incl).astype(BF16)
            mrk[pr, h] = (gk[C:] * tril_incl).astype(BF16)
    us = [q0s[pr][:C] + head_sel(dot(lak[pr, 0], vbs[pr]), dot(lak[pr, 1], vbs[pr])) for pr in pairs]
    xs = lab
    for lev in range(n_lev):
        ubs = [us[pr].astype(BF16) for pr in pairs]
        us = [us[pr] + head_sel(dot(xs[pr, 0], ubs[pr]), dot(xs[pr, 1], ubs[pr])) for pr in pairs]
        if lev < n_lev - 1:
            xs = {key: dot(xh, xh).astype(BF16) for key, xh in xs.items()}
    ubs = [us[pr].astype(BF16) for pr in pairs]
    y_tiles = [q0s[pr][C:] + head_sel(dot(mrb[pr, 0], ubs[pr]) + dot(mrk[pr, 0], vbs[pr]),
                                      dot(mrb[pr, 1], ubs[pr]) + dot(mrk[pr, 1], vbs[pr])) for pr in pairs]
    for pr in pairs:
        sl = sls[pr]
        uv_t = jnp.concatenate([us[pr], v[:, sl]], axis=0).T.astype(BF16)
        bk_h = jnp.concatenate([b_h[:, sl], k_h[:, sl]], axis=0).astype(BF16)
        st_ref[0, pr] = zs[pr] * w_end[:, sl] + blk_mask * dot(uv_t, bk_h)

    y = jnp.concatenate(y_tiles, axis=1)
    inv_n = 1.0 / RWKV_HEAD
    mean = _seg_sum(y, seg_ones) * inv_n
    yc = y - mean
    var = _seg_sum(yc * yc, seg_ones) * inv_n
    yn = yc * lax.rsqrt(var + GN_EPS) * lnx_w + lnx_b
    y_ref[0] = ((yn + bonus) * g).astype(y_ref.dtype)


def _rwkv(p3, prev0, s0z, mu_pad, par, w2p, a2p, g2p, *, chunk, n_valid):
    b, t, _ = p3.shape
    n_pairs = D_MODEL // LANES
    full = lambda shape: pl.BlockSpec(shape, lambda i, c: (0,) * len(shape))
    return pl.pallas_call(
        functools.partial(_rwkv_kernel, chunk=chunk, n_valid=n_valid),
        grid=(b, t // chunk),
        in_specs=[
            pl.BlockSpec((1, chunk, RW_PAD), lambda i, c: (i, c, 0)),
            pl.BlockSpec((1, 1, RW_PAD), lambda i, c: (i, 0, 0)),
            pl.BlockSpec((1, n_pairs, LANES, LANES), lambda i, c: (i, 0, 0, 0)),
            full((1, RW_PAD)),
            full((SUBLANES, D_MODEL)),
            full((LANES, D_MODEL)),
            full((LANES, D_MODEL)),
            full((2 * LANES, D_MODEL)),
        ],
        out_specs=[
            pl.BlockSpec((1, chunk, D_MODEL), lambda i, c: (i, c, 0)),
            pl.BlockSpec((1, n_pairs, LANES, LANES), lambda i, c: (i, 0, 0, 0)),
        ],
        out_shape=[
            jax.ShapeDtypeStruct((b, t, D_MODEL), BF16),
            jax.ShapeDtypeStruct((b, n_pairs, LANES, LANES), F32),
        ],
        scratch_shapes=[pltpu.VMEM((1, RW_PAD), F32)],
        compiler_params=_cparams(("parallel", "arbitrary")),
        name="rwkv",
    )(p3, prev0, s0z, mu_pad, par, w2p, a2p, g2p)


def _attn_kernel(*refs, blk, dil, subs, has_prev):
    if has_prev:
        q_ref, ko_ref, vo_ref, kp_ref, vp_ref, o_ref, l_ref = refs
    else:
        q_ref, ko_ref, vo_ref, o_ref, l_ref = refs
    n = pl.program_id(1)
    unit = dil * blk
    scale = ATT_HEAD ** -0.5

    def class_rows(j, r):
        return pl.ds(j * unit + r, blk, stride=dil) if dil > 1 else pl.ds(j * unit, blk)

    for j in range(subs):
        two = has_prev or j > 0
        nk = 2 * blk if two else blk
        qi = lax.broadcasted_iota(jnp.int32, (blk, nk), 0)
        ki = lax.broadcasted_iota(jnp.int32, (blk, nk), 1)
        if two:
            valid = (ki >= qi) & (ki <= qi + blk)
            if j == 0:
                valid = valid & (ki >= jnp.where(n == 0, blk, 0))
        else:
            valid = ki <= qi
        for r in range(dil):
            rows = class_rows(j, r)
            q = q_ref[0, rows, :].astype(BF16)
            kc = ko_ref[0, rows, :]
            vc = vo_ref[0, rows, :]
            if j > 0:
                kc = jnp.concatenate([ko_ref[0, class_rows(j - 1, r), :], kc], axis=0)
                vc = jnp.concatenate([vo_ref[0, class_rows(j - 1, r), :], vc], axis=0)
            elif has_prev:
                kc = jnp.concatenate([kp_ref[0, class_rows(0, r), :], kc], axis=0)
                vc = jnp.concatenate([vp_ref[0, class_rows(0, r), :], vc], axis=0)
            s = _nt_dot(q, kc.astype(BF16)) * scale
            s = jnp.where(valid, s, -jnp.inf)
            m = jnp.max(s, axis=-1, keepdims=True)
            pexp = jnp.exp(s - m)
            den = jnp.sum(pexp, axis=-1, keepdims=True)
            o = jnp.dot(pexp.astype(BF16), vc.astype(BF16), preferred_element_type=F32) / den
            o_ref[0, rows, :] = o
            l_ref[0, rows, :] = jnp.broadcast_to(m + jnp.log(den), (blk, ATT_HEAD))


ATTN_STEP_ROWS = 1024


def _attn_prompt(p3, g, dil, blk):
    b, t, _ = p3.shape
    unit = dil * blk
    assert t % unit == 0
    subs = max(1, min(ATTN_STEP_ROWS, t) // unit)
    span = unit * subs
    assert t % span == 0
    nb = t // span
    has_prev = nb > 1
    q_col = (T_Q * PROJ_TN + g * ATT_OUT) // ATT_HEAD
    k_col = (T_KV * PROJ_TN + g * 2 * ATT_OUT) // ATT_HEAD
    v_col = k_col + ATT_HPG
    cur = lambda col: pl.BlockSpec((1, span, ATT_HEAD), lambda bi, n, h: (bi, n, col + h))
    prev = lambda col: pl.BlockSpec((1, unit, ATT_HEAD),
                                    lambda bi, n, h: (bi, jnp.maximum(n * subs - 1, 0), col + h))
    in_specs = [cur(q_col), cur(k_col), cur(v_col)]
    if has_prev:
        in_specs += [prev(k_col), prev(v_col)]
    out_sd = jax.ShapeDtypeStruct((b, t, ATT_OUT), F32)
    out_spec = pl.BlockSpec((1, span, ATT_HEAD), lambda bi, n, h: (bi, n, h))
    o, l = pl.pallas_call(
        functools.partial(_attn_kernel, blk=blk, dil=dil, subs=subs, has_prev=has_prev),
        grid=(b, nb, ATT_HPG),
        in_specs=in_specs,
        out_specs=[out_spec, out_spec],
        out_shape=[out_sd, out_sd],
        compiler_params=_cparams(("parallel", "parallel", "arbitrary")),
        name=f"attn_prompt_g{g}",
    )(*([p3] * len(in_specs)))
    return o.reshape(b * t, ATT_OUT), l.reshape(b * t, ATT_OUT)


def _kv_rows_kernel(x_ref, o_ref):
    for s in range(2):
        for h in range(ATT_HPG):
            c0 = (s * ATT_HPG + h) * ATT_HEAD
            o_ref[0, :, s, h, :] = x_ref[0, :, c0:c0 + ATT_HEAD]


def _kv_rows(p3, g, keep, tr):
    b, t, _ = p3.shape
    col = (T_KV * PROJ_TN) // (2 * ATT_OUT) + g
    r0 = (t - keep) // tr
    return pl.pallas_call(
        _kv_rows_kernel,
        grid=(b, keep // tr),
        in_specs=[pl.BlockSpec((1, tr, 2 * ATT_OUT), lambda bi, i: (bi, r0 + i, col))],
        out_specs=pl.BlockSpec((1, tr, 2, ATT_HPG, ATT_HEAD), lambda bi, i: (bi, i, 0, 0, 0)),
        out_shape=jax.ShapeDtypeStruct((b, keep, 2, ATT_HPG, ATT_HEAD), F32),
        compiler_params=_cparams(("parallel", "parallel")),
        name=f"kv_rows_g{g}",
    )(p3)


def _attn_sample_kernel(q_ref, n0_ref, n1_ref, n2_ref, c0_ref, c1_ref, c2_ref, o_ref, l_ref, *, dils, ts):
    n_refs = (n0_ref, n1_ref, n2_ref)
    c_refs = (c0_ref, c1_ref, c2_ref)
    scale = ATT_HEAD ** -0.5
    for g in range(N_GROUPS):
        dil = dils[g]
        cache = c_refs[g][0, 0]
        kc, vc = cache[:, 0], cache[:, 1]
        new = n_refs[g][0]
        kn, vn = new[:, 0], new[:, 1]
        w = kc.shape[0]
        wi = lax.broadcasted_iota(jnp.int32, (w, ATT_HPG, 1), 0)
        ni = lax.broadcasted_iota(jnp.int32, (ts, ATT_HPG, 1), 0)
        for t in range(ts):
            q = q_ref[0, t, g]
            sc = jnp.sum(kc * q[None], axis=-1, keepdims=True) * scale
            sn = jnp.sum(kn * q[None], axis=-1, keepdims=True) * scale
            c_ok = ((wi & (dil - 1)) == (t % dil)) & (wi >= t)
            n_ok = functools.reduce(jnp.logical_or, [ni == tn for tn in range(t + 1) if (t - tn) % dil == 0])
            sc = jnp.where(c_ok, sc, -jnp.inf)
            sn = jnp.where(n_ok, sn, -jnp.inf)
            m = jnp.maximum(jnp.max(sc, axis=0, keepdims=True), jnp.max(sn, axis=0, keepdims=True))
            pc = jnp.exp(sc - m)
            pn = jnp.exp(sn - m)
            den = jnp.sum(pc, axis=0, keepdims=True) + jnp.sum(pn, axis=0, keepdims=True)
            o = (jnp.sum(pc * vc, axis=0, keepdims=True) + jnp.sum(pn * vn, axis=0, keepdims=True)) / den
            o_ref[0, t, g] = o[0]
            l_ref[0, t, g] = jnp.broadcast_to((m + jnp.log(den))[0], (ATT_HPG, ATT_HEAD))


def _attn_sample(q5, new_kv, caches):
    b, ts = q5.shape[:2]
    dils = tuple(d for _, d in DILATION_GROUPS)
    in_specs = [pl.BlockSpec((1, ts, N_GROUPS, ATT_HPG, ATT_HEAD), lambda bi: (bi, 0, 0, 0, 0))]
    for g in range(N_GROUPS):
        in_specs.append(pl.BlockSpec((1, ts, 2, ATT_HPG, ATT_HEAD), lambda bi: (bi, 0, 0, 0, 0)))
    for g in range(N_GROUPS):
        w = caches[g].shape[2]
        in_specs.append(pl.BlockSpec((1, 1, w, 2, ATT_HPG, ATT_HEAD), lambda bi: (0, bi, 0, 0, 0, 0)))
    out_sd = jax.ShapeDtypeStruct((b, ts, N_GROUPS, ATT_HPG, ATT_HEAD), F32)
    out_spec = pl.BlockSpec((1, ts, N_GROUPS, ATT_HPG, ATT_HEAD), lambda bi: (bi, 0, 0, 0, 0))
    o, l = pl.pallas_call(
        functools.partial(_attn_sample_kernel, dils=dils, ts=ts),
        grid=(b,),
        in_specs=in_specs,
        out_specs=[out_spec, out_spec],
        out_shape=[out_sd, out_sd],
        compiler_params=_cparams(("parallel",)),
        name="attn_sample",
    )(q5, *new_kv, *caches)
    o_list = [o[:, :, g].reshape(b * ts, ATT_OUT) for g in range(N_GROUPS)]
    l_list = [l[:, :, g].reshape(b * ts, ATT_OUT) for g in range(N_GROUPS)]
    return o_list, l_list


ROLL_ROWS = 64
ROLL_BLOCK_BYTES = 8 << 20


def _roll_kernel(c_ref, n_ref, o_ref):
    w, ts = c_ref.shape[2], n_ref.shape[2]
    body_rows = w - ROLL_ROWS

    def move(i, carry):
        r = pl.multiple_of(i * ROLL_ROWS, ROLL_ROWS)
        o_ref[0, :, pl.ds(r, ROLL_ROWS)] = c_ref[0, :, pl.ds(r + ts, ROLL_ROWS)]
        return carry

    lax.fori_loop(0, body_rows // ROLL_ROWS, move, 0)
    o_ref[0, :, body_rows:w - ts] = c_ref[0, :, body_rows + ts:w]
    o_ref[0, :, w - ts:w] = n_ref[0]


def _cache_roll(cache, new_rows):
    _, b, w = cache.shape[:3]
    ts = new_rows.shape[2]
    assert w % ROLL_ROWS == 0 and ts < ROLL_ROWS
    row_bytes = 2 * ATT_OUT * 4
    bb = max(1, min(b, ROLL_BLOCK_BYTES // (w * row_bytes)))
    assert b % bb == 0
    tail = (2, ATT_HPG, ATT_HEAD)
    return pl.pallas_call(
        _roll_kernel,
        grid=(b // bb,),
        in_specs=[pl.BlockSpec((1, bb, w) + tail, lambda i: (0, i, 0, 0, 0, 0)),
                  pl.BlockSpec((1, bb, ts) + tail, lambda i: (0, i, 0, 0, 0, 0))],
        out_specs=pl.BlockSpec((1, bb, w) + tail, lambda i: (0, i, 0, 0, 0, 0)),
        out_shape=jax.ShapeDtypeStruct(cache.shape, cache.dtype),
        compiler_params=_cparams(("parallel",)),
        name="cache_roll",
    )(cache, new_rows)


def _merge_kernel(x_ref, yrw_ref, gate_ref, o0_ref, o1_ref, o2_ref, l0_ref, l1_ref, l2_ref,
                  wba_ref, wbb_ref, wo_ref, fnw_ref, rw_ref, rb_ref,
                  h_ref, hn_ref, ti_ref, tg_ref, cnt_ref, run_ref):
    @pl.when(pl.program_id(0) == 0)
    def _():
        run_ref[...] = jnp.zeros(run_ref.shape, F32)

    l0, l1, l2 = l0_ref[...], l1_ref[...], l2_ref[...]
    m = jnp.maximum(jnp.maximum(l0, l1), l2)
    e0, e1, e2 = jnp.exp(l0 - m), jnp.exp(l1 - m), jnp.exp(l2 - m)
    o_att = (e0 * o0_ref[...] + e1 * o1_ref[...] + e2 * o2_ref[...]) / (e0 + e1 + e2)
    br_a = jnp.dot(yrw_ref[...], wba_ref[...], preferred_element_type=F32)
    br_b = jnp.dot(o_att.astype(BF16), wbb_ref[...], preferred_element_type=F32)
    merged = gate_ref[:, 0:D_MODEL] * br_a + gate_ref[:, D_MODEL:2 * D_MODEL] * br_b
    h = x_ref[...] + jnp.dot(merged.astype(BF16), wo_ref[...], preferred_element_type=F32)
    h_ref[...] = h
    hn = h * lax.rsqrt(jnp.mean(h * h, axis=-1, keepdims=True) + NORM_EPS) * fnw_ref[...]
    hn_ref[...] = hn
    logits = jnp.dot(hn.astype(BF16), rw_ref[...], preferred_element_type=F32) + rb_ref[...]
    lane_i = lax.broadcasted_iota(jnp.int32, logits.shape, 1)
    lane = lane_i.astype(F32)
    vals, idxs = [], []
    cur = logits
    for _ in range(TOP_K):
        mx = jnp.max(cur, axis=-1, keepdims=True)
        ix = jnp.min(jnp.where(cur == mx, lane, float(LANES)), axis=-1, keepdims=True)
        vals.append(mx)
        idxs.append(ix)
        cur = jnp.where(lane == ix, -jnp.inf, cur)
    es = [jnp.exp(vv - vals[0]) for vv in vals]
    tot = es[0] + es[1] + es[2] + es[3]
    tm = logits.shape[0]
    ri = lax.broadcasted_iota(jnp.int32, (tm, tm), 0)
    rj = lax.broadcasted_iota(jnp.int32, (tm, tm), 1)
    before = jnp.where(ri > rj, 1.0, 0.0).astype(BF16)
    run = run_ref[0:1, :]
    ranks = []
    for kq in range(TOP_K):
        onehot = jnp.where(lane == idxs[kq], 1.0, 0.0)
        prior = jnp.dot(before, onehot.astype(BF16), preferred_element_type=F32) + run
        ranks.append(jnp.sum(onehot * prior, axis=-1, keepdims=True))
        run = run + jnp.sum(onehot, axis=0, keepdims=True)
    run_ref[...] = jnp.broadcast_to(run, run_ref.shape)
    cnt_ref[...] = jnp.broadcast_to(run, cnt_ref.shape)

    ti = jnp.zeros(logits.shape, F32)
    tg = jnp.zeros(logits.shape, F32)
    for kq in range(TOP_K):
        ti = jnp.where(lane_i == kq, idxs[kq], ti)
        ti = jnp.where(lane_i == TOP_K + kq, ranks[kq], ti)
        tg = jnp.where(lane_i == kq, es[kq] / tot, tg)
    ti_ref[...] = ti.astype(jnp.int32)
    tg_ref[...] = tg


def _merge(x2d, yrw, p2d, o_list, l_list, wba, wbb, wo, fnw, rw_pad, rb_pad, *, tm):
    n = x2d.shape[0]
    row = lambda wdt: pl.BlockSpec((tm, wdt), lambda i: (i, 0))
    full = lambda a: pl.BlockSpec(a.shape, lambda i: (0,) * a.ndim)
    gate_blk = (T_GATE * PROJ_TN) // (2 * D_MODEL)
    return pl.pallas_call(
        _merge_kernel,
        grid=(n // tm,),
        in_specs=[row(D_MODEL), row(D_MODEL),
                  pl.BlockSpec((tm, 2 * D_MODEL), lambda i: (i, gate_blk)),
                  row(ATT_OUT), row(ATT_OUT), row(ATT_OUT), row(ATT_OUT), row(ATT_OUT), row(ATT_OUT),
                  full(wba), full(wbb), full(wo), full(fnw), full(rw_pad), full(rb_pad)],
        out_specs=[row(D_MODEL), row(D_MODEL), row(LANES), row(LANES),
                   pl.BlockSpec((SUBLANES, LANES), lambda i: (0, 0))],
        out_shape=[jax.ShapeDtypeStruct((n, D_MODEL), F32), jax.ShapeDtypeStruct((n, D_MODEL), F32),
                   jax.ShapeDtypeStruct((n, LANES), jnp.int32), jax.ShapeDtypeStruct((n, LANES), F32),
                   jax.ShapeDtypeStruct((SUBLANES, LANES), F32)],
        scratch_shapes=[pltpu.VMEM((SUBLANES, LANES), F32)],
        compiler_params=_cparams(("arbitrary",)),
        name="merge",
    )(x2d, yrw, p2d, *o_list, *l_list, wba, wbb, wo, fnw, rw_pad, rb_pad)


def _row_copy(src_ref, s_row, dst_ref, d_row, sem):
    return pltpu.make_async_copy(src_ref.at[pl.ds(s_row, 1), :], dst_ref.at[pl.ds(d_row, 1), :], sem)


def _dispatch_kernel(pad_start_ref, pad_len_ref, pos_ref, hn_ref, xs_ref, zrow_ref, sem, *, tt):
    i = pl.program_id(0)

    @pl.when(i == 0)
    def _():
        zrow_ref[...] = jnp.zeros(zrow_ref.shape, F32)

        def per_expert(e, carry):
            s = pad_start_ref[e]
            cnt = pad_len_ref[e]

            def issue(rr, cc):
                _row_copy(zrow_ref, 0, xs_ref, s + rr, sem).start()
                return cc

            def drain(rr, cc):
                _row_copy(zrow_ref, 0, xs_ref, s + rr, sem).wait()
                return cc

            lax.fori_loop(0, cnt, issue, 0)
            lax.fori_loop(0, cnt, drain, 0)
            return carry

        lax.fori_loop(0, N_EXPERTS, per_expert, 0)

    def issue(nn, cc):
        for kq in range(TOP_K):
            _row_copy(hn_ref, nn, xs_ref, pos_ref[0, 0, nn * TOP_K + kq], sem).start()
        return cc

    def drain(nn, cc):
        for kq in range(TOP_K):
            _row_copy(hn_ref, nn, xs_ref, pos_ref[0, 0, nn * TOP_K + kq], sem).wait()
        return cc

    lax.fori_loop(0, tt, issue, 0)
    lax.fori_loop(0, tt, drain, 0)


def _dispatch(hn, pos3, pad_start, pad_len, rows, *, tt):
    n = hn.shape[0]
    return pl.pallas_call(
        functools.partial(_dispatch_kernel, tt=tt),
        grid_spec=pltpu.PrefetchScalarGridSpec(
            num_scalar_prefetch=2,
            grid=(n // tt,),
            in_specs=[
                pl.BlockSpec((1, 1, tt * TOP_K), lambda i, ps, plen: (i, 0, 0), memory_space=pltpu.SMEM),
                pl.BlockSpec((tt, D_MODEL), lambda i, ps, plen: (i, 0)),
            ],
            out_specs=pl.BlockSpec(memory_space=pl.ANY),
            scratch_shapes=[pltpu.VMEM((SUBLANES, D_MODEL), F32), pltpu.SemaphoreType.DMA(())],
        ),
        out_shape=jax.ShapeDtypeStruct((rows, D_MODEL), F32),
        compiler_params=_cparams(("arbitrary",)),
        name="moe_dispatch",
    )(pad_start, pad_len, pos3, hn)


def _moe_kernel(be_ref, nu_ref, xs_ref, w1_ref, b1_ref, w2_ref, b2_ref, y_ref):
    i = pl.program_id(0)

    @pl.when(i < nu_ref[0])
    def _():
        x = xs_ref[...].astype(BF16)
        hdn = jnp.dot(x, w1_ref[0], preferred_element_type=F32) + b1_ref[0]
        glu = jnp.minimum(hdn[:, 0:D_MODEL], SWIGLU_LIMIT)
        lin = jnp.clip(hdn[:, D_MODEL:2 * D_MODEL], -SWIGLU_LIMIT, SWIGLU_LIMIT)
        act = glu * jax.nn.sigmoid(SWIGLU_ALPHA * glu) * (lin + 1.0)
        y_ref[...] = jnp.dot(act.astype(BF16), w2_ref[0], preferred_element_type=F32) + b2_ref[0]

    @pl.when(i >= nu_ref[0])
    def _():
        y_ref[...] = jnp.zeros(y_ref.shape, F32)


def _moe(xs, blk_expert, n_used, w1, b1, w2, b2, *, tm):
    rows = xs.shape[0]
    nb = rows // tm
    return pl.pallas_call(
        _moe_kernel,
        grid_spec=pltpu.PrefetchScalarGridSpec(
            num_scalar_prefetch=2,
            grid=(nb,),
            in_specs=[
                pl.BlockSpec((tm, D_MODEL), lambda i, be, nu: (jnp.minimum(i, nu[0] - 1), 0)),
                pl.BlockSpec((1, D_MODEL, 2 * D_MODEL), lambda i, be, nu: (be[i], 0, 0)),
                pl.BlockSpec((1, 1, 2 * D_MODEL), lambda i, be, nu: (be[i], 0, 0)),
                pl.BlockSpec((1, D_MODEL, D_MODEL), lambda i, be, nu: (be[i], 0, 0)),
                pl.BlockSpec((1, 1, D_MODEL), lambda i, be, nu: (be[i], 0, 0)),
            ],
            out_specs=pl.BlockSpec((tm, D_MODEL), lambda i, be, nu: (i, 0)),
        ),
        out_shape=jax.ShapeDtypeStruct((rows, D_MODEL), F32),
        compiler_params=_cparams(("arbitrary",)),
        name="moe_experts",
    )(blk_expert, n_used, xs, w1, b1, w2, b2)


def _combine_kernel(pos_ref, h_ref, tg_ref, yb_ref, o_ref, buf_ref, sem, *, tt):
    def issue(nn, cc):
        for kq in range(TOP_K):
            _row_copy(yb_ref, pos_ref[0, 0, nn * TOP_K + kq], buf_ref.at[kq], nn, sem).start()
        return cc

    def drain(nn, cc):
        for kq in range(TOP_K):
            _row_copy(yb_ref, pos_ref[0, 0, nn * TOP_K + kq], buf_ref.at[kq], nn, sem).wait()
        return cc

    lax.fori_loop(0, tt, issue, 0)
    lax.fori_loop(0, tt, drain, 0)
    acc = h_ref[...]
    for kq in range(TOP_K):
        acc = acc + tg_ref[:, kq:kq + 1] * buf_ref[kq]
    o_ref[...] = acc


def _combine(h, tg, pos3, yb, *, tt):
    n = h.shape[0]
    return pl.pallas_call(
        functools.partial(_combine_kernel, tt=tt),
        grid=(n // tt,),
        in_specs=[
            pl.BlockSpec((1, 1, tt * TOP_K), lambda i: (i, 0, 0), memory_space=pltpu.SMEM),
            pl.BlockSpec((tt, D_MODEL), lambda i: (i, 0)),
            pl.BlockSpec((tt, LANES), lambda i: (i, 0)),
            pl.BlockSpec(memory_space=pl.ANY),
        ],
        out_specs=pl.BlockSpec((tt, D_MODEL), lambda i: (i, 0)),
        out_shape=jax.ShapeDtypeStruct((n, D_MODEL), F32),
        scratch_shapes=[pltpu.VMEM((TOP_K, tt, D_MODEL), F32), pltpu.SemaphoreType.DMA(())],
        compiler_params=_cparams(("arbitrary",)),
        name="moe_combine",
    )(pos3, h, tg, yb)


def _routing(top_idx, rank, counts, tm):
    n = top_idx.shape[0]
    nk = n * TOP_K
    experts = jnp.arange(N_EXPERTS, dtype=jnp.int32)
    blocks_e = (counts + tm - 1) // tm
    upper = (experts[:, None] <= experts[None, :]).astype(jnp.int32)
    block_end = jnp.sum(blocks_e[:, None] * upper, axis=0)
    row_start = (block_end - blocks_e) * tm
    onehot = (top_idx[:, :, None] == experts[None, None, :]).astype(jnp.int32)
    pos = (jnp.sum(onehot * row_start[None, None, :], axis=-1) + rank).reshape(nk).astype(jnp.int32)
    n_blocks = -(-nk // tm) + N_EXPERTS
    blk_ids = jnp.arange(n_blocks, dtype=jnp.int32)
    blk_expert = jnp.minimum(
        jnp.sum((block_end[None, :] <= blk_ids[:, None]).astype(jnp.int32), axis=1), N_EXPERTS - 1
    ).astype(jnp.int32)
    n_used = block_end[-1:].astype(jnp.int32)
    pad_start = (row_start + counts).astype(jnp.int32)
    pad_len = (blocks_e * tm - counts).astype(jnp.int32)
    return pos, blk_expert, n_used, pad_start, pad_len, n_blocks * tm


def _pad_cols(a, width):
    return jnp.pad(a, ((0, 0), (0, width - a.shape[1])))


def _pad_rows(a, height):
    return jnp.pad(a, ((0, height - a.shape[0]), (0, 0)))


def _proj_columns(a):
    d = D_MODEL
    xw = a[:, 3 * d:3 * d + DECAY_LORA]
    xa = a[:, 3 * d + DECAY_LORA:3 * d + DECAY_LORA + AAA_LORA]
    xg = a[:, 3 * d + DECAY_LORA + AAA_LORA:C_RW]
    att = a[:, C_RW:C_RW + 3 * ATT_DIM]
    q, k, v = att[:, :ATT_DIM], att[:, ATT_DIM:2 * ATT_DIM], att[:, 2 * ATT_DIM:]
    parts = [a[:, :3 * d], _pad_cols(xw, LANES), _pad_cols(xa, LANES), _pad_cols(xg, 2 * LANES), q]
    for g in range(N_GROUPS):
        parts.append(k[:, g * ATT_OUT:(g + 1) * ATT_OUT])
        parts.append(v[:, g * ATT_OUT:(g + 1) * ATT_OUT])
    parts.append(a[:, C_RW + 3 * ATT_DIM:])
    return jnp.concatenate(parts, axis=1)


def _rope_tables(pos):
    inv = ROPE_THETA ** (-jnp.arange(0, ATT_HEAD, 2, dtype=F32) / ATT_HEAD)
    ang = pos.astype(F32)[:, None] * inv[None, :]
    cos, sin = jnp.cos(ang), jnp.sin(ang)
    return jnp.concatenate([cos, cos], axis=1), jnp.concatenate([-sin, sin], axis=1)


def _prep_weights(attn_norm_w, w_in, rw_mu, rw_w0, rw_w2, rw_a0, rw_a2, rw_g2, rw_kk, rw_ka, rw_rk,
                  rw_lnx_w, rw_lnx_b, q_norm_w, k_norm_w, w_br, w_out, ffn_norm_w, router_w, router_b,
                  moe_w1, moe_b1, moe_w2, moe_b2):
    w = {}
    w['nw'] = attn_norm_w.reshape(1, D_MODEL)
    w['w_pad'] = _proj_columns(w_in).astype(BF16)
    w['mu_pad'] = _proj_columns(_pad_cols(rw_mu.reshape(1, C_RW), C_RW + 3 * ATT_DIM + 2 * D_MODEL))[:, :RW_PAD]
    w['par'] = jnp.concatenate([rw_w0.reshape(1, -1), rw_a0.reshape(1, -1), rw_kk.reshape(1, -1),
                                rw_ka.reshape(1, -1), rw_rk.reshape(1, -1), rw_lnx_w.reshape(1, -1),
                                rw_lnx_b.reshape(1, -1), jnp.zeros((1, D_MODEL), F32)], axis=0)
    w['w2p'] = _pad_rows(rw_w2, LANES).astype(BF16)
    w['a2p'] = _pad_rows(rw_a2, LANES).astype(BF16)
    w['g2p'] = _pad_rows(rw_g2, 2 * LANES).astype(BF16)
    w['qn'] = q_norm_w.reshape(1, ATT_HEAD)
    w['kn'] = k_norm_w.reshape(1, ATT_HEAD)
    w['wba'] = w_br[:D_MODEL].astype(BF16)
    w['wbb'] = w_br[D_MODEL:].astype(BF16)
    w['wo'] = w_out.astype(BF16)
    w['fnw'] = ffn_norm_w.reshape(1, D_MODEL)
    w['rw_pad'] = _pad_cols(router_w, LANES).astype(BF16)
    w['rb_pad'] = jnp.concatenate([router_b.reshape(1, N_EXPERTS),
                                   jnp.full((1, LANES - N_EXPERTS), -1e30, F32)], axis=1)
    cols = jnp.arange(2 * D_MODEL, dtype=jnp.int32)
    src = jnp.where(cols < D_MODEL, 2 * cols, 2 * (cols - D_MODEL) + 1)
    perm = (jnp.arange(2 * D_MODEL, dtype=jnp.int32)[:, None] == src[None, :]).astype(BF16)
    w['w1'] = jnp.einsum('edh,hk->edk', moe_w1.astype(BF16), perm, preferred_element_type=BF16)
    w['b1'] = jnp.concatenate([moe_b1[:, 0::2], moe_b1[:, 1::2]], axis=1).reshape(N_EXPERTS, 1, 2 * D_MODEL)
    w['w2'] = moe_w2.astype(BF16)
    w['b2'] = moe_b2.reshape(N_EXPERTS, 1, D_MODEL)
    return w


def _state_to_pairs(s):
    b = s.shape[0]
    sr = s.reshape(b, RWKV_HEADS // 2, 2, RWKV_HEAD, RWKV_HEAD)
    z = jnp.zeros_like(sr[:, :, 0])
    top = jnp.concatenate([sr[:, :, 0], z], axis=-1)
    bot = jnp.concatenate([z, sr[:, :, 1]], axis=-1)
    return jnp.concatenate([top, bot], axis=-2)


def _pairs_to_state(zp):
    b = zp.shape[0]
    zr = zp.reshape(b, RWKV_HEADS // 2, 2, RWKV_HEAD, 2, RWKV_HEAD)
    return jnp.stack([zr[:, :, 0, :, 0, :], zr[:, :, 1, :, 1, :]], axis=2).reshape(b, RWKV_HEADS, RWKV_HEAD, RWKV_HEAD)


def _layer(x, pos, shift_prev, wkv0, caches, w, *, proj_tm, merge_tm, moe_tm, tok_tt):
    b, t, _ = x.shape
    n = b * t
    x2d = x.reshape(n, D_MODEL)
    is_prompt = caches is None

    cs_tab, sn_tab = _rope_tables(pos)
    if not is_prompt:
        cs_tab = jnp.tile(cs_tab, (proj_tm // t, 1))
        sn_tab = jnp.tile(sn_tab, (proj_tm // t, 1))
    p2d = _proj(x2d, w['nw'], w['w_pad'], w['qn'], w['kn'], cs_tab, sn_tab,
                tm=proj_tm, n_tiles=N_TILES, do_norm=True)
    p3 = p2d.reshape(b, t, P_PAD)
    shift_out = _rmsnorm_rows(x[:, t - 1, :], w['nw'])

    if is_prompt:
        prev0 = jnp.zeros((b, 1, RW_PAD), F32)
        s0z = jnp.zeros((b, RWKV_HEADS // 2, LANES, LANES), F32)
        yrw, sfin = _rwkv(p3, prev0, s0z, w['mu_pad'], w['par'], w['w2p'], w['a2p'], w['g2p'],
                          chunk=SCAN_CHUNK, n_valid=None)
        yrw = yrw.reshape(n, D_MODEL)
    else:
        one = jnp.ones((1, ATT_HEAD), F32)
        zero_tab = jnp.zeros((b, ATT_HEAD), F32)
        prev_rw = _proj(shift_prev, w['nw'], w['w_pad'], one, one, zero_tab, zero_tab,
                        tm=b, n_tiles=T_LORA + 1, do_norm=False)
        t_pad = SCAN_CHUNK
        p_rw = jnp.pad(p3[:, :, :RW_PAD], ((0, 0), (0, t_pad - t), (0, 0)))
        yrw, sfin = _rwkv(p_rw, prev_rw.reshape(b, 1, RW_PAD), _state_to_pairs(wkv0), w['mu_pad'], w['par'],
                          w['w2p'], w['a2p'], w['g2p'], chunk=SCAN_CHUNK, n_valid=t)
        yrw = yrw[:, :t].reshape(n, D_MODEL)
    s_fin = _pairs_to_state(sfin)

    new_kv = []
    if is_prompt:
        o_list, l_list = [], []
        for g, (window, dil) in enumerate(DILATION_GROUPS):
            o_g, l_g = _attn_prompt(p3, g, dil, window // dil)
            o_list.append(o_g)
            l_list.append(l_g)
            keep = min(window, t)
            new_kv.append(_kv_rows(p3, g, keep, min(keep, 256))[None])
    else:
        kv_col = T_KV * PROJ_TN
        q5 = p3[:, :, T_Q * PROJ_TN:T_KV * PROJ_TN].reshape(b, t, N_GROUPS, ATT_HPG, ATT_HEAD)
        kv_new = [p3[:, :, kv_col + g * 2 * ATT_OUT:kv_col + (g + 1) * 2 * ATT_OUT].reshape(b, t, 2, ATT_HPG, ATT_HEAD)
                  for g in range(N_GROUPS)]
        o_list, l_list = _attn_sample(q5, kv_new, caches)
        new_kv = [_cache_roll(caches[g], kv_new[g][None]) for g in range(N_GROUPS)]

    h, hn, ti, tg, cnt = _merge(x2d, yrw, p2d, o_list, l_list, w['wba'], w['wbb'], w['wo'], w['fnw'],
                                w['rw_pad'], w['rb_pad'], tm=merge_tm)

    counts = cnt[0, :N_EXPERTS].astype(jnp.int32)
    pos_rows, blk_expert, n_used, pad_start, pad_len, rows = _routing(
        ti[:, :TOP_K], ti[:, TOP_K:2 * TOP_K], counts, moe_tm)
    pos3 = pos_rows.reshape(n // tok_tt, 1, tok_tt * TOP_K)
    xs = _dispatch(hn, pos3, pad_start, pad_len, rows, tt=tok_tt)
    yb = _moe(xs, blk_expert, n_used, w['w1'], w['b1'], w['w2'], w['b2'], tm=moe_tm)
    y = _combine(h, tg, pos3, yb, tt=tok_tt)
    return y.reshape(b, t, D_MODEL), shift_out, s_fin, new_kv


def kernel(x_prompt, x_sample, state_wkv, state_shift, cache_kv_w128, cache_kv_w512, cache_kv_w2048,
           attn_norm_w, w_in, rw_mu, rw_w0, rw_w2, rw_a0, rw_a2, rw_g2, rw_kk, rw_ka, rw_rk,
           rw_lnx_w, rw_lnx_b, q_norm_w, k_norm_w, w_br, w_out, ffn_norm_w, router_w, router_b,
           moe_w1, moe_b1, moe_w2, moe_b2):
    depth = w_in.shape[0]
    assert depth == 1
    bp, tp, _ = x_prompt.shape
    bs, ts, _ = x_sample.shape
    l = 0
    w = _prep_weights(attn_norm_w[l], w_in[l], rw_mu[l], rw_w0[l], rw_w2[l], rw_a0[l], rw_a2[l], rw_g2[l],
                      rw_kk[l], rw_ka[l], rw_rk[l], rw_lnx_w[l], rw_lnx_b[l], q_norm_w[l], k_norm_w[l],
                      w_br[l], w_out[l], ffn_norm_w[l], router_w[l], router_b[l],
                      moe_w1[l], moe_b1[l], moe_w2[l], moe_b2[l])
    pos_p = jnp.arange(tp, dtype=jnp.int32)
    pos_s = PAST_LEN + jnp.arange(ts, dtype=jnp.int32)
    yp, sh_p, wkv_p, kv_p = _layer(x_prompt, pos_p, None, None, None, w,
                                   proj_tm=min(1024, tp), merge_tm=256, moe_tm=MOE_TM, tok_tt=256)
    ns = bs * ts
    ys, sh_s, wkv_s, kv_s = _layer(x_sample, pos_s, state_shift[l], state_wkv[l],
                                   (cache_kv_w128, cache_kv_w512, cache_kv_w2048), w,
                                   proj_tm=ns, merge_tm=min(256, ns), moe_tm=128, tok_tt=min(256, ns))
    return (yp, ys,
            wkv_p[None], sh_p[None], kv_p[0], kv_p[1], kv_p[2],
            wkv_s[None], sh_s[None], kv_s[0], kv_s[1], kv_s[2])
```

```python
import functools

import jax
import jax.numpy as jnp
from jax import lax
from jax.experimental import pallas as pl
from jax.experimental.pallas import tpu as pltpu

F32 = jnp.float32
BF16 = jnp.bfloat16

D_MODEL = 1024
NORM_EPS = 1e-5
RWKV_HEAD = 64
RWKV_HEADS = D_MODEL // RWKV_HEAD
DECAY_LORA = 64
AAA_LORA = 64
GATE_LORA = 160
GN_EPS = 64e-5
C_RW = 3 * D_MODEL + DECAY_LORA + AAA_LORA + GATE_LORA
ATT_HEAD = 128
ATT_HPG = 4
DILATION_GROUPS = ((128, 1), (512, 4), (2048, 16))
N_GROUPS = 3
ATT_DIM = N_GROUPS * ATT_HPG * ATT_HEAD
ATT_OUT = ATT_HPG * ATT_HEAD
ROPE_THETA = 10000.0
N_EXPERTS = 32
TOP_K = 4
SWIGLU_ALPHA = 1.702
SWIGLU_LIMIT = 7.0
PAST_LEN = 8192

LANES = 128
SUBLANES = 8
VMEM_LIMIT = 56 * 1024 * 1024

PROJ_TN = 512
T_RKV = 0
T_LORA = 6
T_Q = 7
T_KV = 10
T_GATE = 16
N_TILES = 20
P_PAD = N_TILES * PROJ_TN
RW_PAD = (T_LORA + 1) * PROJ_TN

SCAN_CHUNK = 64
RWKV_ROWS_PER_STEP = 4
RWKV_ROWS_TOGETHER = 2
MOE_TM = 512


def _cparams(sem, vmem=VMEM_LIMIT):
    return pltpu.CompilerParams(dimension_semantics=sem, vmem_limit_bytes=vmem)


def _proj_kernel(x_ref, nw_ref, w_ref, qn_ref, kn_ref, cs_ref, sn_ref, o_ref, xn_ref, *, do_norm):
    j = pl.program_id(1)

    @pl.when(j == 0)
    def _():
        x = x_ref[...]
        if do_norm:
            x = x * lax.rsqrt(jnp.mean(x * x, axis=-1, keepdims=True) + NORM_EPS) * nw_ref[...]
        xn_ref[...] = x.astype(BF16)

    acc = jnp.dot(xn_ref[...], w_ref[...], preferred_element_type=F32)
    is_q = (j >= T_Q) & (j < T_KV)
    is_k = (j >= T_KV) & (j < T_GATE) & ((j - T_KV) % 2 == 0)
    is_qk = is_q | is_k
    is_gate = j >= T_GATE

    @pl.when(is_qk)
    def _():
        nwh = jnp.where(j < T_KV, qn_ref[...], kn_ref[...])
        cs = cs_ref[...]
        sn = sn_ref[...]
        for h in range(ATT_HPG):
            xh = acc[:, h * ATT_HEAD:(h + 1) * ATT_HEAD]
            y = xh * lax.rsqrt(jnp.mean(xh * xh, axis=-1, keepdims=True) + NORM_EPS) * nwh
            o_ref[:, h * ATT_HEAD:(h + 1) * ATT_HEAD] = y * cs + pltpu.roll(y, ATT_HEAD // 2, axis=1) * sn

    @pl.when(is_gate)
    def _():
        o_ref[...] = jax.nn.sigmoid(acc)

    @pl.when(jnp.logical_not(is_qk | is_gate))
    def _():
        o_ref[...] = acc


def _proj(x2d, nw, w_pad, qn, kn, cs_tab, sn_tab, *, tm, n_tiles, do_norm):
    n = x2d.shape[0]
    tab_blocks = cs_tab.shape[0] // tm
    return pl.pallas_call(
        functools.partial(_proj_kernel, do_norm=do_norm),
        grid=(n // tm, n_tiles),
        in_specs=[
            pl.BlockSpec((tm, D_MODEL), lambda i, j: (i, 0)),
            pl.BlockSpec((1, D_MODEL), lambda i, j: (0, 0)),
            pl.BlockSpec((D_MODEL, PROJ_TN), lambda i, j: (0, j)),
            pl.BlockSpec((1, ATT_HEAD), lambda i, j: (0, 0)),
            pl.BlockSpec((1, ATT_HEAD), lambda i, j: (0, 0)),
            pl.BlockSpec((tm, ATT_HEAD), lambda i, j: (i % tab_blocks, 0)),
            pl.BlockSpec((tm, ATT_HEAD), lambda i, j: (i % tab_blocks, 0)),
        ],
        out_specs=pl.BlockSpec((tm, PROJ_TN), lambda i, j: (i, j)),
        out_shape=jax.ShapeDtypeStruct((n, n_tiles * PROJ_TN), F32),
        scratch_shapes=[pltpu.VMEM((tm, D_MODEL), BF16)],
        compiler_params=_cparams(("parallel", "arbitrary")),
        name="proj",
    )(x2d, nw, w_pad, qn, kn, cs_tab, sn_tab)


def _rmsnorm_kernel(x_ref, w_ref, o_ref):
    x = x_ref[...]
    o_ref[...] = x * lax.rsqrt(jnp.mean(x * x, axis=-1, keepdims=True) + NORM_EPS) * w_ref[...]


def _rmsnorm_rows(x2d, w):
    return pl.pallas_call(
        _rmsnorm_kernel,
        out_shape=jax.ShapeDtypeStruct(x2d.shape, F32),
        name="rmsnorm_rows",
    )(x2d, w)


def _split3(x):
    h1 = x.astype(BF16)
    r1 = x - h1.astype(F32)
    h2 = r1.astype(BF16)
    h3 = (r1 - h2.astype(F32)).astype(BF16)
    return h1, h2, h3


def _seg_sum(x, seg_ones):
    rows = x.shape[0]
    hi = x.astype(BF16)
    lo = (x - hi.astype(F32)).astype(BF16)
    hl = jnp.concatenate([hi, lo], axis=0)
    outs = []
    for p in range(D_MODEL // LANES):
        s = jnp.dot(hl[:, p * LANES:(p + 1) * LANES], seg_ones, preferred_element_type=F32)
        outs.append(s[:rows] + s[rows:])
    return jnp.concatenate(outs, axis=1)


def _nt_dot(a, b):
    return lax.dot_general(a, b, (((1,), (1,)), ((), ())), preferred_element_type=F32)


def _rwkv_kernel(p_ref, prev0_ref, s0_ref, mu_ref, par_ref, w2_ref, a2_ref, g2_ref,
                 y_ref, st_ref, prev_ref, *, chunk, n_valid, together):
    c = pl.program_id(1)
    C = chunk
    rows_per_step = p_ref.shape[0]

    @pl.when(c == 0)
    def _():
        st_ref[...] = s0_ref[...]
        prev_ref[...] = prev0_ref[...]

    w0 = par_ref[0:1, :]
    a0 = par_ref[1:2, :]
    k_k = par_ref[2:3, :]
    k_a = par_ref[3:4, :]
    r_k = par_ref[4:5, :]
    lnx_w = par_ref[5:6, :]
    lnx_b = par_ref[6:7, :]

    row = lax.broadcasted_iota(jnp.int32, (C, 1), 0)
    li = lax.broadcasted_iota(jnp.int32, (LANES, LANES), 0)
    lj = lax.broadcasted_iota(jnp.int32, (LANES, LANES), 1)
    same_head = (li < RWKV_HEAD) == (lj < RWKV_HEAD)
    seg_ones = jnp.where(same_head, 1.0, 0.0).astype(BF16)
    blk_mask = jnp.where(same_head, 1.0, 0.0)
    ti = lax.broadcasted_iota(jnp.int32, (C, C), 0)
    tj = lax.broadcasted_iota(jnp.int32, (C, C), 1)
    tril_b = jnp.where(ti >= tj, 1.0, 0.0).astype(BF16)
    ti2 = lax.broadcasted_iota(jnp.int32, (C, 2 * C), 0)
    tj2 = lax.broadcasted_iota(jnp.int32, (C, 2 * C), 1) & (C - 1)
    tril_incl2 = jnp.where(ti2 >= tj2, 1.0, 0.0)
    tril_strict2 = jnp.where(ti2 > tj2, 1.0, 0.0)
    lane = lax.broadcasted_iota(jnp.int32, (1, LANES), 1)
    first = lane < RWKV_HEAD
    n_lev = max(1, (C - 1).bit_length())
    pairs = range(D_MODEL // LANES)
    sls = [slice(pr * LANES, (pr + 1) * LANES) for pr in pairs]
    dot = functools.partial(jnp.dot, preferred_element_type=F32)
    head_sel = lambda x0, x1: jnp.where(first, x0, x1)

    def prepare(gi, out):
        p = p_ref[gi]
        shifted = jnp.where(row == 0, prev_ref[gi], pltpu.roll(p, 1, axis=0))
        prev_ref[gi] = p[C - 1:C, :]
        ps = p + (shifted - p) * mu_ref[...]
        r = ps[:, 0:D_MODEL]
        k = ps[:, D_MODEL:2 * D_MODEL]
        v = ps[:, 2 * D_MODEL:3 * D_MODEL]
        xw = ps[:, 3 * D_MODEL:3 * D_MODEL + LANES]
        xa = ps[:, 3 * D_MODEL + LANES:3 * D_MODEL + 2 * LANES]
        xg = ps[:, 3 * D_MODEL + 2 * LANES:RW_PAD]
        yield
        zw = w0 + dot(jnp.tanh(xw).astype(BF16), w2_ref[...])
        nz = -zw
        softplus = jnp.log(1.0 + jnp.exp(-jnp.abs(nz))) + jnp.maximum(nz, 0.0)
        lw = -jnp.exp(-softplus - 0.5)
        a = jax.nn.sigmoid(a0 + dot(xa.astype(BF16), a2_ref[...]))
        g = dot(jax.nn.sigmoid(xg).astype(BF16), g2_ref[...])
        yield
        kkr = k * k_k
        kk = kkr / jnp.maximum(jnp.sqrt(_seg_sum(kkr * kkr, seg_ones)), 1e-12)
        yield
        k2 = k * (1.0 + (a - 1.0) * k_a)
        bonus = _seg_sum(r * k2 * r_k, seg_ones) * v
        yield
        if n_valid is not None:
            valid = (c * C + row) < n_valid
            lw = jnp.where(valid, lw, 0.0)
            kk = jnp.where(valid, kk, 0.0)
            k2 = jnp.where(valid, k2, 0.0)
        l1, l2, l3 = _split3(lw)
        cl = dot(tril_b, l1) + dot(tril_b, l2) + dot(tril_b, l3)
        cl_end = cl[C - 1:C, :]
        yield
        kb = kk * a
        e_inv = jnp.exp(-cl)
        e_tail = jnp.exp(cl_end - cl)
        out.update(v=v, g=g, bonus=bonus, a_t=(-kk) * jnp.exp(cl - lw), r_t=r * jnp.exp(cl))
        yield
        out.update(b_t=kb * e_inv, k_t=k2 * e_inv, b_h=kb * e_tail, k_h=k2 * e_tail, w_end=jnp.exp(cl_end))

    def finish(gi, q, y):
        inv_n = 1.0 / RWKV_HEAD
        mean = _seg_sum(y, seg_ones) * inv_n
        yc = y - mean
        yield
        var = _seg_sum(yc * yc, seg_ones) * inv_n
        yield
        yn = yc * lax.rsqrt(var + GN_EPS) * lnx_w + lnx_b
        y_ref[gi] = ((yn + q['bonus']) * q['g']).astype(y_ref.dtype)

    def scan_chunk(gis, qs, fillers):
        def fill():
            for f in fillers:
                next(f, None)

        units = [(n, pr) for n in range(len(gis)) for pr in pairs]
        by_head = lambda x: jnp.concatenate([jnp.where(first, x, 0.0), jnp.where(first, 0.0, x)],
                                            axis=0).astype(BF16)
        zs = {(n, pr): st_ref[gis[n], pr] for n, pr in units}
        ars = {(n, pr): jnp.concatenate([qs[n]['a_t'][:, sls[pr]], qs[n]['r_t'][:, sls[pr]]], axis=0)
               for n, pr in units}
        q0s = {u: _nt_dot(ars[u].astype(BF16), zs[u].astype(BF16)) for u in units}
        fill()
        xx, lk, mb, mk, vsw = {}, {}, {}, {}, {}
        for u in units:
            n, pr = u
            btb = qs[n]['b_t'][:, sls[pr]].astype(BF16)
            ktb = qs[n]['k_t'][:, sls[pr]].astype(BF16)
            g0 = _nt_dot(jnp.where(first, ars[u], 0.0).astype(BF16), jnp.concatenate([btb, ktb], axis=0))
            g1 = _nt_dot(jnp.where(first, 0.0, ars[u]).astype(BF16), jnp.concatenate([ktb, btb], axis=0))
            xx[u] = head_sel(g0[:C], g1[:C]) * tril_strict2
            lk[u] = (head_sel(g1[:C], g0[:C]) * tril_strict2).astype(BF16)
            mb[u] = (head_sel(g0[C:], g1[C:]) * tril_incl2).astype(BF16)
            mk[u] = (head_sel(g1[C:], g0[C:]) * tril_incl2).astype(BF16)
            v_p = qs[n]['v'][:, sls[pr]]
            vsw[u] = jnp.concatenate([jnp.where(first, 0.0, v_p), jnp.where(first, v_p, 0.0)],
                                     axis=0).astype(BF16)
        fill()
        us = {u: q0s[u][:C] + dot(lk[u], vsw[u]) for u in units}
        for lev in range(n_lev):
            us = {u: us[u] + dot(xx[u].astype(BF16), by_head(us[u])) for u in units}
            if lev < n_lev - 1:
                xx = {u: dot(xx[u].astype(BF16), by_head(xx[u])) for u in units}
            fill()
        ys = {u: q0s[u][C:] + dot(jnp.concatenate([mb[u], mk[u]], axis=1),
                                  jnp.concatenate([by_head(us[u]), vsw[u]], axis=0)) for u in units}
        for u in units:
            n, pr = u
            sl = sls[pr]
            uv_t = jnp.concatenate([us[u], qs[n]['v'][:, sl]], axis=0).T.astype(BF16)
            bk_h = jnp.concatenate([qs[n]['b_h'][:, sl], qs[n]['k_h'][:, sl]], axis=0).astype(BF16)
            st_ref[gis[n], pr] = zs[u] * qs[n]['w_end'][:, sl] + blk_mask * dot(uv_t, bk_h)
        for f in fillers:
            for _ in f:
                pass
        return [jnp.concatenate([ys[n, pr] for pr in pairs], axis=1) for n in range(len(gis))]

    def run_all(gens):
        for gen in gens:
            for _ in gen:
                pass

    def chain(gens):
        for gen in gens:
            yield from gen

    groups = [list(range(s0, min(s0 + together, rows_per_step))) for s0 in range(0, rows_per_step, together)]
    prepared = [dict() for _ in groups[0]]
    run_all([prepare(gi, prepared[n]) for n, gi in enumerate(groups[0])])
    finishing = []
    for gidx, gis in enumerate(groups):
        following, fillers = [], list(finishing)
        if gidx + 1 < len(groups):
            following = [dict() for _ in groups[gidx + 1]]
            fillers.append(chain([prepare(gi, following[n]) for n, gi in enumerate(groups[gidx + 1])]))
        ys = scan_chunk(gis, prepared, fillers)
        finishing = [chain([finish(gi, prepared[n], ys[n]) for n, gi in enumerate(gis)])]
        prepared = following
    run_all(finishing)


def _rwkv(p3, prev0, s0z, mu_pad, par, w2p, a2p, g2p, *, chunk, n_valid, rows_per_step):
    b, t, _ = p3.shape
    n_pairs = D_MODEL // LANES
    g = rows_per_step
    assert b % g == 0
    full = lambda shape: pl.BlockSpec(shape, lambda i, c: (0,) * len(shape))
    return pl.pallas_call(
        functools.partial(_rwkv_kernel, chunk=chunk, n_valid=n_valid, together=RWKV_ROWS_TOGETHER),
        grid=(b // g, t // chunk),
        in_specs=[
            pl.BlockSpec((g, chunk, RW_PAD), lambda i, c: (i, c, 0)),
            pl.BlockSpec((g, 1, RW_PAD), lambda i, c: (i, 0, 0)),
            pl.BlockSpec((g, n_pairs, LANES, LANES), lambda i, c: (i, 0, 0, 0)),
            full((1, RW_PAD)),
            full((SUBLANES, D_MODEL)),
            full((LANES, D_MODEL)),
            full((LANES, D_MODEL)),
            full((2 * LANES, D_MODEL)),
        ],
        out_specs=[
            pl.BlockSpec((g, chunk, D_MODEL), lambda i, c: (i, c, 0)),
            pl.BlockSpec((g, n_pairs, LANES, LANES), lambda i, c: (i, 0, 0, 0)),
        ],
        out_shape=[
            jax.ShapeDtypeStruct((b, t, D_MODEL), BF16),
            jax.ShapeDtypeStruct((b, n_pairs, LANES, LANES), F32),
        ],
        scratch_shapes=[pltpu.VMEM((g, 1, RW_PAD), F32)],
        compiler_params=_cparams(("parallel", "arbitrary")),
        name="rwkv",
    )(p3, prev0, s0z, mu_pad, par, w2p, a2p, g2p)


def _attn_kernel(*refs, blk, dil, subs, has_prev):
    if has_prev:
        q_ref, ko_ref, vo_ref, kp_ref, vp_ref, o_ref, l_ref = refs
    else:
        q_ref, ko_ref, vo_ref, o_ref, l_ref = refs
    n = pl.program_id(1)
    unit = dil * blk
    scale = ATT_HEAD ** -0.5

    def class_rows(j, r):
        return pl.ds(j * unit + r, blk, stride=dil) if dil > 1 else pl.ds(j * unit, blk)

    for j in range(subs):
        two = has_prev or j > 0
        nk = 2 * blk if two else blk
        qi = lax.broadcasted_iota(jnp.int32, (blk, nk), 0)
        ki = lax.broadcasted_iota(jnp.int32, (blk, nk), 1)
        if two:
            valid = (ki >= qi) & (ki <= qi + blk)
            if j == 0:
                valid = valid & (ki >= jnp.where(n == 0, blk, 0))
        else:
            valid = ki <= qi
        for r in range(dil):
            rows = class_rows(j, r)
            q = q_ref[0, rows, :].astype(BF16)
            kc = ko_ref[0, rows, :]
            vc = vo_ref[0, rows, :]
            if j > 0:
                kc = jnp.concatenate([ko_ref[0, class_rows(j - 1, r), :], kc], axis=0)
                vc = jnp.concatenate([vo_ref[0, class_rows(j - 1, r), :], vc], axis=0)
            elif has_prev:
                kc = jnp.concatenate([kp_ref[0, class_rows(0, r), :], kc], axis=0)
                vc = jnp.concatenate([vp_ref[0, class_rows(0, r), :], vc], axis=0)
            s = _nt_dot(q, kc.astype(BF16)) * scale
            s = jnp.where(valid, s, -jnp.inf)
            m = jnp.max(s, axis=-1, keepdims=True)
            pexp = jnp.exp(s - m)
            den = jnp.sum(pexp, axis=-1, keepdims=True)
            o = jnp.dot(pexp.astype(BF16), vc.astype(BF16), preferred_element_type=F32) / den
            o_ref[0, rows, :] = o
            l_ref[0, rows, :] = jnp.broadcast_to(m + jnp.log(den), (blk, ATT_HEAD))


ATTN_STEP_ROWS = 1024


def _attn_prompt(p3, g, dil, blk):
    b, t, _ = p3.shape
    unit = dil * blk
    assert t % unit == 0
    subs = max(1, min(ATTN_STEP_ROWS, t) // unit)
    span = unit * subs
    assert t % span == 0
    nb = t // span
    has_prev = nb > 1
    q_col = (T_Q * PROJ_TN + g * ATT_OUT) // ATT_HEAD
    k_col = (T_KV * PROJ_TN + g * 2 * ATT_OUT) // ATT_HEAD
    v_col = k_col + ATT_HPG
    cur = lambda col: pl.BlockSpec((1, span, ATT_HEAD), lambda bi, n, h: (bi, n, col + h))
    prev = lambda col: pl.BlockSpec((1, unit, ATT_HEAD),
                                    lambda bi, n, h: (bi, jnp.maximum(n * subs - 1, 0), col + h))
    in_specs = [cur(q_col), cur(k_col), cur(v_col)]
    if has_prev:
        in_specs += [prev(k_col), prev(v_col)]
    out_sd = jax.ShapeDtypeStruct((b, t, ATT_OUT), F32)
    out_spec = pl.BlockSpec((1, span, ATT_HEAD), lambda bi, n, h: (bi, n, h))
    o, l = pl.pallas_call(
        functools.partial(_attn_kernel, blk=blk, dil=dil, subs=subs, has_prev=has_prev),
        grid=(b, nb, ATT_HPG),
        in_specs=in_specs,
        out_specs=[out_spec, out_spec],
        out_shape=[out_sd, out_sd],
        compiler_params=_cparams(("parallel", "parallel", "arbitrary")),
        name=f"attn_prompt_g{g}",
    )(*([p3] * len(in_specs)))
    return o.reshape(b * t, ATT_OUT), l.reshape(b * t, ATT_OUT)


def _kv_rows_kernel(x_ref, o_ref):
    for s in range(2):
        for h in range(ATT_HPG):
            c0 = (s * ATT_HPG + h) * ATT_HEAD
            o_ref[0, :, s, h, :] = x_ref[0, :, c0:c0 + ATT_HEAD]


def _kv_rows(p3, g, keep, tr):
    b, t, _ = p3.shape
    col = (T_KV * PROJ_TN) // (2 * ATT_OUT) + g
    r0 = (t - keep) // tr
    return pl.pallas_call(
        _kv_rows_kernel,
        grid=(b, keep // tr),
        in_specs=[pl.BlockSpec((1, tr, 2 * ATT_OUT), lambda bi, i: (bi, r0 + i, col))],
        out_specs=pl.BlockSpec((1, tr, 2, ATT_HPG, ATT_HEAD), lambda bi, i: (bi, i, 0, 0, 0)),
        out_shape=jax.ShapeDtypeStruct((b, keep, 2, ATT_HPG, ATT_HEAD), F32),
        compiler_params=_cparams(("parallel", "parallel")),
        name=f"kv_rows_g{g}",
    )(p3)


def _attn_sample_kernel(q_ref, n0_ref, n1_ref, n2_ref, c0_ref, c1_ref, c2_ref, o_ref, l_ref, *, dils, ts):
    n_refs = (n0_ref, n1_ref, n2_ref)
    c_refs = (c0_ref, c1_ref, c2_ref)
    scale = ATT_HEAD ** -0.5
    for g in range(N_GROUPS):
        dil = dils[g]
        new = n_refs[g][0]
        kn, vn = new[:, 0], new[:, 1]
        rows = c_refs[g].shape[2] // dil
        wi = lax.broadcasted_iota(jnp.int32, (rows, ATT_HPG, 1), 0)
        ni = lax.broadcasted_iota(jnp.int32, (ts, ATT_HPG, 1), 0)
        for t in range(ts):
            cls = pl.ds(t % dil, rows, stride=dil) if dil > 1 else pl.ds(0, rows)
            cache = c_refs[g][0, 0, cls]
            kc, vc = cache[:, 0], cache[:, 1]
            q = q_ref[0, t, g]
            sc = jnp.sum(kc * q[None], axis=-1, keepdims=True) * scale
            sn = jnp.sum(kn * q[None], axis=-1, keepdims=True) * scale
            c_ok = wi >= (t // dil)
            n_ok = functools.reduce(jnp.logical_or, [ni == tn for tn in range(t + 1) if (t - tn) % dil == 0])
            sc = jnp.where(c_ok, sc, -jnp.inf)
            sn = jnp.where(n_ok, sn, -jnp.inf)
            m = jnp.maximum(jnp.max(sc, axis=0, keepdims=True), jnp.max(sn, axis=0, keepdims=True))
            pc = jnp.exp(sc - m)
            pn = jnp.exp(sn - m)
            den = jnp.sum(pc, axis=0, keepdims=True) + jnp.sum(pn, axis=0, keepdims=True)
            o = (jnp.sum(pc * vc, axis=0, keepdims=True) + jnp.sum(pn * vn, axis=0, keepdims=True)) / den
            o_ref[0, t, g] = o[0]
            l_ref[0, t, g] = jnp.broadcast_to((m + jnp.log(den))[0], (ATT_HPG, ATT_HEAD))


def _attn_sample(q5, new_kv, caches):
    b, ts = q5.shape[:2]
    dils = tuple(d for _, d in DILATION_GROUPS)
    in_specs = [pl.BlockSpec((1, ts, N_GROUPS, ATT_HPG, ATT_HEAD), lambda bi: (bi, 0, 0, 0, 0))]
    for g in range(N_GROUPS):
        in_specs.append(pl.BlockSpec((1, ts, 2, ATT_HPG, ATT_HEAD), lambda bi: (bi, 0, 0, 0, 0)))
    for g in range(N_GROUPS):
        w = caches[g].shape[2]
        in_specs.append(pl.BlockSpec((1, 1, w, 2, ATT_HPG, ATT_HEAD), lambda bi: (0, bi, 0, 0, 0, 0)))
    out_sd = jax.ShapeDtypeStruct((b, ts, N_GROUPS, ATT_HPG, ATT_HEAD), F32)
    out_spec = pl.BlockSpec((1, ts, N_GROUPS, ATT_HPG, ATT_HEAD), lambda bi: (bi, 0, 0, 0, 0))
    o, l = pl.pallas_call(
        functools.partial(_attn_sample_kernel, dils=dils, ts=ts),
        grid=(b,),
        in_specs=in_specs,
        out_specs=[out_spec, out_spec],
        out_shape=[out_sd, out_sd],
        compiler_params=_cparams(("parallel",)),
        name="attn_sample",
    )(q5, *new_kv, *caches)
    o_list = [o[:, :, g].reshape(b * ts, ATT_OUT) for g in range(N_GROUPS)]
    l_list = [l[:, :, g].reshape(b * ts, ATT_OUT) for g in range(N_GROUPS)]
    return o_list, l_list


ROLL_ROWS = 64
ROLL_BLOCK_BYTES = 8 << 20


def _roll_kernel(c_ref, n_ref, o_ref):
    w, ts = c_ref.shape[2], n_ref.shape[2]
    body_rows = w - ROLL_ROWS

    def move(i, carry):
        r = pl.multiple_of(i * ROLL_ROWS, ROLL_ROWS)
        o_ref[0, :, pl.ds(r, ROLL_ROWS)] = c_ref[0, :, pl.ds(r + ts, ROLL_ROWS)]
        return carry

    lax.fori_loop(0, body_rows // ROLL_ROWS, move, 0)
    o_ref[0, :, body_rows:w - ts] = c_ref[0, :, body_rows + ts:w]
    o_ref[0, :, w - ts:w] = n_ref[0]


def _cache_roll(cache, new_rows):
    _, b, w = cache.shape[:3]
    ts = new_rows.shape[2]
    assert w % ROLL_ROWS == 0 and ts < ROLL_ROWS
    row_bytes = 2 * ATT_OUT * 4
    bb = max(1, min(b, ROLL_BLOCK_BYTES // (w * row_bytes)))
    assert b % bb == 0
    tail = (2, ATT_HPG, ATT_HEAD)
    return pl.pallas_call(
        _roll_kernel,
        grid=(b // bb,),
        in_specs=[pl.BlockSpec((1, bb, w) + tail, lambda i: (0, i, 0, 0, 0, 0)),
                  pl.BlockSpec((1, bb, ts) + tail, lambda i: (0, i, 0, 0, 0, 0))],
        out_specs=pl.BlockSpec((1, bb, w) + tail, lambda i: (0, i, 0, 0, 0, 0)),
        out_shape=jax.ShapeDtypeStruct(cache.shape, cache.dtype),
        compiler_params=_cparams(("parallel",)),
        name="cache_roll",
    )(cache, new_rows)


def _merge_kernel(x_ref, yrw_ref, gate_ref, o0_ref, o1_ref, o2_ref, l0_ref, l1_ref, l2_ref,
                  wba_ref, wbb_ref, wo_ref, fnw_ref, rw_ref, rb_ref,
                  h_ref, hn_ref, ti_ref, tg_ref, cnt_ref, run_ref):
    @pl.when(pl.program_id(0) == 0)
    def _():
        run_ref[...] = jnp.zeros(run_ref.shape, F32)

    l0, l1, l2 = l0_ref[...], l1_ref[...], l2_ref[...]
    m = jnp.maximum(jnp.maximum(l0, l1), l2)
    e0, e1, e2 = jnp.exp(l0 - m), jnp.exp(l1 - m), jnp.exp(l2 - m)
    o_att = (e0 * o0_ref[...] + e1 * o1_ref[...] + e2 * o2_ref[...]) / (e0 + e1 + e2)
    br_a = jnp.dot(yrw_ref[...], wba_ref[...], preferred_element_type=F32)
    br_b = jnp.dot(o_att.astype(BF16), wbb_ref[...], preferred_element_type=F32)
    merged = gate_ref[:, 0:D_MODEL] * br_a + gate_ref[:, D_MODEL:2 * D_MODEL] * br_b
    h = x_ref[...] + jnp.dot(merged.astype(BF16), wo_ref[...], preferred_element_type=F32)
    h_ref[...] = h
    hn = h * lax.rsqrt(jnp.mean(h * h, axis=-1, keepdims=True) + NORM_EPS) * fnw_ref[...]
    hn_ref[...] = hn
    logits = jnp.dot(hn.astype(BF16), rw_ref[...], preferred_element_type=F32) + rb_ref[...]
    lane_i = lax.broadcasted_iota(jnp.int32, logits.shape, 1)
    lane = lane_i.astype(F32)
    vals, idxs = [], []
    cur = logits
    for _ in range(TOP_K):
        mx = jnp.max(cur, axis=-1, keepdims=True)
        ix = jnp.min(jnp.where(cur == mx, lane, float(LANES)), axis=-1, keepdims=True)
        vals.append(mx)
        idxs.append(ix)
        cur = jnp.where(lane == ix, -jnp.inf, cur)
    es = [jnp.exp(vv - vals[0]) for vv in vals]
    tot = es[0] + es[1] + es[2] + es[3]
    tm = logits.shape[0]
    ri = lax.broadcasted_iota(jnp.int32, (tm, tm), 0)
    rj = lax.broadcasted_iota(jnp.int32, (tm, tm), 1)
    before = jnp.where(ri > rj, 1.0, 0.0).astype(BF16)
    run = run_ref[0:1, :]
    ranks = []
    for kq in range(TOP_K):
        onehot = jnp.where(lane == idxs[kq], 1.0, 0.0)
        prior = jnp.dot(before, onehot.astype(BF16), preferred_element_type=F32) + run
        ranks.append(jnp.sum(onehot * prior, axis=-1, keepdims=True))
        run = run + jnp.sum(onehot, axis=0, keepdims=True)
    run_ref[...] = jnp.broadcast_to(run, run_ref.shape)
    cnt_ref[...] = jnp.broadcast_to(run, cnt_ref.shape)

    ti = jnp.zeros(logits.shape, F32)
    tg = jnp.zeros(logits.shape, F32)
    for kq in range(TOP_K):
        ti = jnp.where(lane_i == kq, idxs[kq], ti)
        ti = jnp.where(lane_i == TOP_K + kq, ranks[kq], ti)
        tg = jnp.where(lane_i == kq, es[kq] / tot, tg)
    ti_ref[...] = ti.astype(jnp.int32)
    tg_ref[...] = tg


def _merge(x2d, yrw, p2d, o_list, l_list, wba, wbb, wo, fnw, rw_pad, rb_pad, *, tm):
    n = x2d.shape[0]
    row = lambda wdt: pl.BlockSpec((tm, wdt), lambda i: (i, 0))
    full = lambda a: pl.BlockSpec(a.shape, lambda i: (0,) * a.ndim)
    gate_blk = (T_GATE * PROJ_TN) // (2 * D_MODEL)
    return pl.pallas_call(
        _merge_kernel,
        grid=(n // tm,),
        in_specs=[row(D_MODEL), row(D_MODEL),
                  pl.BlockSpec((tm, 2 * D_MODEL), lambda i: (i, gate_blk)),
                  row(ATT_OUT), row(ATT_OUT), row(ATT_OUT), row(ATT_OUT), row(ATT_OUT), row(ATT_OUT),
                  full(wba), full(wbb), full(wo), full(fnw), full(rw_pad), full(rb_pad)],
        out_specs=[row(D_MODEL), row(D_MODEL), row(LANES), row(LANES),
                   pl.BlockSpec((SUBLANES, LANES), lambda i: (0, 0))],
        out_shape=[jax.ShapeDtypeStruct((n, D_MODEL), F32), jax.ShapeDtypeStruct((n, D_MODEL), F32),
                   jax.ShapeDtypeStruct((n, LANES), jnp.int32), jax.ShapeDtypeStruct((n, LANES), F32),
                   jax.ShapeDtypeStruct((SUBLANES, LANES), F32)],
        scratch_shapes=[pltpu.VMEM((SUBLANES, LANES), F32)],
        compiler_params=_cparams(("arbitrary",)),
        name="merge",
    )(x2d, yrw, p2d, *o_list, *l_list, wba, wbb, wo, fnw, rw_pad, rb_pad)


def _row_copy(src_ref, s_row, dst_ref, d_row, sem):
    return pltpu.make_async_copy(src_ref.at[pl.ds(s_row, 1), :], dst_ref.at[pl.ds(d_row, 1), :], sem)


def _dispatch_kernel(pad_start_ref, pad_len_ref, pos_ref, hn_ref, xs_ref, zrow_ref, sem, *, tt):
    i = pl.program_id(0)

    @pl.when(i == 0)
    def _():
        zrow_ref[...] = jnp.zeros(zrow_ref.shape, F32)

        def per_expert(e, carry):
            s = pad_start_ref[e]
            cnt = pad_len_ref[e]

            def issue(rr, cc):
                _row_copy(zrow_ref, 0, xs_ref, s + rr, sem).start()
                return cc

            def drain(rr, cc):
                _row_copy(zrow_ref, 0, xs_ref, s + rr, sem).wait()
                return cc

            lax.fori_loop(0, cnt, issue, 0)
            lax.fori_loop(0, cnt, drain, 0)
            return carry

        lax.fori_loop(0, N_EXPERTS, per_expert, 0)

    def issue(nn, cc):
        for kq in range(TOP_K):
            _row_copy(hn_ref, nn, xs_ref, pos_ref[0, 0, nn * TOP_K + kq], sem).start(priority=kq % 2)
        return cc

    def drain(nn, cc):
        for kq in range(TOP_K):
            _row_copy(hn_ref, nn, xs_ref, pos_ref[0, 0, nn * TOP_K + kq], sem).wait()
        return cc

    lax.fori_loop(0, tt, issue, 0)
    lax.fori_loop(0, tt, drain, 0)


def _dispatch(hn, pos3, pad_start, pad_len, rows, *, tt):
    n = hn.shape[0]
    return pl.pallas_call(
        functools.partial(_dispatch_kernel, tt=tt),
        grid_spec=pltpu.PrefetchScalarGridSpec(
            num_scalar_prefetch=2,
            grid=(n // tt,),
            in_specs=[
                pl.BlockSpec((1, 1, tt * TOP_K), lambda i, ps, plen: (i, 0, 0), memory_space=pltpu.SMEM),
                pl.BlockSpec((tt, D_MODEL), lambda i, ps, plen: (i, 0)),
            ],
            out_specs=pl.BlockSpec(memory_space=pl.ANY),
            scratch_shapes=[pltpu.VMEM((SUBLANES, D_MODEL), F32), pltpu.SemaphoreType.DMA(())],
        ),
        out_shape=jax.ShapeDtypeStruct((rows, D_MODEL), F32),
        compiler_params=_cparams(("arbitrary",)),
        name="moe_dispatch",
    )(pad_start, pad_len, pos3, hn)


def _moe_kernel(be_ref, nu_ref, xs_ref, w1_ref, b1_ref, w2_ref, b2_ref, y_ref):
    i = pl.program_id(0)

    @pl.when(i < nu_ref[0])
    def _():
        x = xs_ref[...].astype(BF16)
        hdn = jnp.dot(x, w1_ref[0], preferred_element_type=F32) + b1_ref[0]
        glu = jnp.minimum(hdn[:, 0:D_MODEL], SWIGLU_LIMIT)
        lin = jnp.clip(hdn[:, D_MODEL:2 * D_MODEL], -SWIGLU_LIMIT, SWIGLU_LIMIT)
        act = glu * jax.nn.sigmoid(SWIGLU_ALPHA * glu) * (lin + 1.0)
        y_ref[...] = jnp.dot(act.astype(BF16), w2_ref[0], preferred_element_type=F32) + b2_ref[0]

    @pl.when(i >= nu_ref[0])
    def _():
        y_ref[...] = jnp.zeros(y_ref.shape, F32)


def _moe(xs, blk_expert, n_used, w1, b1, w2, b2, *, tm):
    rows = xs.shape[0]
    nb = rows // tm
    return pl.pallas_call(
        _moe_kernel,
        grid_spec=pltpu.PrefetchScalarGridSpec(
            num_scalar_prefetch=2,
            grid=(nb,),
            in_specs=[
                pl.BlockSpec((tm, D_MODEL), lambda i, be, nu: (jnp.minimum(i, nu[0] - 1), 0)),
                pl.BlockSpec((1, D_MODEL, 2 * D_MODEL), lambda i, be, nu: (be[i], 0, 0)),
                pl.BlockSpec((1, 1, 2 * D_MODEL), lambda i, be, nu: (be[i], 0, 0)),
                pl.BlockSpec((1, D_MODEL, D_MODEL), lambda i, be, nu: (be[i], 0, 0)),
                pl.BlockSpec((1, 1, D_MODEL), lambda i, be, nu: (be[i], 0, 0)),
            ],
            out_specs=pl.BlockSpec((tm, D_MODEL), lambda i, be, nu: (i, 0)),
        ),
        out_shape=jax.ShapeDtypeStruct((rows, D_MODEL), F32),
        compiler_params=_cparams(("arbitrary",)),
        name="moe_experts",
    )(blk_expert, n_used, xs, w1, b1, w2, b2)


def _combine_kernel(pos_ref, h_ref, tg_ref, yb_ref, o_ref, buf_ref, sem, *, tt):
    def issue(nn, cc):
        for kq in range(TOP_K):
            _row_copy(yb_ref, pos_ref[0, 0, nn * TOP_K + kq], buf_ref.at[kq], nn, sem).start(priority=kq % 2)
        return cc

    def drain(nn, cc):
        for kq in range(TOP_K):
            _row_copy(yb_ref, pos_ref[0, 0, nn * TOP_K + kq], buf_ref.at[kq], nn, sem).wait()
        return cc

    lax.fori_loop(0, tt, issue, 0)
    lax.fori_loop(0, tt, drain, 0)
    acc = h_ref[...]
    for kq in range(TOP_K):
        acc = acc + tg_ref[:, kq:kq + 1] * buf_ref[kq]
    o_ref[...] = acc


def _combine(h, tg, pos3, yb, *, tt):
    n = h.shape[0]
    return pl.pallas_call(
        functools.partial(_combine_kernel, tt=tt),
        grid=(n // tt,),
        in_specs=[
            pl.BlockSpec((1, 1, tt * TOP_K), lambda i: (i, 0, 0), memory_space=pltpu.SMEM),
            pl.BlockSpec((tt, D_MODEL), lambda i: (i, 0)),
            pl.BlockSpec((tt, LANES), lambda i: (i, 0)),
            pl.BlockSpec(memory_space=pl.ANY),
        ],
        out_specs=pl.BlockSpec((tt, D_MODEL), lambda i: (i, 0)),
        out_shape=jax.ShapeDtypeStruct((n, D_MODEL), F32),
        scratch_shapes=[pltpu.VMEM((TOP_K, tt, D_MODEL), F32), pltpu.SemaphoreType.DMA(())],
        compiler_params=_cparams(("arbitrary",)),
        name="moe_combine",
    )(pos3, h, tg, yb)


def _routing(top_idx, rank, counts, tm):
    n = top_idx.shape[0]
    nk = n * TOP_K
    experts = jnp.arange(N_EXPERTS, dtype=jnp.int32)
    blocks_e = (counts + tm - 1) // tm
    upper = (experts[:, None] <= experts[None, :]).astype(jnp.int32)
    block_end = jnp.sum(blocks_e[:, None] * upper, axis=0)
    row_start = (block_end - blocks_e) * tm
    onehot = (top_idx[:, :, None] == experts[None, None, :]).astype(jnp.int32)
    pos = (jnp.sum(onehot * row_start[None, None, :], axis=-1) + rank).reshape(nk).astype(jnp.int32)
    n_blocks = -(-nk // tm) + N_EXPERTS
    blk_ids = jnp.arange(n_blocks, dtype=jnp.int32)
    blk_expert = jnp.minimum(
        jnp.sum((block_end[None, :] <= blk_ids[:, None]).astype(jnp.int32), axis=1), N_EXPERTS - 1
    ).astype(jnp.int32)
    n_used = block_end[-1:].astype(jnp.int32)
    pad_start = (row_start + counts).astype(jnp.int32)
    pad_len = (blocks_e * tm - counts).astype(jnp.int32)
    return pos, blk_expert, n_used, pad_start, pad_len, n_blocks * tm


def _pad_cols(a, width):
    return jnp.pad(a, ((0, 0), (0, width - a.shape[1])))


def _pad_rows(a, height):
    return jnp.pad(a, ((0, height - a.shape[0]), (0, 0)))


def _proj_columns(a):
    d = D_MODEL
    xw = a[:, 3 * d:3 * d + DECAY_LORA]
    xa = a[:, 3 * d + DECAY_LORA:3 * d + DECAY_LORA + AAA_LORA]
    xg = a[:, 3 * d + DECAY_LORA + AAA_LORA:C_RW]
    att = a[:, C_RW:C_RW + 3 * ATT_DIM]
    q, k, v = att[:, :ATT_DIM], att[:, ATT_DIM:2 * ATT_DIM], att[:, 2 * ATT_DIM:]
    parts = [a[:, :3 * d], _pad_cols(xw, LANES), _pad_cols(xa, LANES), _pad_cols(xg, 2 * LANES), q]
    for g in range(N_GROUPS):
        parts.append(k[:, g * ATT_OUT:(g + 1) * ATT_OUT])
        parts.append(v[:, g * ATT_OUT:(g + 1) * ATT_OUT])
    parts.append(a[:, C_RW + 3 * ATT_DIM:])
    return jnp.concatenate(parts, axis=1)


def _rope_tables(pos):
    inv = ROPE_THETA ** (-jnp.arange(0, ATT_HEAD, 2, dtype=F32) / ATT_HEAD)
    ang = pos.astype(F32)[:, None] * inv[None, :]
    cos, sin = jnp.cos(ang), jnp.sin(ang)
    return jnp.concatenate([cos, cos], axis=1), jnp.concatenate([-sin, sin], axis=1)


def _prep_weights(attn_norm_w, w_in, rw_mu, rw_w0, rw_w2, rw_a0, rw_a2, rw_g2, rw_kk, rw_ka, rw_rk,
                  rw_lnx_w, rw_lnx_b, q_norm_w, k_norm_w, w_br, w_out, ffn_norm_w, router_w, router_b,
                  moe_w1, moe_b1, moe_w2, moe_b2):
    w = {}
    w['nw'] = attn_norm_w.reshape(1, D_MODEL)
    w['w_pad'] = _proj_columns(w_in).astype(BF16)
    w['mu_pad'] = _proj_columns(_pad_cols(rw_mu.reshape(1, C_RW), C_RW + 3 * ATT_DIM + 2 * D_MODEL))[:, :RW_PAD]
    w['par'] = jnp.concatenate([rw_w0.reshape(1, -1), rw_a0.reshape(1, -1), rw_kk.reshape(1, -1),
                                rw_ka.reshape(1, -1), rw_rk.reshape(1, -1), rw_lnx_w.reshape(1, -1),
                                rw_lnx_b.reshape(1, -1), jnp.zeros((1, D_MODEL), F32)], axis=0)
    w['w2p'] = _pad_rows(rw_w2, LANES).astype(BF16)
    w['a2p'] = _pad_rows(rw_a2, LANES).astype(BF16)
    w['g2p'] = _pad_rows(rw_g2, 2 * LANES).astype(BF16)
    w['qn'] = q_norm_w.reshape(1, ATT_HEAD)
    w['kn'] = k_norm_w.reshape(1, ATT_HEAD)
    w['wba'] = w_br[:D_MODEL].astype(BF16)
    w['wbb'] = w_br[D_MODEL:].astype(BF16)
    w['wo'] = w_out.astype(BF16)
    w['fnw'] = ffn_norm_w.reshape(1, D_MODEL)
    w['rw_pad'] = _pad_cols(router_w, LANES).astype(BF16)
    w['rb_pad'] = jnp.concatenate([router_b.reshape(1, N_EXPERTS),
                                   jnp.full((1, LANES - N_EXPERTS), -1e30, F32)], axis=1)
    cols = jnp.arange(2 * D_MODEL, dtype=jnp.int32)
    src = jnp.where(cols < D_MODEL, 2 * cols, 2 * (cols - D_MODEL) + 1)
    perm = (jnp.arange(2 * D_MODEL, dtype=jnp.int32)[:, None] == src[None, :]).astype(BF16)
    w['w1'] = jnp.einsum('edh,hk->edk', moe_w1.astype(BF16), perm, preferred_element_type=BF16)
    w['b1'] = jnp.concatenate([moe_b1[:, 0::2], moe_b1[:, 1::2]], axis=1).reshape(N_EXPERTS, 1, 2 * D_MODEL)
    w['w2'] = moe_w2.astype(BF16)
    w['b2'] = moe_b2.reshape(N_EXPERTS, 1, D_MODEL)
    return w


def _state_to_pairs(s):
    b = s.shape[0]
    sr = s.reshape(b, RWKV_HEADS // 2, 2, RWKV_HEAD, RWKV_HEAD)
    z = jnp.zeros_like(sr[:, :, 0])
    top = jnp.concatenate([sr[:, :, 0], z], axis=-1)
    bot = jnp.concatenate([z, sr[:, :, 1]], axis=-1)
    return jnp.concatenate([top, bot], axis=-2)


def _pairs_to_state(zp):
    b = zp.shape[0]
    zr = zp.reshape(b, RWKV_HEADS // 2, 2, RWKV_HEAD, 2, RWKV_HEAD)
    return jnp.stack([zr[:, :, 0, :, 0, :], zr[:, :, 1, :, 1, :]], axis=2).reshape(b, RWKV_HEADS, RWKV_HEAD, RWKV_HEAD)


def _layer(x, pos, shift_prev, wkv0, caches, w, *, proj_tm, merge_tm, moe_tm, tok_tt):
    b, t, _ = x.shape
    n = b * t
    x2d = x.reshape(n, D_MODEL)
    is_prompt = caches is None
    rw_rows = next(g for g in (RWKV_ROWS_PER_STEP, RWKV_ROWS_TOGETHER, 1) if b % g == 0)

    cs_tab, sn_tab = _rope_tables(pos)
    if not is_prompt:
        cs_tab = jnp.tile(cs_tab, (proj_tm // t, 1))
        sn_tab = jnp.tile(sn_tab, (proj_tm // t, 1))
    p2d = _proj(x2d, w['nw'], w['w_pad'], w['qn'], w['kn'], cs_tab, sn_tab,
                tm=proj_tm, n_tiles=N_TILES, do_norm=True)
    p3 = p2d.reshape(b, t, P_PAD)
    shift_out = _rmsnorm_rows(x[:, t - 1, :], w['nw'])

    if is_prompt:
        prev0 = jnp.zeros((b, 1, RW_PAD), F32)
        s0z = jnp.zeros((b, RWKV_HEADS // 2, LANES, LANES), F32)
        yrw, sfin = _rwkv(p3, prev0, s0z, w['mu_pad'], w['par'], w['w2p'], w['a2p'], w['g2p'],
                          chunk=SCAN_CHUNK, n_valid=None, rows_per_step=rw_rows)
        yrw = yrw.reshape(n, D_MODEL)
    else:
        one = jnp.ones((1, ATT_HEAD), F32)
        zero_tab = jnp.zeros((b, ATT_HEAD), F32)
        prev_rw = _proj(shift_prev, w['nw'], w['w_pad'], one, one, zero_tab, zero_tab,
                        tm=b, n_tiles=T_LORA + 1, do_norm=False)
        t_pad = SCAN_CHUNK
        p_rw = jnp.pad(p3[:, :, :RW_PAD], ((0, 0), (0, t_pad - t), (0, 0)))
        yrw, sfin = _rwkv(p_rw, prev_rw.reshape(b, 1, RW_PAD), _state_to_pairs(wkv0), w['mu_pad'], w['par'],
                          w['w2p'], w['a2p'], w['g2p'], chunk=SCAN_CHUNK, n_valid=t, rows_per_step=rw_rows)
        yrw = yrw[:, :t].reshape(n, D_MODEL)
    s_fin = _pairs_to_state(sfin)

    new_kv = []
    if is_prompt:
        o_list, l_list = [], []
        for g, (window, dil) in enumerate(DILATION_GROUPS):
            o_g, l_g = _attn_prompt(p3, g, dil, window // dil)
            o_list.append(o_g)
            l_list.append(l_g)
            keep = min(window, t)
            new_kv.append(_kv_rows(p3, g, keep, min(keep, 256))[None])
    else:
        kv_col = T_KV * PROJ_TN
        q5 = p3[:, :, T_Q * PROJ_TN:T_KV * PROJ_TN].reshape(b, t, N_GROUPS, ATT_HPG, ATT_HEAD)
        kv_new = [p3[:, :, kv_col + g * 2 * ATT_OUT:kv_col + (g + 1) * 2 * ATT_OUT].reshape(b, t, 2, ATT_HPG, ATT_HEAD)
                  for g in range(N_GROUPS)]
        o_list, l_list = _attn_sample(q5, kv_new, caches)
        new_kv = [_cache_roll(caches[g], kv_new[g][None]) for g in range(N_GROUPS)]

    h, hn, ti, tg, cnt = _merge(x2d, yrw, p2d, o_list, l_list, w['wba'], w['wbb'], w['wo'], w['fnw'],
                                w['rw_pad'], w['rb_pad'], tm=merge_tm)

    counts = cnt[0, :N_EXPERTS].astype(jnp.int32)
    pos_rows, blk_expert, n_used, pad_start, pad_len, rows = _routing(
        ti[:, :TOP_K], ti[:, TOP_K:2 * TOP_K], counts, moe_tm)
    pos3 = pos_rows.reshape(n // tok_tt, 1, tok_tt * TOP_K)
    xs = _dispatch(hn, pos3, pad_start, pad_len, rows, tt=tok_tt)
    yb = _moe(xs, blk_expert, n_used, w['w1'], w['b1'], w['w2'], w['b2'], tm=moe_tm)
    y = _combine(h, tg, pos3, yb, tt=tok_tt)
    return y.reshape(b, t, D_MODEL), shift_out, s_fin, new_kv


def kernel(x_prompt, x_sample, state_wkv, state_shift, cache_kv_w128, cache_kv_w512, cache_kv_w2048,
           attn_norm_w, w_in, rw_mu, rw_w0, rw_w2, rw_a0, rw_a2, rw_g2, rw_kk, rw_ka, rw_rk,
           rw_lnx_w, rw_lnx_b, q_norm_w, k_norm_w, w_br, w_out, ffn_norm_w, router_w, router_b,
           moe_w1, moe_b1, moe_w2, moe_b2):
    depth = w_in.shape[0]
    assert depth == 1
    bp, tp, _ = x_prompt.shape
    bs, ts, _ = x_sample.shape
    l = 0
    w = _prep_weights(attn_norm_w[l], w_in[l], rw_mu[l], rw_w0[l], rw_w2[l], rw_a0[l], rw_a2[l], rw_g2[l],
                      rw_kk[l], rw_ka[l], rw_rk[l], rw_lnx_w[l], rw_lnx_b[l], q_norm_w[l], k_norm_w[l],
                      w_br[l], w_out[l], ffn_norm_w[l], router_w[l], router_b[l],
                      moe_w1[l], moe_b1[l], moe_w2[l], moe_b2[l])
    pos_p = jnp.arange(tp, dtype=jnp.int32)
    pos_s = PAST_LEN + jnp.arange(ts, dtype=jnp.int32)
    yp, sh_p, wkv_p, kv_p = _layer(x_prompt, pos_p, None, None, None, w,
                                   proj_tm=min(1024, tp), merge_tm=256, moe_tm=MOE_TM, tok_tt=256)
    ns = bs * ts
    ys, sh_s, wkv_s, kv_s = _layer(x_sample, pos_s, state_shift[l], state_wkv[l],
                                   (cache_kv_w128, cache_kv_w512, cache_kv_w2048), w,
                                   proj_tm=ns, merge_tm=min(256, ns), moe_tm=128, tok_tt=min(256, ns))
    return (yp, ys,
            wkv_p[None], sh_p[None], kv_p[0], kv_p[1], kv_p[2],
            wkv_s[None], sh_s[None], kv_s[0], kv_s[1], kv_s[2])
```

```python
import functools

import jax
import jax.numpy as jnp
from jax import lax
from jax.experimental import pallas as pl
from jax.experimental.pallas import tpu as pltpu

F32 = jnp.float32
BF16 = jnp.bfloat16

D_MODEL = 1024
NORM_EPS = 1e-5
RWKV_HEAD = 64
RWKV_HEADS = D_MODEL // RWKV_HEAD
DECAY_LORA = 64
AAA_LORA = 64
GATE_LORA = 160
GN_EPS = 64e-5
C_RW = 3 * D_MODEL + DECAY_LORA + AAA_LORA + GATE_LORA
ATT_HEAD = 128
ATT_HPG = 4
DILATION_GROUPS = ((128, 1), (512, 4), (2048, 16))
N_GROUPS = 3
ATT_DIM = N_GROUPS * ATT_HPG * ATT_HEAD
ATT_OUT = ATT_HPG * ATT_HEAD
ROPE_THETA = 10000.0
N_EXPERTS = 32
TOP_K = 4
SWIGLU_ALPHA = 1.702
SWIGLU_LIMIT = 7.0
PAST_LEN = 8192

LANES = 128
SUBLANES = 8
VMEM_LIMIT = 56 * 1024 * 1024

PROJ_TN = 512
T_RKV = 0
T_LORA = 6
T_Q = 7
T_KV = 10
T_GATE = 16
N_TILES = 20
P_PAD = N_TILES * PROJ_TN
RW_PAD = (T_LORA + 1) * PROJ_TN

SCAN_CHUNK = 64
RWKV_ROWS_PER_STEP = 4
RWKV_ROWS_TOGETHER = 2
MOE_TM = 512


def _cparams(sem, vmem=VMEM_LIMIT):
    return pltpu.CompilerParams(dimension_semantics=sem, vmem_limit_bytes=vmem)


def _proj_kernel(x_ref, nw_ref, w_ref, qn_ref, kn_ref, cs_ref, sn_ref, o_ref, xn_ref, *, do_norm):
    j = pl.program_id(1)

    @pl.when(j == 0)
    def _():
        x = x_ref[...]
        if do_norm:
            x = x * lax.rsqrt(jnp.mean(x * x, axis=-1, keepdims=True) + NORM_EPS) * nw_ref[...]
        xn_ref[...] = x.astype(BF16)

    acc = jnp.dot(xn_ref[...], w_ref[...], preferred_element_type=F32)
    is_q = (j >= T_Q) & (j < T_KV)
    is_k = (j >= T_KV) & (j < T_GATE) & ((j - T_KV) % 2 == 0)
    is_qk = is_q | is_k
    is_gate = j >= T_GATE

    @pl.when(is_qk)
    def _():
        nwh = jnp.where(j < T_KV, qn_ref[...], kn_ref[...])
        cs = cs_ref[...]
        sn = sn_ref[...]
        for h in range(ATT_HPG):
            xh = acc[:, h * ATT_HEAD:(h + 1) * ATT_HEAD]
            y = xh * lax.rsqrt(jnp.mean(xh * xh, axis=-1, keepdims=True) + NORM_EPS) * nwh
            o_ref[:, h * ATT_HEAD:(h + 1) * ATT_HEAD] = y * cs + pltpu.roll(y, ATT_HEAD // 2, axis=1) * sn

    @pl.when(is_gate)
    def _():
        o_ref[...] = jax.nn.sigmoid(acc)

    @pl.when(jnp.logical_not(is_qk | is_gate))
    def _():
        o_ref[...] = acc


def _proj(x2d, nw, w_pad, qn, kn, cs_tab, sn_tab, *, tm, n_tiles, do_norm):
    n = x2d.shape[0]
    tab_blocks = cs_tab.shape[0] // tm
    return pl.pallas_call(
        functools.partial(_proj_kernel, do_norm=do_norm),
        grid=(n // tm, n_tiles),
        in_specs=[
            pl.BlockSpec((tm, D_MODEL), lambda i, j: (i, 0)),
            pl.BlockSpec((1, D_MODEL), lambda i, j: (0, 0)),
            pl.BlockSpec((D_MODEL, PROJ_TN), lambda i, j: (0, j)),
            pl.BlockSpec((1, ATT_HEAD), lambda i, j: (0, 0)),
            pl.BlockSpec((1, ATT_HEAD), lambda i, j: (0, 0)),
            pl.BlockSpec((tm, ATT_HEAD), lambda i, j: (i % tab_blocks, 0)),
            pl.BlockSpec((tm, ATT_HEAD), lambda i, j: (i % tab_blocks, 0)),
        ],
        out_specs=pl.BlockSpec((tm, PROJ_TN), lambda i, j: (i, j)),
        out_shape=jax.ShapeDtypeStruct((n, n_tiles * PROJ_TN), F32),
        scratch_shapes=[pltpu.VMEM((tm, D_MODEL), BF16)],
        compiler_params=_cparams(("parallel", "arbitrary")),
        name="proj",
    )(x2d, nw, w_pad, qn, kn, cs_tab, sn_tab)


def _rmsnorm_kernel(x_ref, w_ref, o_ref):
    x = x_ref[...]
    o_ref[...] = x * lax.rsqrt(jnp.mean(x * x, axis=-1, keepdims=True) + NORM_EPS) * w_ref[...]


def _rmsnorm_rows(x2d, w):
    return pl.pallas_call(
        _rmsnorm_kernel,
        out_shape=jax.ShapeDtypeStruct(x2d.shape, F32),
        name="rmsnorm_rows",
    )(x2d, w)


def _split3(x):
    h1 = x.astype(BF16)
    r1 = x - h1.astype(F32)
    h2 = r1.astype(BF16)
    h3 = (r1 - h2.astype(F32)).astype(BF16)
    return h1, h2, h3


def _seg_sum(x, seg_ones):
    rows = x.shape[0]
    hi = x.astype(BF16)
    lo = (x - hi.astype(F32)).astype(BF16)
    hl = jnp.concatenate([hi, lo], axis=0)
    outs = []
    for p in range(D_MODEL // LANES):
        s = jnp.dot(hl[:, p * LANES:(p + 1) * LANES], seg_ones, preferred_element_type=F32)
        outs.append(s[:rows] + s[rows:])
    return jnp.concatenate(outs, axis=1)


def _nt_dot(a, b):
    return lax.dot_general(a, b, (((1,), (1,)), ((), ())), preferred_element_type=F32)


def _rwkv_kernel(p_ref, prev0_ref, s0_ref, mu_ref, par_ref, w2_ref, a2_ref, g2_ref,
                 y_ref, st_ref, prev_ref, *, chunk, n_valid, together):
    c = pl.program_id(1)
    C = chunk
    rows_per_step = p_ref.shape[0]

    @pl.when(c == 0)
    def _():
        st_ref[...] = s0_ref[...]
        prev_ref[...] = prev0_ref[...]

    w0 = par_ref[0:1, :]
    a0 = par_ref[1:2, :]
    k_k = par_ref[2:3, :]
    k_a = par_ref[3:4, :]
    r_k = par_ref[4:5, :]
    lnx_w = par_ref[5:6, :]
    lnx_b = par_ref[6:7, :]

    row = lax.broadcasted_iota(jnp.int32, (C, 1), 0)
    li = lax.broadcasted_iota(jnp.int32, (LANES, LANES), 0)
    lj = lax.broadcasted_iota(jnp.int32, (LANES, LANES), 1)
    same_head = (li < RWKV_HEAD) == (lj < RWKV_HEAD)
    seg_ones = jnp.where(same_head, 1.0, 0.0).astype(BF16)
    blk_mask = jnp.where(same_head, 1.0, 0.0)
    ti = lax.broadcasted_iota(jnp.int32, (C, C), 0)
    tj = lax.broadcasted_iota(jnp.int32, (C, C), 1)
    tril_b = jnp.where(ti >= tj, 1.0, 0.0).astype(BF16)
    ti2 = lax.broadcasted_iota(jnp.int32, (C, 2 * C), 0)
    tj2 = lax.broadcasted_iota(jnp.int32, (C, 2 * C), 1) & (C - 1)
    tril_incl2 = jnp.where(ti2 >= tj2, 1.0, 0.0)
    tril_strict2 = jnp.where(ti2 > tj2, 1.0, 0.0)
    lane = lax.broadcasted_iota(jnp.int32, (1, LANES), 1)
    first = lane < RWKV_HEAD
    n_lev = max(1, (C - 1).bit_length())
    pairs = range(D_MODEL // LANES)
    sls = [slice(pr * LANES, (pr + 1) * LANES) for pr in pairs]
    dot = functools.partial(jnp.dot, preferred_element_type=F32)
    head_sel = lambda x0, x1: jnp.where(first, x0, x1)

    def prepare(gi, out):
        p = p_ref[gi]
        shifted = jnp.where(row == 0, prev_ref[gi], pltpu.roll(p, 1, axis=0))
        prev_ref[gi] = p[C - 1:C, :]
        ps = p + (shifted - p) * mu_ref[...]
        r = ps[:, 0:D_MODEL]
        k = ps[:, D_MODEL:2 * D_MODEL]
        v = ps[:, 2 * D_MODEL:3 * D_MODEL]
        xw = ps[:, 3 * D_MODEL:3 * D_MODEL + LANES]
        xa = ps[:, 3 * D_MODEL + LANES:3 * D_MODEL + 2 * LANES]
        xg = ps[:, 3 * D_MODEL + 2 * LANES:RW_PAD]
        yield
        zw = w0 + dot(jnp.tanh(xw).astype(BF16), w2_ref[...])
        nz = -zw
        softplus = jnp.log(1.0 + jnp.exp(-jnp.abs(nz))) + jnp.maximum(nz, 0.0)
        lw = -jnp.exp(-softplus - 0.5)
        a = jax.nn.sigmoid(a0 + dot(xa.astype(BF16), a2_ref[...]))
        g = dot(jax.nn.sigmoid(xg).astype(BF16), g2_ref[...])
        yield
        kkr = k * k_k
        kk = kkr / jnp.maximum(jnp.sqrt(_seg_sum(kkr * kkr, seg_ones)), 1e-12)
        yield
        k2 = k * (1.0 + (a - 1.0) * k_a)
        bonus = _seg_sum(r * k2 * r_k, seg_ones) * v
        yield
        if n_valid is not None:
            valid = (c * C + row) < n_valid
            lw = jnp.where(valid, lw, 0.0)
            kk = jnp.where(valid, kk, 0.0)
            k2 = jnp.where(valid, k2, 0.0)
        l1, l2, l3 = _split3(lw)
        cl = dot(tril_b, l1) + dot(tril_b, l2) + dot(tril_b, l3)
        cl_end = cl[C - 1:C, :]
        yield
        kb = kk * a
        e_inv = jnp.exp(-cl)
        e_tail = jnp.exp(cl_end - cl)
        out.update(v=v, g=g, bonus=bonus, a_t=(-kk) * jnp.exp(cl - lw), r_t=r * jnp.exp(cl))
        yield
        out.update(b_t=kb * e_inv, k_t=k2 * e_inv, b_h=kb * e_tail, k_h=k2 * e_tail, w_end=jnp.exp(cl_end))

    def finish(gi, q, y):
        inv_n = 1.0 / RWKV_HEAD
        mean = _seg_sum(y, seg_ones) * inv_n
        yc = y - mean
        yield
        var = _seg_sum(yc * yc, seg_ones) * inv_n
        yield
        yn = yc * lax.rsqrt(var + GN_EPS) * lnx_w + lnx_b
        y_ref[gi] = ((yn + q['bonus']) * q['g']).astype(y_ref.dtype)

    def scan_chunk(gis, qs, fillers):
        def fill():
            for f in fillers:
                next(f, None)

        units = [(n, pr) for n in range(len(gis)) for pr in pairs]
        by_head = lambda x: jnp.concatenate([jnp.where(first, x, 0.0), jnp.where(first, 0.0, x)],
                                            axis=0).astype(BF16)
        zs = {(n, pr): st_ref[gis[n], pr] for n, pr in units}
        ars = {(n, pr): jnp.concatenate([qs[n]['a_t'][:, sls[pr]], qs[n]['r_t'][:, sls[pr]]], axis=0)
               for n, pr in units}
        q0s = {u: _nt_dot(ars[u].astype(BF16), zs[u].astype(BF16)) for u in units}
        fill()
        xx, lk, mb, mk, vsw = {}, {}, {}, {}, {}
        for u in units:
            n, pr = u
            btb = qs[n]['b_t'][:, sls[pr]].astype(BF16)
            ktb = qs[n]['k_t'][:, sls[pr]].astype(BF16)
            g0 = _nt_dot(jnp.where(first, ars[u], 0.0).astype(BF16), jnp.concatenate([btb, ktb], axis=0))
            g1 = _nt_dot(jnp.where(first, 0.0, ars[u]).astype(BF16), jnp.concatenate([ktb, btb], axis=0))
            xx[u] = head_sel(g0[:C], g1[:C]) * tril_strict2
            lk[u] = (head_sel(g1[:C], g0[:C]) * tril_strict2).astype(BF16)
            mb[u] = (head_sel(g0[C:], g1[C:]) * tril_incl2).astype(BF16)
            mk[u] = (head_sel(g1[C:], g0[C:]) * tril_incl2).astype(BF16)
            v_p = qs[n]['v'][:, sls[pr]]
            vsw[u] = jnp.concatenate([jnp.where(first, 0.0, v_p), jnp.where(first, v_p, 0.0)],
                                     axis=0).astype(BF16)
        fill()
        us = {u: q0s[u][:C] + dot(lk[u], vsw[u]) for u in units}
        for lev in range(n_lev):
            us = {u: us[u] + dot(xx[u].astype(BF16), by_head(us[u])) for u in units}
            if lev < n_lev - 1:
                xx = {u: dot(xx[u].astype(BF16), by_head(xx[u])) for u in units}
            fill()
        ys = {u: q0s[u][C:] + dot(jnp.concatenate([mb[u], mk[u]], axis=1),
                                  jnp.concatenate([by_head(us[u]), vsw[u]], axis=0)) for u in units}
        for u in units:
            n, pr = u
            sl = sls[pr]
            uv_t = jnp.concatenate([us[u], qs[n]['v'][:, sl]], axis=0).T.astype(BF16)
            bk_h = jnp.concatenate([qs[n]['b_h'][:, sl], qs[n]['k_h'][:, sl]], axis=0).astype(BF16)
            st_ref[gis[n], pr] = zs[u] * qs[n]['w_end'][:, sl] + blk_mask * dot(uv_t, bk_h)
        for f in fillers:
            for _ in f:
                pass
        return [jnp.concatenate([ys[n, pr] for pr in pairs], axis=1) for n in range(len(gis))]

    def run_all(gens):
        for gen in gens:
            for _ in gen:
                pass

    def chain(gens):
        for gen in gens:
            yield from gen

    groups = [list(range(s0, min(s0 + together, rows_per_step))) for s0 in range(0, rows_per_step, together)]
    prepared = [dict() for _ in groups[0]]
    run_all([prepare(gi, prepared[n]) for n, gi in enumerate(groups[0])])
    finishing = []
    for gidx, gis in enumerate(groups):
        following, fillers = [], list(finishing)
        if gidx + 1 < len(groups):
            following = [dict() for _ in groups[gidx + 1]]
            fillers.append(chain([prepare(gi, following[n]) for n, gi in enumerate(groups[gidx + 1])]))
        ys = scan_chunk(gis, prepared, fillers)
        finishing = [chain([finish(gi, prepared[n], ys[n]) for n, gi in enumerate(gis)])]
        prepared = following
    run_all(finishing)


def _rwkv(p3, prev0, s0z, mu_pad, par, w2p, a2p, g2p, *, chunk, n_valid, rows_per_step):
    b, t, _ = p3.shape
    n_pairs = D_MODEL // LANES
    g = rows_per_step
    assert b % g == 0
    full = lambda shape: pl.BlockSpec(shape, lambda i, c: (0,) * len(shape))
    return pl.pallas_call(
        functools.partial(_rwkv_kernel, chunk=chunk, n_valid=n_valid, together=RWKV_ROWS_TOGETHER),
        grid=(b // g, t // chunk),
        in_specs=[
            pl.BlockSpec((g, chunk, RW_PAD), lambda i, c: (i, c, 0)),
            pl.BlockSpec((g, 1, RW_PAD), lambda i, c: (i, 0, 0)),
            pl.BlockSpec((g, n_pairs, LANES, LANES), lambda i, c: (i, 0, 0, 0)),
            full((1, RW_PAD)),
            full((SUBLANES, D_MODEL)),
            full((LANES, D_MODEL)),
            full((LANES, D_MODEL)),
            full((2 * LANES, D_MODEL)),
        ],
        out_specs=[
            pl.BlockSpec((g, chunk, D_MODEL), lambda i, c: (i, c, 0)),
            pl.BlockSpec((g, n_pairs, LANES, LANES), lambda i, c: (i, 0, 0, 0)),
        ],
        out_shape=[
            jax.ShapeDtypeStruct((b, t, D_MODEL), BF16),
            jax.ShapeDtypeStruct((b, n_pairs, LANES, LANES), F32),
        ],
        scratch_shapes=[pltpu.VMEM((g, 1, RW_PAD), F32)],
        compiler_params=_cparams(("parallel", "arbitrary")),
        name="rwkv",
    )(p3, prev0, s0z, mu_pad, par, w2p, a2p, g2p)


def _attn_kernel(*refs, blk, dil, subs, has_prev):
    if has_prev:
        q_ref, ko_ref, vo_ref, kp_ref, vp_ref, o_ref, l_ref = refs
    else:
        q_ref, ko_ref, vo_ref, o_ref, l_ref = refs
    n = pl.program_id(1)
    unit = dil * blk
    scale = ATT_HEAD ** -0.5

    def class_rows(j, r):
        return pl.ds(j * unit + r, blk, stride=dil) if dil > 1 else pl.ds(j * unit, blk)

    def window_mask(j):
        two = has_prev or j > 0
        nk = 2 * blk if two else blk
        qi = lax.broadcasted_iota(jnp.int32, (blk, nk), 0)
        ki = lax.broadcasted_iota(jnp.int32, (blk, nk), 1)
        if not two:
            return ki <= qi
        valid = (ki >= qi) & (ki <= qi + blk)
        if j == 0:
            valid = valid & (ki >= jnp.where(n == 0, blk, 0))
        return valid

    def keys_values(j, r):
        rows = class_rows(j, r)
        kc = ko_ref[0, rows, :]
        vc = vo_ref[0, rows, :]
        if j > 0:
            kc = jnp.concatenate([ko_ref[0, class_rows(j - 1, r), :], kc], axis=0)
            vc = jnp.concatenate([vo_ref[0, class_rows(j - 1, r), :], vc], axis=0)
        elif has_prev:
            kc = jnp.concatenate([kp_ref[0, class_rows(0, r), :], kc], axis=0)
            vc = jnp.concatenate([vp_ref[0, class_rows(0, r), :], vc], axis=0)
        return kc.astype(BF16), vc.astype(BF16)

    windows = [(j, r) for j in range(subs) for r in range(dil)]
    for w0 in range(0, len(windows), ATTN_WINDOWS_TOGETHER):
        group = windows[w0:w0 + ATTN_WINDOWS_TOGETHER]
        kvs = [keys_values(j, r) for j, r in group]
        scores = [_nt_dot(q_ref[0, class_rows(j, r), :].astype(BF16), kvs[i][0]) * scale
                  for i, (j, r) in enumerate(group)]
        probs, dens, lses = [], [], []
        for i, (j, r) in enumerate(group):
            s = jnp.where(window_mask(j), scores[i], -jnp.inf)
            m = jnp.max(s, axis=-1, keepdims=True)
            pexp = jnp.exp(s - m)
            den = jnp.sum(pexp, axis=-1, keepdims=True)
            probs.append(pexp.astype(BF16))
            dens.append(den)
            lses.append(m + jnp.log(den))
        for i, (j, r) in enumerate(group):
            rows = class_rows(j, r)
            o = jnp.dot(probs[i], kvs[i][1], preferred_element_type=F32) / dens[i]
            o_ref[0, rows, :] = o.astype(o_ref.dtype)
            l_ref[0, rows, :] = jnp.broadcast_to(lses[i], (blk, ATT_HEAD))


ATTN_STEP_ROWS = 1024
ATTN_WINDOWS_TOGETHER = 8


def _attn_prompt(p3, g, dil, blk):
    b, t, _ = p3.shape
    unit = dil * blk
    assert t % unit == 0
    subs = max(1, min(ATTN_STEP_ROWS, t) // unit)
    span = unit * subs
    assert t % span == 0
    nb = t // span
    has_prev = nb > 1
    q_col = (T_Q * PROJ_TN + g * ATT_OUT) // ATT_HEAD
    k_col = (T_KV * PROJ_TN + g * 2 * ATT_OUT) // ATT_HEAD
    v_col = k_col + ATT_HPG
    cur = lambda col: pl.BlockSpec((1, span, ATT_HEAD), lambda bi, n, h: (bi, n, col + h))
    prev = lambda col: pl.BlockSpec((1, unit, ATT_HEAD),
                                    lambda bi, n, h: (bi, jnp.maximum(n * subs - 1, 0), col + h))
    in_specs = [cur(q_col), cur(k_col), cur(v_col)]
    if has_prev:
        in_specs += [prev(k_col), prev(v_col)]
    out_sd = jax.ShapeDtypeStruct((b, t, ATT_OUT), F32)
    out_spec = pl.BlockSpec((1, span, ATT_HEAD), lambda bi, n, h: (bi, n, h))
    o, l = pl.pallas_call(
        functools.partial(_attn_kernel, blk=blk, dil=dil, subs=subs, has_prev=has_prev),
        grid=(b, nb, ATT_HPG),
        in_specs=in_specs,
        out_specs=[out_spec, out_spec],
        out_shape=[out_sd, out_sd],
        compiler_params=_cparams(("parallel", "parallel", "arbitrary")),
        name=f"attn_prompt_g{g}",
    )(*([p3] * len(in_specs)))
    return o.reshape(b * t, ATT_OUT), l.reshape(b * t, ATT_OUT)


def _kv_rows_kernel(x_ref, o_ref):
    for s in range(2):
        for h in range(ATT_HPG):
            c0 = (s * ATT_HPG + h) * ATT_HEAD
            o_ref[0, :, s, h, :] = x_ref[0, :, c0:c0 + ATT_HEAD]


def _kv_rows(p3, g, keep, tr):
    b, t, _ = p3.shape
    col = (T_KV * PROJ_TN) // (2 * ATT_OUT) + g
    r0 = (t - keep) // tr
    return pl.pallas_call(
        _kv_rows_kernel,
        grid=(b, keep // tr),
        in_specs=[pl.BlockSpec((1, tr, 2 * ATT_OUT), lambda bi, i: (bi, r0 + i, col))],
        out_specs=pl.BlockSpec((1, tr, 2, ATT_HPG, ATT_HEAD), lambda bi, i: (bi, i, 0, 0, 0)),
        out_shape=jax.ShapeDtypeStruct((b, keep, 2, ATT_HPG, ATT_HEAD), F32),
        compiler_params=_cparams(("parallel", "parallel")),
        name=f"kv_rows_g{g}",
    )(p3)


def _attn_sample_kernel(q_ref, n0_ref, n1_ref, n2_ref, c0_ref, c1_ref, c2_ref, o_ref, l_ref, *, dils, ts):
    n_refs = (n0_ref, n1_ref, n2_ref)
    c_refs = (c0_ref, c1_ref, c2_ref)
    scale = ATT_HEAD ** -0.5
    for g in range(N_GROUPS):
        dil = dils[g]
        new = n_refs[g][0]
        kn, vn = new[:, 0], new[:, 1]
        rows = c_refs[g].shape[2] // dil
        wi = lax.broadcasted_iota(jnp.int32, (rows, ATT_HPG, 1), 0)
        ni = lax.broadcasted_iota(jnp.int32, (ts, ATT_HPG, 1), 0)
        for t in range(ts):
            cls = pl.ds(t % dil, rows, stride=dil) if dil > 1 else pl.ds(0, rows)
            cache = c_refs[g][0, 0, cls]
            kc, vc = cache[:, 0], cache[:, 1]
            q = q_ref[0, t, g]
            sc = jnp.sum(kc * q[None], axis=-1, keepdims=True) * scale
            sn = jnp.sum(kn * q[None], axis=-1, keepdims=True) * scale
            c_ok = wi >= (t // dil)
            n_ok = functools.reduce(jnp.logical_or, [ni == tn for tn in range(t + 1) if (t - tn) % dil == 0])
            sc = jnp.where(c_ok, sc, -jnp.inf)
            sn = jnp.where(n_ok, sn, -jnp.inf)
            m = jnp.maximum(jnp.max(sc, axis=0, keepdims=True), jnp.max(sn, axis=0, keepdims=True))
            pc = jnp.exp(sc - m)
            pn = jnp.exp(sn - m)
            den = jnp.sum(pc, axis=0, keepdims=True) + jnp.sum(pn, axis=0, keepdims=True)
            o = (jnp.sum(pc * vc, axis=0, keepdims=True) + jnp.sum(pn * vn, axis=0, keepdims=True)) / den
            o_ref[0, t, g] = o[0]
            l_ref[0, t, g] = jnp.broadcast_to((m + jnp.log(den))[0], (ATT_HPG, ATT_HEAD))


def _attn_sample(q5, new_kv, caches):
    b, ts = q5.shape[:2]
    dils = tuple(d for _, d in DILATION_GROUPS)
    in_specs = [pl.BlockSpec((1, ts, N_GROUPS, ATT_HPG, ATT_HEAD), lambda bi: (bi, 0, 0, 0, 0))]
    for g in range(N_GROUPS):
        in_specs.append(pl.BlockSpec((1, ts, 2, ATT_HPG, ATT_HEAD), lambda bi: (bi, 0, 0, 0, 0)))
    for g in range(N_GROUPS):
        w = caches[g].shape[2]
        in_specs.append(pl.BlockSpec((1, 1, w, 2, ATT_HPG, ATT_HEAD), lambda bi: (0, bi, 0, 0, 0, 0)))
    out_sd = jax.ShapeDtypeStruct((b, ts, N_GROUPS, ATT_HPG, ATT_HEAD), F32)
    out_spec = pl.BlockSpec((1, ts, N_GROUPS, ATT_HPG, ATT_HEAD), lambda bi: (bi, 0, 0, 0, 0))
    o, l = pl.pallas_call(
        functools.partial(_attn_sample_kernel, dils=dils, ts=ts),
        grid=(b,),
        in_specs=in_specs,
        out_specs=[out_spec, out_spec],
        out_shape=[out_sd, out_sd],
        compiler_params=_cparams(("parallel",)),
        name="attn_sample",
    )(q5, *new_kv, *caches)
    o_list = [o[:, :, g].reshape(b * ts, ATT_OUT) for g in range(N_GROUPS)]
    l_list = [l[:, :, g].reshape(b * ts, ATT_OUT) for g in range(N_GROUPS)]
    return o_list, l_list


ROLL_ROWS = 64
ROLL_BLOCK_BYTES = 8 << 20


def _roll_kernel(c_ref, n_ref, o_ref):
    w, ts = c_ref.shape[2], n_ref.shape[2]
    body_rows = w - ROLL_ROWS

    def move(i, carry):
        r = pl.multiple_of(i * ROLL_ROWS, ROLL_ROWS)
        o_ref[0, :, pl.ds(r, ROLL_ROWS)] = c_ref[0, :, pl.ds(r + ts, ROLL_ROWS)]
        return carry

    lax.fori_loop(0, body_rows // ROLL_ROWS, move, 0)
    o_ref[0, :, body_rows:w - ts] = c_ref[0, :, body_rows + ts:w]
    o_ref[0, :, w - ts:w] = n_ref[0]


def _cache_roll(cache, new_rows):
    _, b, w = cache.shape[:3]
    ts = new_rows.shape[2]
    assert w % ROLL_ROWS == 0 and ts < ROLL_ROWS
    row_bytes = 2 * ATT_OUT * 4
    bb = max(1, min(b, ROLL_BLOCK_BYTES // (w * row_bytes)))
    assert b % bb == 0
    tail = (2, ATT_HPG, ATT_HEAD)
    return pl.pallas_call(
        _roll_kernel,
        grid=(b // bb,),
        in_specs=[pl.BlockSpec((1, bb, w) + tail, lambda i: (0, i, 0, 0, 0, 0)),
                  pl.BlockSpec((1, bb, ts) + tail, lambda i: (0, i, 0, 0, 0, 0))],
        out_specs=pl.BlockSpec((1, bb, w) + tail, lambda i: (0, i, 0, 0, 0, 0)),
        out_shape=jax.ShapeDtypeStruct(cache.shape, cache.dtype),
        compiler_params=_cparams(("parallel",)),
        name="cache_roll",
    )(cache, new_rows)


def _merge_kernel(x_ref, yrw_ref, gate_ref, o0_ref, o1_ref, o2_ref, l0_ref, l1_ref, l2_ref,
                  wba_ref, wbb_ref, wo_ref, fnw_ref, rw_ref, rb_ref,
                  h_ref, hn_ref, ti_ref, tg_ref, cnt_ref, run_ref):
    @pl.when(pl.program_id(0) == 0)
    def _():
        run_ref[...] = jnp.zeros(run_ref.shape, F32)

    l0, l1, l2 = l0_ref[...], l1_ref[...], l2_ref[...]
    m = jnp.maximum(jnp.maximum(l0, l1), l2)
    e0, e1, e2 = jnp.exp(l0 - m), jnp.exp(l1 - m), jnp.exp(l2 - m)
    o_att = (e0 * o0_ref[...] + e1 * o1_ref[...] + e2 * o2_ref[...]) / (e0 + e1 + e2)
    br_a = jnp.dot(yrw_ref[...], wba_ref[...], preferred_element_type=F32)
    br_b = jnp.dot(o_att.astype(BF16), wbb_ref[...], preferred_element_type=F32)
    merged = gate_ref[:, 0:D_MODEL] * br_a + gate_ref[:, D_MODEL:2 * D_MODEL] * br_b
    h = x_ref[...] + jnp.dot(merged.astype(BF16), wo_ref[...], preferred_element_type=F32)
    h_ref[...] = h
    hn = h * lax.rsqrt(jnp.mean(h * h, axis=-1, keepdims=True) + NORM_EPS) * fnw_ref[...]
    hn_ref[...] = hn
    logits = jnp.dot(hn.astype(BF16), rw_ref[...], preferred_element_type=F32) + rb_ref[...]
    lane_i = lax.broadcasted_iota(jnp.int32, logits.shape, 1)
    lane = lane_i.astype(F32)
    vals, idxs = [], []
    cur = logits
    for _ in range(TOP_K):
        mx = jnp.max(cur, axis=-1, keepdims=True)
        ix = jnp.min(jnp.where(cur == mx, lane, float(LANES)), axis=-1, keepdims=True)
        vals.append(mx)
        idxs.append(ix)
        cur = jnp.where(lane == ix, -jnp.inf, cur)
    es = [jnp.exp(vv - vals[0]) for vv in vals]
    tot = es[0] + es[1] + es[2] + es[3]
    tm = logits.shape[0]
    ri = lax.broadcasted_iota(jnp.int32, (tm, tm), 0)
    rj = lax.broadcasted_iota(jnp.int32, (tm, tm), 1)
    before = jnp.where(ri > rj, 1.0, 0.0).astype(BF16)
    run = run_ref[0:1, :]
    ranks = []
    for kq in range(TOP_K):
        onehot = jnp.where(lane == idxs[kq], 1.0, 0.0)
        prior = jnp.dot(before, onehot.astype(BF16), preferred_element_type=F32) + run
        ranks.append(jnp.sum(onehot * prior, axis=-1, keepdims=True))
        run = run + jnp.sum(onehot, axis=0, keepdims=True)
    run_ref[...] = jnp.broadcast_to(run, run_ref.shape)
    cnt_ref[...] = jnp.broadcast_to(run, cnt_ref.shape)

    ti = jnp.zeros(logits.shape, F32)
    tg = jnp.zeros(logits.shape, F32)
    for kq in range(TOP_K):
        ti = jnp.where(lane_i == kq, idxs[kq], ti)
        ti = jnp.where(lane_i == TOP_K + kq, ranks[kq], ti)
        tg = jnp.where(lane_i == kq, es[kq] / tot, tg)
    ti_ref[...] = ti.astype(jnp.int32)
    tg_ref[...] = tg


def _merge(x2d, yrw, p2d, o_list, l_list, wba, wbb, wo, fnw, rw_pad, rb_pad, *, tm):
    n = x2d.shape[0]
    row = lambda wdt: pl.BlockSpec((tm, wdt), lambda i: (i, 0))
    full = lambda a: pl.BlockSpec(a.shape, lambda i: (0,) * a.ndim)
    gate_blk = (T_GATE * PROJ_TN) // (2 * D_MODEL)
    return pl.pallas_call(
        _merge_kernel,
        grid=(n // tm,),
        in_specs=[row(D_MODEL), row(D_MODEL),
                  pl.BlockSpec((tm, 2 * D_MODEL), lambda i: (i, gate_blk)),
                  row(ATT_OUT), row(ATT_OUT), row(ATT_OUT), row(ATT_OUT), row(ATT_OUT), row(ATT_OUT),
                  full(wba), full(wbb), full(wo), full(fnw), full(rw_pad), full(rb_pad)],
        out_specs=[row(D_MODEL), row(D_MODEL), row(LANES), row(LANES),
                   pl.BlockSpec((SUBLANES, LANES), lambda i: (0, 0))],
        out_shape=[jax.ShapeDtypeStruct((n, D_MODEL), F32), jax.ShapeDtypeStruct((n, D_MODEL), F32),
                   jax.ShapeDtypeStruct((n, LANES), jnp.int32), jax.ShapeDtypeStruct((n, LANES), F32),
                   jax.ShapeDtypeStruct((SUBLANES, LANES), F32)],
        scratch_shapes=[pltpu.VMEM((SUBLANES, LANES), F32)],
        compiler_params=_cparams(("arbitrary",)),
        name="merge",
    )(x2d, yrw, p2d, *o_list, *l_list, wba, wbb, wo, fnw, rw_pad, rb_pad)


def _row_copy(src_ref, s_row, dst_ref, d_row, sem):
    return pltpu.make_async_copy(src_ref.at[pl.ds(s_row, 1), :], dst_ref.at[pl.ds(d_row, 1), :], sem)


def _dispatch_kernel(pad_start_ref, pad_len_ref, pos_ref, hn_ref, xs_ref, zrow_ref, sem, *, tt):
    i = pl.program_id(0)

    @pl.when(i == 0)
    def _():
        zrow_ref[...] = jnp.zeros(zrow_ref.shape, F32)

        def per_expert(e, carry):
            s = pad_start_ref[e]
            cnt = pad_len_ref[e]

            def issue(rr, cc):
                _row_copy(zrow_ref, 0, xs_ref, s + rr, sem).start()
                return cc

            def drain(rr, cc):
                _row_copy(zrow_ref, 0, xs_ref, s + rr, sem).wait()
                return cc

            lax.fori_loop(0, cnt, issue, 0)
            lax.fori_loop(0, cnt, drain, 0)
            return carry

        lax.fori_loop(0, N_EXPERTS, per_expert, 0)

    def issue(nn, cc):
        for kq in range(TOP_K):
            _row_copy(hn_ref, nn, xs_ref, pos_ref[0, 0, nn * TOP_K + kq], sem).start(priority=kq % 2)
        return cc

    def drain(nn, cc):
        for kq in range(TOP_K):
            _row_copy(hn_ref, nn, xs_ref, pos_ref[0, 0, nn * TOP_K + kq], sem).wait()
        return cc

    lax.fori_loop(0, tt, issue, 0)
    lax.fori_loop(0, tt, drain, 0)


def _dispatch(hn, pos3, pad_start, pad_len, rows, *, tt):
    n = hn.shape[0]
    return pl.pallas_call(
        functools.partial(_dispatch_kernel, tt=tt),
        grid_spec=pltpu.PrefetchScalarGridSpec(
            num_scalar_prefetch=2,
            grid=(n // tt,),
            in_specs=[
                pl.BlockSpec((1, 1, tt * TOP_K), lambda i, ps, plen: (i, 0, 0), memory_space=pltpu.SMEM),
                pl.BlockSpec((tt, D_MODEL), lambda i, ps, plen: (i, 0)),
            ],
            out_specs=pl.BlockSpec(memory_space=pl.ANY),
            scratch_shapes=[pltpu.VMEM((SUBLANES, D_MODEL), F32), pltpu.SemaphoreType.DMA(())],
        ),
        out_shape=jax.ShapeDtypeStruct((rows, D_MODEL), F32),
        compiler_params=_cparams(("arbitrary",)),
        name="moe_dispatch",
    )(pad_start, pad_len, pos3, hn)


def _moe_kernel(be_ref, nu_ref, xs_ref, w1_ref, b1_ref, w2_ref, b2_ref, y_ref):
    i = pl.program_id(0)

    @pl.when(i < nu_ref[0])
    def _():
        x = xs_ref[...].astype(BF16)
        hdn = jnp.dot(x, w1_ref[0], preferred_element_type=F32) + b1_ref[0]
        glu = jnp.minimum(hdn[:, 0:D_MODEL], SWIGLU_LIMIT)
        lin = jnp.clip(hdn[:, D_MODEL:2 * D_MODEL], -SWIGLU_LIMIT, SWIGLU_LIMIT)
        act = glu * jax.nn.sigmoid(SWIGLU_ALPHA * glu) * (lin + 1.0)
        y_ref[...] = jnp.dot(act.astype(BF16), w2_ref[0], preferred_element_type=F32) + b2_ref[0]

    @pl.when(i >= nu_ref[0])
    def _():
        y_ref[...] = jnp.zeros(y_ref.shape, F32)


def _moe(xs, blk_expert, n_used, w1, b1, w2, b2, *, tm):
    rows = xs.shape[0]
    nb = rows // tm
    return pl.pallas_call(
        _moe_kernel,
        grid_spec=pltpu.PrefetchScalarGridSpec(
            num_scalar_prefetch=2,
            grid=(nb,),
            in_specs=[
                pl.BlockSpec((tm, D_MODEL), lambda i, be, nu: (jnp.minimum(i, nu[0] - 1), 0)),
                pl.BlockSpec((1, D_MODEL, 2 * D_MODEL), lambda i, be, nu: (be[i], 0, 0)),
                pl.BlockSpec((1, 1, 2 * D_MODEL), lambda i, be, nu: (be[i], 0, 0)),
                pl.BlockSpec((1, D_MODEL, D_MODEL), lambda i, be, nu: (be[i], 0, 0)),
                pl.BlockSpec((1, 1, D_MODEL), lambda i, be, nu: (be[i], 0, 0)),
            ],
            out_specs=pl.BlockSpec((tm, D_MODEL), lambda i, be, nu: (i, 0)),
        ),
        out_shape=jax.ShapeDtypeStruct((rows, D_MODEL), F32),
        compiler_params=_cparams(("arbitrary",)),
        name="moe_experts",
    )(blk_expert, n_used, xs, w1, b1, w2, b2)


def _combine_kernel(pos_ref, h_ref, tg_ref, yb_ref, o_ref, buf_ref, sem, *, tt):
    def issue(nn, cc):
        for kq in range(TOP_K):
            _row_copy(yb_ref, pos_ref[0, 0, nn * TOP_K + kq], buf_ref.at[kq], nn, sem).start(priority=kq % 2)
        return cc

    def drain(nn, cc):
        for kq in range(TOP_K):
            _row_copy(yb_ref, pos_ref[0, 0, nn * TOP_K + kq], buf_ref.at[kq], nn, sem).wait()
        return cc

    lax.fori_loop(0, tt, issue, 0)
    lax.fori_loop(0, tt, drain, 0)
    acc = h_ref[...]
    for kq in range(TOP_K):
        acc = acc + tg_ref[:, kq:kq + 1] * buf_ref[kq]
    o_ref[...] = acc


def _combine(h, tg, pos3, yb, *, tt):
    n = h.shape[0]
    return pl.pallas_call(
        functools.partial(_combine_kernel, tt=tt),
        grid=(n // tt,),
        in_specs=[
            pl.BlockSpec((1, 1, tt * TOP_K), lambda i: (i, 0, 0), memory_space=pltpu.SMEM),
            pl.BlockSpec((tt, D_MODEL), lambda i: (i, 0)),
            pl.BlockSpec((tt, LANES), lambda i: (i, 0)),
            pl.BlockSpec(memory_space=pl.ANY),
        ],
        out_specs=pl.BlockSpec((tt, D_MODEL), lambda i: (i, 0)),
        out_shape=jax.ShapeDtypeStruct((n, D_MODEL), F32),
        scratch_shapes=[pltpu.VMEM((TOP_K, tt, D_MODEL), F32), pltpu.SemaphoreType.DMA(())],
        compiler_params=_cparams(("arbitrary",)),
        name="moe_combine",
    )(pos3, h, tg, yb)


def _routing(top_idx, rank, counts, tm):
    n = top_idx.shape[0]
    nk = n * TOP_K
    experts = jnp.arange(N_EXPERTS, dtype=jnp.int32)
    blocks_e = (counts + tm - 1) // tm
    upper = (experts[:, None] <= experts[None, :]).astype(jnp.int32)
    block_end = jnp.sum(blocks_e[:, None] * upper, axis=0)
    row_start = (block_end - blocks_e) * tm
    onehot = (top_idx[:, :, None] == experts[None, None, :]).astype(jnp.int32)
    pos = (jnp.sum(onehot * row_start[None, None, :], axis=-1) + rank).reshape(nk).astype(jnp.int32)
    n_blocks = -(-nk // tm) + N_EXPERTS
    blk_ids = jnp.arange(n_blocks, dtype=jnp.int32)
    blk_expert = jnp.minimum(
        jnp.sum((block_end[None, :] <= blk_ids[:, None]).astype(jnp.int32), axis=1), N_EXPERTS - 1
    ).astype(jnp.int32)
    n_used = block_end[-1:].astype(jnp.int32)
    pad_start = (row_start + counts).astype(jnp.int32)
    pad_len = (blocks_e * tm - counts).astype(jnp.int32)
    return pos, blk_expert, n_used, pad_start, pad_len, n_blocks * tm


def _pad_cols(a, width):
    return jnp.pad(a, ((0, 0), (0, width - a.shape[1])))


def _pad_rows(a, height):
    return jnp.pad(a, ((0, height - a.shape[0]), (0, 0)))


def _proj_columns(a):
    d = D_MODEL
    xw = a[:, 3 * d:3 * d + DECAY_LORA]
    xa = a[:, 3 * d + DECAY_LORA:3 * d + DECAY_LORA + AAA_LORA]
    xg = a[:, 3 * d + DECAY_LORA + AAA_LORA:C_RW]
    att = a[:, C_RW:C_RW + 3 * ATT_DIM]
    q, k, v = att[:, :ATT_DIM], att[:, ATT_DIM:2 * ATT_DIM], att[:, 2 * ATT_DIM:]
    parts = [a[:, :3 * d], _pad_cols(xw, LANES), _pad_cols(xa, LANES), _pad_cols(xg, 2 * LANES), q]
    for g in range(N_GROUPS):
        parts.append(k[:, g * ATT_OUT:(g + 1) * ATT_OUT])
        parts.append(v[:, g * ATT_OUT:(g + 1) * ATT_OUT])
    parts.append(a[:, C_RW + 3 * ATT_DIM:])
    return jnp.concatenate(parts, axis=1)


def _rope_tables(pos):
    inv = ROPE_THETA ** (-jnp.arange(0, ATT_HEAD, 2, dtype=F32) / ATT_HEAD)
    ang = pos.astype(F32)[:, None] * inv[None, :]
    cos, sin = jnp.cos(ang), jnp.sin(ang)
    return jnp.concatenate([cos, cos], axis=1), jnp.concatenate([-sin, sin], axis=1)


def _prep_weights(attn_norm_w, w_in, rw_mu, rw_w0, rw_w2, rw_a0, rw_a2, rw_g2, rw_kk, rw_ka, rw_rk,
                  rw_lnx_w, rw_lnx_b, q_norm_w, k_norm_w, w_br, w_out, ffn_norm_w, router_w, router_b,
                  moe_w1, moe_b1, moe_w2, moe_b2):
    w = {}
    w['nw'] = attn_norm_w.reshape(1, D_MODEL)
    w['w_pad'] = _proj_columns(w_in).astype(BF16)
    w['mu_pad'] = _proj_columns(_pad_cols(rw_mu.reshape(1, C_RW), C_RW + 3 * ATT_DIM + 2 * D_MODEL))[:, :RW_PAD]
    w['par'] = jnp.concatenate([rw_w0.reshape(1, -1), rw_a0.reshape(1, -1), rw_kk.reshape(1, -1),
                                rw_ka.reshape(1, -1), rw_rk.reshape(1, -1), rw_lnx_w.reshape(1, -1),
                                rw_lnx_b.reshape(1, -1), jnp.zeros((1, D_MODEL), F32)], axis=0)
    w['w2p'] = _pad_rows(rw_w2, LANES).astype(BF16)
    w['a2p'] = _pad_rows(rw_a2, LANES).astype(BF16)
    w['g2p'] = _pad_rows(rw_g2, 2 * LANES).astype(BF16)
    w['qn'] = q_norm_w.reshape(1, ATT_HEAD)
    w['kn'] = k_norm_w.reshape(1, ATT_HEAD)
    w['wba'] = w_br[:D_MODEL].astype(BF16)
    w['wbb'] = w_br[D_MODEL:].astype(BF16)
    w['wo'] = w_out.astype(BF16)
    w['fnw'] = ffn_norm_w.reshape(1, D_MODEL)
    w['rw_pad'] = _pad_cols(router_w, LANES).astype(BF16)
    w['rb_pad'] = jnp.concatenate([router_b.reshape(1, N_EXPERTS),
                                   jnp.full((1, LANES - N_EXPERTS), -1e30, F32)], axis=1)
    cols = jnp.arange(2 * D_MODEL, dtype=jnp.int32)
    src = jnp.where(cols < D_MODEL, 2 * cols, 2 * (cols - D_MODEL) + 1)
    perm = (jnp.arange(2 * D_MODEL, dtype=jnp.int32)[:, None] == src[None, :]).astype(BF16)
    w['w1'] = jnp.einsum('edh,hk->edk', moe_w1.astype(BF16), perm, preferred_element_type=BF16)
    w['b1'] = jnp.concatenate([moe_b1[:, 0::2], moe_b1[:, 1::2]], axis=1).reshape(N_EXPERTS, 1, 2 * D_MODEL)
    w['w2'] = moe_w2.astype(BF16)
    w['b2'] = moe_b2.reshape(N_EXPERTS, 1, D_MODEL)
    return w


def _state_to_pairs(s):
    b = s.shape[0]
    sr = s.reshape(b, RWKV_HEADS // 2, 2, RWKV_HEAD, RWKV_HEAD)
    z = jnp.zeros_like(sr[:, :, 0])
    top = jnp.concatenate([sr[:, :, 0], z], axis=-1)
    bot = jnp.concatenate([z, sr[:, :, 1]], axis=-1)
    return jnp.concatenate([top, bot], axis=-2)


def _pairs_to_state(zp):
    b = zp.shape[0]
    zr = zp.reshape(b, RWKV_HEADS // 2, 2, RWKV_HEAD, 2, RWKV_HEAD)
    return jnp.stack([zr[:, :, 0, :, 0, :], zr[:, :, 1, :, 1, :]], axis=2).reshape(b, RWKV_HEADS, RWKV_HEAD, RWKV_HEAD)


def _layer(x, pos, shift_prev, wkv0, caches, w, *, proj_tm, merge_tm, moe_tm, tok_tt):
    b, t, _ = x.shape
    n = b * t
    x2d = x.reshape(n, D_MODEL)
    is_prompt = caches is None
    rw_rows = next(g for g in (RWKV_ROWS_PER_STEP, RWKV_ROWS_TOGETHER, 1) if b % g == 0)

    cs_tab, sn_tab = _rope_tables(pos)
    if not is_prompt:
        cs_tab = jnp.tile(cs_tab, (proj_tm // t, 1))
        sn_tab = jnp.tile(sn_tab, (proj_tm // t, 1))
    p2d = _proj(x2d, w['nw'], w['w_pad'], w['qn'], w['kn'], cs_tab, sn_tab,
                tm=proj_tm, n_tiles=N_TILES, do_norm=True)
    p3 = p2d.reshape(b, t, P_PAD)
    shift_out = _rmsnorm_rows(x[:, t - 1, :], w['nw'])

    if is_prompt:
        prev0 = jnp.zeros((b, 1, RW_PAD), F32)
        s0z = jnp.zeros((b, RWKV_HEADS // 2, LANES, LANES), F32)
        yrw, sfin = _rwkv(p3, prev0, s0z, w['mu_pad'], w['par'], w['w2p'], w['a2p'], w['g2p'],
                          chunk=SCAN_CHUNK, n_valid=None, rows_per_step=rw_rows)
        yrw = yrw.reshape(n, D_MODEL)
    else:
        one = jnp.ones((1, ATT_HEAD), F32)
        zero_tab = jnp.zeros((b, ATT_HEAD), F32)
        prev_rw = _proj(shift_prev, w['nw'], w['w_pad'], one, one, zero_tab, zero_tab,
                        tm=b, n_tiles=T_LORA + 1, do_norm=False)
        t_pad = SCAN_CHUNK
        p_rw = jnp.pad(p3[:, :, :RW_PAD], ((0, 0), (0, t_pad - t), (0, 0)))
        yrw, sfin = _rwkv(p_rw, prev_rw.reshape(b, 1, RW_PAD), _state_to_pairs(wkv0), w['mu_pad'], w['par'],
                          w['w2p'], w['a2p'], w['g2p'], chunk=SCAN_CHUNK, n_valid=t, rows_per_step=rw_rows)
        yrw = yrw[:, :t].reshape(n, D_MODEL)
    s_fin = _pairs_to_state(sfin)

    new_kv = []
    if is_prompt:
        o_list, l_list = [], []
        for g, (window, dil) in enumerate(DILATION_GROUPS):
            o_g, l_g = _attn_prompt(p3, g, dil, window // dil)
            o_list.append(o_g)
            l_list.append(l_g)
            keep = min(window, t)
            new_kv.append(_kv_rows(p3, g, keep, min(keep, 256))[None])
    else:
        kv_col = T_KV * PROJ_TN
        q5 = p3[:, :, T_Q * PROJ_TN:T_KV * PROJ_TN].reshape(b, t, N_GROUPS, ATT_HPG, ATT_HEAD)
        kv_new = [p3[:, :, kv_col + g * 2 * ATT_OUT:kv_col + (g + 1) * 2 * ATT_OUT].reshape(b, t, 2, ATT_HPG, ATT_HEAD)
                  for g in range(N_GROUPS)]
        o_list, l_list = _attn_sample(q5, kv_new, caches)
        new_kv = [_cache_roll(caches[g], kv_new[g][None]) for g in range(N_GROUPS)]

    h, hn, ti, tg, cnt = _merge(x2d, yrw, p2d, o_list, l_list, w['wba'], w['wbb'], w['wo'], w['fnw'],
                                w['rw_pad'], w['rb_pad'], tm=merge_tm)

    counts = cnt[0, :N_EXPERTS].astype(jnp.int32)
    pos_rows, blk_expert, n_used, pad_start, pad_len, rows = _routing(
        ti[:, :TOP_K], ti[:, TOP_K:2 * TOP_K], counts, moe_tm)
    pos3 = pos_rows.reshape(n // tok_tt, 1, tok_tt * TOP_K)
    xs = _dispatch(hn, pos3, pad_start, pad_len, rows, tt=tok_tt)
    yb = _moe(xs, blk_expert, n_used, w['w1'], w['b1'], w['w2'], w['b2'], tm=moe_tm)
    y = _combine(h, tg, pos3, yb, tt=tok_tt)
    return y.reshape(b, t, D_MODEL), shift_out, s_fin, new_kv


def kernel(x_prompt, x_sample, state_wkv, state_shift, cache_kv_w128, cache_kv_w512, cache_kv_w2048,
           attn_norm_w, w_in, rw_mu, rw_w0, rw_w2, rw_a0, rw_a2, rw_g2, rw_kk, rw_ka, rw_rk,
           rw_lnx_w, rw_lnx_b, q_norm_w, k_norm_w, w_br, w_out, ffn_norm_w, router_w, router_b,
           moe_w1, moe_b1, moe_w2, moe_b2):
    depth = w_in.shape[0]
    assert depth == 1
    bp, tp, _ = x_prompt.shape
    bs, ts, _ = x_sample.shape
    l = 0
    w = _prep_weights(attn_norm_w[l], w_in[l], rw_mu[l], rw_w0[l], rw_w2[l], rw_a0[l], rw_a2[l], rw_g2[l],
                      rw_kk[l], rw_ka[l], rw_rk[l], rw_lnx_w[l], rw_lnx_b[l], q_norm_w[l], k_norm_w[l],
                      w_br[l], w_out[l], ffn_norm_w[l], router_w[l], router_b[l],
                      moe_w1[l], moe_b1[l], moe_w2[l], moe_b2[l])
    pos_p = jnp.arange(tp, dtype=jnp.int32)
    pos_s = PAST_LEN + jnp.arange(ts, dtype=jnp.int32)
    yp, sh_p, wkv_p, kv_p = _layer(x_prompt, pos_p, None, None, None, w,
                                   proj_tm=min(2048, tp), merge_tm=256, moe_tm=MOE_TM, tok_tt=256)
    ns = bs * ts
    ys, sh_s, wkv_s, kv_s = _layer(x_sample, pos_s, state_shift[l], state_wkv[l],
                                   (cache_kv_w128, cache_kv_w512, cache_kv_w2048), w,
                                   proj_tm=ns, merge_tm=min(256, ns), moe_tm=128, tok_tt=min(256, ns))
    return (yp, ys,
            wkv_p[None], sh_p[None], kv_p[0], kv_p[1], kv_p[2],
            wkv_s[None], sh_s[None], kv_s[0], kv_s[1], kv_s[2])
```

```python
import functools

import jax
import jax.numpy as jnp
from jax import lax
from jax.experimental import pallas as pl
from jax.experimental.pallas import tpu as pltpu

F32 = jnp.float32
BF16 = jnp.bfloat16

D_MODEL = 1024
NORM_EPS = 1e-5
RWKV_HEAD = 64
RWKV_HEADS = D_MODEL // RWKV_HEAD
DECAY_LORA = 64
AAA_LORA = 64
GATE_LORA = 160
GN_EPS = 64e-5
C_RW = 3 * D_MODEL + DECAY_LORA + AAA_LORA + GATE_LORA
ATT_HEAD = 128
ATT_HPG = 4
DILATION_GROUPS = ((128, 1), (512, 4), (2048, 16))
N_GROUPS = 3
ATT_DIM = N_GROUPS * ATT_HPG * ATT_HEAD
ATT_OUT = ATT_HPG * ATT_HEAD
ROPE_THETA = 10000.0
N_EXPERTS = 32
TOP_K = 4
SWIGLU_ALPHA = 1.702
SWIGLU_LIMIT = 7.0
PAST_LEN = 8192

LANES = 128
SUBLANES = 8
VMEM_LIMIT = 56 * 1024 * 1024

PROJ_TN = 512
T_RKV = 0
T_LORA = 6
T_Q = 7
T_KV = 10
T_GATE = 16
N_TILES = 20
P_PAD = N_TILES * PROJ_TN
RW_PAD = (T_LORA + 1) * PROJ_TN

SCAN_CHUNK = 64
SHORT_SCAN_CHUNK = 16
RWKV_ROWS_PER_STEP = 4
RWKV_ROWS_TOGETHER = 2
MOE_TM = 512


def _cparams(sem, vmem=VMEM_LIMIT):
    return pltpu.CompilerParams(dimension_semantics=sem, vmem_limit_bytes=vmem)


def _proj_kernel(x_ref, nw_ref, w_ref, qn_ref, kn_ref, cs_ref, sn_ref, o_ref, xn_ref, *, do_norm):
    j = pl.program_id(1)

    @pl.when(j == 0)
    def _():
        x = x_ref[...]
        if do_norm:
            x = x * lax.rsqrt(jnp.mean(x * x, axis=-1, keepdims=True) + NORM_EPS) * nw_ref[...]
        xn_ref[...] = x.astype(BF16)

    acc = jnp.dot(xn_ref[...], w_ref[...], preferred_element_type=F32)
    is_q = (j >= T_Q) & (j < T_KV)
    is_k = (j >= T_KV) & (j < T_GATE) & ((j - T_KV) % 2 == 0)
    is_qk = is_q | is_k
    is_gate = j >= T_GATE

    @pl.when(is_qk)
    def _():
        nwh = jnp.where(j < T_KV, qn_ref[...], kn_ref[...])
        cs = cs_ref[...]
        sn = sn_ref[...]
        for h in range(ATT_HPG):
            xh = acc[:, h * ATT_HEAD:(h + 1) * ATT_HEAD]
            y = xh * lax.rsqrt(jnp.mean(xh * xh, axis=-1, keepdims=True) + NORM_EPS) * nwh
            o_ref[:, h * ATT_HEAD:(h + 1) * ATT_HEAD] = y * cs + pltpu.roll(y, ATT_HEAD // 2, axis=1) * sn

    @pl.when(is_gate)
    def _():
        o_ref[...] = jax.nn.sigmoid(acc)

    @pl.when(jnp.logical_not(is_qk | is_gate))
    def _():
        o_ref[...] = acc


def _proj(x2d, nw, w_pad, qn, kn, cs_tab, sn_tab, *, tm, n_tiles, do_norm):
    n = x2d.shape[0]
    tab_blocks = cs_tab.shape[0] // tm
    return pl.pallas_call(
        functools.partial(_proj_kernel, do_norm=do_norm),
        grid=(n // tm, n_tiles),
        in_specs=[
            pl.BlockSpec((tm, D_MODEL), lambda i, j: (i, 0)),
            pl.BlockSpec((1, D_MODEL), lambda i, j: (0, 0)),
            pl.BlockSpec((D_MODEL, PROJ_TN), lambda i, j: (0, j)),
            pl.BlockSpec((1, ATT_HEAD), lambda i, j: (0, 0)),
            pl.BlockSpec((1, ATT_HEAD), lambda i, j: (0, 0)),
            pl.BlockSpec((tm, ATT_HEAD), lambda i, j: (i % tab_blocks, 0)),
            pl.BlockSpec((tm, ATT_HEAD), lambda i, j: (i % tab_blocks, 0)),
        ],
        out_specs=pl.BlockSpec((tm, PROJ_TN), lambda i, j: (i, j)),
        out_shape=jax.ShapeDtypeStruct((n, n_tiles * PROJ_TN), F32),
        scratch_shapes=[pltpu.VMEM((tm, D_MODEL), BF16)],
        compiler_params=_cparams(("parallel", "arbitrary")),
        name="proj",
    )(x2d, nw, w_pad, qn, kn, cs_tab, sn_tab)


def _rmsnorm_kernel(x_ref, w_ref, o_ref):
    x = x_ref[...]
    o_ref[...] = x * lax.rsqrt(jnp.mean(x * x, axis=-1, keepdims=True) + NORM_EPS) * w_ref[...]


def _rmsnorm_rows(x2d, w):
    return pl.pallas_call(
        _rmsnorm_kernel,
        out_shape=jax.ShapeDtypeStruct(x2d.shape, F32),
        name="rmsnorm_rows",
    )(x2d, w)


def _split3(x):
    h1 = x.astype(BF16)
    r1 = x - h1.astype(F32)
    h2 = r1.astype(BF16)
    h3 = (r1 - h2.astype(F32)).astype(BF16)
    return h1, h2, h3


def _seg_sum(x, seg_ones):
    rows = x.shape[0]
    hi = x.astype(BF16)
    lo = (x - hi.astype(F32)).astype(BF16)
    hl = jnp.concatenate([hi, lo], axis=0)
    outs = []
    for p in range(D_MODEL // LANES):
        s = jnp.dot(hl[:, p * LANES:(p + 1) * LANES], seg_ones, preferred_element_type=F32)
        outs.append(s[:rows] + s[rows:])
    return jnp.concatenate(outs, axis=1)


def _nt_dot(a, b):
    return lax.dot_general(a, b, (((1,), (1,)), ((), ())), preferred_element_type=F32)


def _rwkv_kernel(p_ref, prev0_ref, s0_ref, mu_ref, par_ref, w2_ref, a2_ref, g2_ref,
                 y_ref, sfin_ref, prev_ref, st_ref, *, chunk, n_valid, together):
    c = pl.program_id(1)
    C = chunk
    rows_per_step = p_ref.shape[0]

    @pl.when(c == 0)
    def _():
        zero = jnp.zeros((RWKV_HEAD, RWKV_HEAD), F32)
        for gi in range(rows_per_step):
            for pr in range(RWKV_HEADS // 2):
                top = jnp.concatenate([s0_ref[gi, 2 * pr], zero], axis=1)
                bot = jnp.concatenate([zero, s0_ref[gi, 2 * pr + 1]], axis=1)
                st_ref[gi, pr] = jnp.concatenate([top, bot], axis=0)
        prev_ref[...] = prev0_ref[...]

    w0 = par_ref[0:1, :]
    a0 = par_ref[1:2, :]
    k_k = par_ref[2:3, :]
    k_a = par_ref[3:4, :]
    r_k = par_ref[4:5, :]
    lnx_w = par_ref[5:6, :]
    lnx_b = par_ref[6:7, :]

    row = lax.broadcasted_iota(jnp.int32, (C, 1), 0)
    li = lax.broadcasted_iota(jnp.int32, (LANES, LANES), 0)
    lj = lax.broadcasted_iota(jnp.int32, (LANES, LANES), 1)
    same_head = (li < RWKV_HEAD) == (lj < RWKV_HEAD)
    seg_ones = jnp.where(same_head, 1.0, 0.0).astype(BF16)
    blk_mask = jnp.where(same_head, 1.0, 0.0)
    ti = lax.broadcasted_iota(jnp.int32, (C, C), 0)
    tj = lax.broadcasted_iota(jnp.int32, (C, C), 1)
    tril_b = jnp.where(ti >= tj, 1.0, 0.0).astype(BF16)
    ti2 = lax.broadcasted_iota(jnp.int32, (C, 2 * C), 0)
    tj2 = lax.broadcasted_iota(jnp.int32, (C, 2 * C), 1) & (C - 1)
    tril_incl2 = jnp.where(ti2 >= tj2, 1.0, 0.0)
    tril_strict2 = jnp.where(ti2 > tj2, 1.0, 0.0)
    lane = lax.broadcasted_iota(jnp.int32, (1, LANES), 1)
    first = lane < RWKV_HEAD
    first_c = lax.broadcasted_iota(jnp.int32, (1, 2 * C), 1) < C
    n_lev = max(1, (C - 1).bit_length())
    pairs = range(D_MODEL // LANES)
    sls = [slice(pr * LANES, (pr + 1) * LANES) for pr in pairs]
    dot = functools.partial(jnp.dot, preferred_element_type=F32)

    def prepare(gi, out):
        p = p_ref[gi]
        shifted = jnp.where(row == 0, prev_ref[gi], pltpu.roll(p, 1, axis=0))
        prev_ref[gi] = p[C - 1:C, :]
        ps = p + (shifted - p) * mu_ref[...]
        r = ps[:, 0:D_MODEL]
        k = ps[:, D_MODEL:2 * D_MODEL]
        v = ps[:, 2 * D_MODEL:3 * D_MODEL]
        xw = ps[:, 3 * D_MODEL:3 * D_MODEL + LANES]
        xa = ps[:, 3 * D_MODEL + LANES:3 * D_MODEL + 2 * LANES]
        xg = ps[:, 3 * D_MODEL + 2 * LANES:RW_PAD]
        yield
        zw = w0 + dot(jnp.tanh(xw).astype(BF16), w2_ref[...])
        nz = -zw
        softplus = jnp.log(1.0 + jnp.exp(-jnp.abs(nz))) + jnp.maximum(nz, 0.0)
        lw = -jnp.exp(-softplus - 0.5)
        a = jax.nn.sigmoid(a0 + dot(xa.astype(BF16), a2_ref[...]))
        g = dot(jax.nn.sigmoid(xg).astype(BF16), g2_ref[...])
        yield
        kkr = k * k_k
        kk = kkr / jnp.maximum(jnp.sqrt(_seg_sum(kkr * kkr, seg_ones)), 1e-12)
        yield
        k2 = k * (1.0 + (a - 1.0) * k_a)
        bonus = _seg_sum(r * k2 * r_k, seg_ones) * v
        yield
        if n_valid is not None:
            valid = (c * C + row) < n_valid
            lw = jnp.where(valid, lw, 0.0)
            kk = jnp.where(valid, kk, 0.0)
            k2 = jnp.where(valid, k2, 0.0)
        l1, l2, l3 = _split3(lw)
        cl = dot(tril_b, l1) + dot(tril_b, l2) + dot(tril_b, l3)
        cl_end = cl[C - 1:C, :]
        yield
        kb = kk * a
        e_inv = jnp.exp(-cl)
        e_tail = jnp.exp(cl_end - cl)
        out.update(v=v, g=g, bonus=bonus, a_t=(-kk) * jnp.exp(cl - lw), r_t=r * jnp.exp(cl))
        yield
        out.update(b_t=kb * e_inv, k_t=k2 * e_inv, b_h=kb * e_tail, k_h=k2 * e_tail, w_end=jnp.exp(cl_end))

    def finish(gi, q, y):
        inv_n = 1.0 / RWKV_HEAD
        mean = _seg_sum(y, seg_ones) * inv_n
        yc = y - mean
        yield
        var = _seg_sum(yc * yc, seg_ones) * inv_n
        yield
        yn = yc * lax.rsqrt(var + GN_EPS) * lnx_w + lnx_b
        y_ref[gi] = ((yn + q['bonus']) * q['g']).astype(y_ref.dtype)

    def scan_chunk(gis, qs, fillers):
        def fill():
            for f in fillers:
                next(f, None)

        units = [(n, pr) for n in range(len(gis)) for pr in pairs]
        def by_head(x, is_first=first):
            return jnp.concatenate([jnp.where(is_first, x, 0.0), jnp.where(is_first, 0.0, x)], axis=0).astype(BF16)

        sel_c = lambda x0, x1: jnp.where(first_c, x0, x1)
        zs = {(n, pr): st_ref[gis[n], pr] for n, pr in units}
        ars = {(n, pr): jnp.concatenate([qs[n]['a_t'][:, sls[pr]], qs[n]['r_t'][:, sls[pr]]], axis=0)
               for n, pr in units}
        q0s = {u: _nt_dot(ars[u].astype(BF16), zs[u].astype(BF16)) for u in units}
        fill()
        xx, lk, mb, mk, vsw = {}, {}, {}, {}, {}
        for u in units:
            n, pr = u
            btb = qs[n]['b_t'][:, sls[pr]].astype(BF16)
            ktb = qs[n]['k_t'][:, sls[pr]].astype(BF16)
            g0 = _nt_dot(jnp.where(first, ars[u], 0.0).astype(BF16), jnp.concatenate([btb, ktb], axis=0))
            g1 = _nt_dot(jnp.where(first, 0.0, ars[u]).astype(BF16), jnp.concatenate([ktb, btb], axis=0))
            xx[u] = sel_c(g0[:C], g1[:C]) * tril_strict2
            lk[u] = (sel_c(g1[:C], g0[:C]) * tril_strict2).astype(BF16)
            mb[u] = (sel_c(g0[C:], g1[C:]) * tril_incl2).astype(BF16)
            mk[u] = (sel_c(g1[C:], g0[C:]) * tril_incl2).astype(BF16)
            v_p = qs[n]['v'][:, sls[pr]]
            vsw[u] = jnp.concatenate([jnp.where(first, 0.0, v_p), jnp.where(first, v_p, 0.0)],
                                     axis=0).astype(BF16)
        fill()
        us = {u: q0s[u][:C] + dot(lk[u], vsw[u]) for u in units}
        for lev in range(n_lev):
            us = {u: us[u] + dot(xx[u].astype(BF16), by_head(us[u])) for u in units}
            if lev < n_lev - 1:
                xx = {u: dot(xx[u].astype(BF16), by_head(xx[u], first_c)) for u in units}
            fill()
        ys = {u: q0s[u][C:] + dot(jnp.concatenate([mb[u], mk[u]], axis=1),
                                  jnp.concatenate([by_head(us[u]), vsw[u]], axis=0)) for u in units}
        for u in units:
            n, pr = u
            sl = sls[pr]
            uv_t = jnp.concatenate([us[u], qs[n]['v'][:, sl]], axis=0).T.astype(BF16)
            bk_h = jnp.concatenate([qs[n]['b_h'][:, sl], qs[n]['k_h'][:, sl]], axis=0).astype(BF16)
            st_ref[gis[n], pr] = zs[u] * qs[n]['w_end'][:, sl] + blk_mask * dot(uv_t, bk_h)
        for f in fillers:
            for _ in f:
                pass
        return [jnp.concatenate([ys[n, pr] for pr in pairs], axis=1) for n in range(len(gis))]

    def run_all(gens):
        for gen in gens:
            for _ in gen:
                pass

    def chain(gens):
        for gen in gens:
            yield from gen

    groups = [list(range(s0, min(s0 + together, rows_per_step))) for s0 in range(0, rows_per_step, together)]
    prepared = [dict() for _ in groups[0]]
    run_all([prepare(gi, prepared[n]) for n, gi in enumerate(groups[0])])
    finishing = []
    for gidx, gis in enumerate(groups):
        following, fillers = [], list(finishing)
        if gidx + 1 < len(groups):
            following = [dict() for _ in groups[gidx + 1]]
            fillers.append(chain([prepare(gi, following[n]) for n, gi in enumerate(groups[gidx + 1])]))
        ys = scan_chunk(gis, prepared, fillers)
        finishing = [chain([finish(gi, prepared[n], ys[n]) for n, gi in enumerate(gis)])]
        prepared = following
    run_all(finishing)

    @pl.when(c == pl.num_programs(1) - 1)
    def _():
        for gi in range(rows_per_step):
            for pr in range(RWKV_HEADS // 2):
                z = st_ref[gi, pr]
                sfin_ref[gi, 2 * pr] = z[:RWKV_HEAD, :RWKV_HEAD]
                sfin_ref[gi, 2 * pr + 1] = z[RWKV_HEAD:, RWKV_HEAD:]


def _rwkv(p3, prev0, s0z, mu_pad, par, w2p, a2p, g2p, *, chunk, n_valid, rows_per_step):
    b, t, _ = p3.shape
    n_pairs = D_MODEL // LANES
    g = rows_per_step
    assert b % g == 0
    full = lambda shape: pl.BlockSpec(shape, lambda i, c: (0,) * len(shape))
    return pl.pallas_call(
        functools.partial(_rwkv_kernel, chunk=chunk, n_valid=n_valid, together=RWKV_ROWS_TOGETHER),
        grid=(b // g, t // chunk),
        in_specs=[
            pl.BlockSpec((g, chunk, RW_PAD), lambda i, c: (i, c, 0)),
            pl.BlockSpec((g, 1, RW_PAD), lambda i, c: (i, 0, 0)),
            pl.BlockSpec((g, RWKV_HEADS, RWKV_HEAD, RWKV_HEAD), lambda i, c: (i, 0, 0, 0)),
            full((1, RW_PAD)),
            full((SUBLANES, D_MODEL)),
            full((LANES, D_MODEL)),
            full((LANES, D_MODEL)),
            full((2 * LANES, D_MODEL)),
        ],
        out_specs=[
            pl.BlockSpec((g, chunk, D_MODEL), lambda i, c: (i, c, 0)),
            pl.BlockSpec((g, RWKV_HEADS, RWKV_HEAD, RWKV_HEAD), lambda i, c: (i, 0, 0, 0)),
        ],
        out_shape=[
            jax.ShapeDtypeStruct((b, t, D_MODEL), BF16),
            jax.ShapeDtypeStruct((b, RWKV_HEADS, RWKV_HEAD, RWKV_HEAD), F32),
        ],
        scratch_shapes=[pltpu.VMEM((g, 1, RW_PAD), F32), pltpu.VMEM((g, n_pairs, LANES, LANES), F32)],
        compiler_params=_cparams(("parallel", "arbitrary")),
        name="rwkv",
    )(p3, prev0, s0z, mu_pad, par, w2p, a2p, g2p)


def _attn_kernel(*refs, blk, dil, subs, has_prev):
    if has_prev:
        q_ref, ko_ref, vo_ref, kp_ref, vp_ref, o_ref, l_ref = refs
    else:
        q_ref, ko_ref, vo_ref, o_ref, l_ref = refs
    n = pl.program_id(1)
    unit = dil * blk
    scale = ATT_HEAD ** -0.5

    def class_rows(j, r):
        return pl.ds(j * unit + r, blk, stride=dil) if dil > 1 else pl.ds(j * unit, blk)

    def window_mask(j):
        two = has_prev or j > 0
        nk = 2 * blk if two else blk
        qi = lax.broadcasted_iota(jnp.int32, (blk, nk), 0)
        ki = lax.broadcasted_iota(jnp.int32, (blk, nk), 1)
        if not two:
            return ki <= qi
        valid = (ki >= qi) & (ki <= qi + blk)
        if j == 0:
            valid = valid & (ki >= jnp.where(n == 0, blk, 0))
        return valid

    def keys_values(j, r):
        rows = class_rows(j, r)
        kc = ko_ref[0, rows, :]
        vc = vo_ref[0, rows, :]
        if j > 0:
            kc = jnp.concatenate([ko_ref[0, class_rows(j - 1, r), :], kc], axis=0)
            vc = jnp.concatenate([vo_ref[0, class_rows(j - 1, r), :], vc], axis=0)
        elif has_prev:
            kc = jnp.concatenate([kp_ref[0, class_rows(0, r), :], kc], axis=0)
            vc = jnp.concatenate([vp_ref[0, class_rows(0, r), :], vc], axis=0)
        return kc.astype(BF16), vc.astype(BF16)

    windows = [(j, r) for j in range(subs) for r in range(dil)]
    for w0 in range(0, len(windows), ATTN_WINDOWS_TOGETHER):
        group = windows[w0:w0 + ATTN_WINDOWS_TOGETHER]
        kvs = [keys_values(j, r) for j, r in group]
        scores = [_nt_dot(q_ref[0, class_rows(j, r), :].astype(BF16), kvs[i][0]) * scale
                  for i, (j, r) in enumerate(group)]
        probs, dens, lses = [], [], []
        for i, (j, r) in enumerate(group):
            s = jnp.where(window_mask(j), scores[i], -jnp.inf)
            m = jnp.max(s, axis=-1, keepdims=True)
            pexp = jnp.exp(s - m)
            den = jnp.sum(pexp, axis=-1, keepdims=True)
            probs.append(pexp.astype(BF16))
            dens.append(den)
            lses.append(m + jnp.log(den))
        for i, (j, r) in enumerate(group):
            rows = class_rows(j, r)
            o = jnp.dot(probs[i], kvs[i][1], preferred_element_type=F32) / dens[i]
            o_ref[0, rows, :] = o.astype(o_ref.dtype)
            l_ref[0, rows, :] = jnp.broadcast_to(lses[i], (blk, ATT_HEAD))


ATTN_STEP_ROWS = 1024
ATTN_WINDOWS_TOGETHER = 8


def _attn_prompt(p3, g, dil, blk):
    b, t, _ = p3.shape
    unit = dil * blk
    assert t % unit == 0
    subs = max(1, min(ATTN_STEP_ROWS, t) // unit)
    span = unit * subs
    assert t % span == 0
    nb = t // span
    has_prev = nb > 1
    q_col = (T_Q * PROJ_TN + g * ATT_OUT) // ATT_HEAD
    k_col = (T_KV * PROJ_TN + g * 2 * ATT_OUT) // ATT_HEAD
    v_col = k_col + ATT_HPG
    cur = lambda col: pl.BlockSpec((1, span, ATT_HEAD), lambda bi, n, h: (bi, n, col + h))
    prev = lambda col: pl.BlockSpec((1, unit, ATT_HEAD),
                                    lambda bi, n, h: (bi, jnp.maximum(n * subs - 1, 0), col + h))
    in_specs = [cur(q_col), cur(k_col), cur(v_col)]
    if has_prev:
        in_specs += [prev(k_col), prev(v_col)]
    out_sd = jax.ShapeDtypeStruct((b, t, ATT_OUT), F32)
    out_spec = pl.BlockSpec((1, span, ATT_HEAD), lambda bi, n, h: (bi, n, h))
    o, l = pl.pallas_call(
        functools.partial(_attn_kernel, blk=blk, dil=dil, subs=subs, has_prev=has_prev),
        grid=(b, nb, ATT_HPG),
        in_specs=in_specs,
        out_specs=[out_spec, out_spec],
        out_shape=[out_sd, out_sd],
        compiler_params=_cparams(("parallel", "parallel", "arbitrary")),
        name=f"attn_prompt_g{g}",
    )(*([p3] * len(in_specs)))
    return o.reshape(b * t, ATT_OUT), l.reshape(b * t, ATT_OUT)


def _kv_rows_kernel(x_ref, o_ref):
    for s in range(2):
        for h in range(ATT_HPG):
            c0 = (s * ATT_HPG + h) * ATT_HEAD
            o_ref[0, :, s, h, :] = x_ref[0, :, c0:c0 + ATT_HEAD]


def _kv_rows(p3, g, keep, tr):
    b, t, _ = p3.shape
    col = (T_KV * PROJ_TN) // (2 * ATT_OUT) + g
    r0 = (t - keep) // tr
    return pl.pallas_call(
        _kv_rows_kernel,
        grid=(b, keep // tr),
        in_specs=[pl.BlockSpec((1, tr, 2 * ATT_OUT), lambda bi, i: (bi, r0 + i, col))],
        out_specs=pl.BlockSpec((1, tr, 2, ATT_HPG, ATT_HEAD), lambda bi, i: (bi, i, 0, 0, 0)),
        out_shape=jax.ShapeDtypeStruct((b, keep, 2, ATT_HPG, ATT_HEAD), F32),
        compiler_params=_cparams(("parallel", "parallel")),
        name=f"kv_rows_g{g}",
    )(p3)


def _attn_sample_kernel(q_ref, n0_ref, n1_ref, n2_ref, c0_ref, c1_ref, c2_ref, o_ref, l_ref, *, dils, ts):
    n_refs = (n0_ref, n1_ref, n2_ref)
    c_refs = (c0_ref, c1_ref, c2_ref)
    scale = ATT_HEAD ** -0.5
    for g in range(N_GROUPS):
        dil = dils[g]
        new = n_refs[g][0]
        kn, vn = new[:, 0], new[:, 1]
        rows = c_refs[g].shape[2] // dil
        wi = lax.broadcasted_iota(jnp.int32, (rows, ATT_HPG, 1), 0)
        ni = lax.broadcasted_iota(jnp.int32, (ts, ATT_HPG, 1), 0)
        for t in range(ts):
            cls = pl.ds(t % dil, rows, stride=dil) if dil > 1 else pl.ds(0, rows)
            cache = c_refs[g][0, 0, cls]
            kc, vc = cache[:, 0], cache[:, 1]
            q = q_ref[0, t, g]
            sc = jnp.sum(kc * q[None], axis=-1, keepdims=True) * scale
            sn = jnp.sum(kn * q[None], axis=-1, keepdims=True) * scale
            c_ok = wi >= (t // dil)
            n_ok = functools.reduce(jnp.logical_or, [ni == tn for tn in range(t + 1) if (t - tn) % dil == 0])
            sc = jnp.where(c_ok, sc, -jnp.inf)
            sn = jnp.where(n_ok, sn, -jnp.inf)
            m = jnp.maximum(jnp.max(sc, axis=0, keepdims=True), jnp.max(sn, axis=0, keepdims=True))
            pc = jnp.exp(sc - m)
            pn = jnp.exp(sn - m)
            den = jnp.sum(pc, axis=0, keepdims=True) + jnp.sum(pn, axis=0, keepdims=True)
            o = (jnp.sum(pc * vc, axis=0, keepdims=True) + jnp.sum(pn * vn, axis=0, keepdims=True)) / den
            o_ref[0, t, g] = o[0]
            l_ref[0, t, g] = jnp.broadcast_to((m + jnp.log(den))[0], (ATT_HPG, ATT_HEAD))


def _attn_sample(q5, new_kv, caches):
    b, ts = q5.shape[:2]
    dils = tuple(d for _, d in DILATION_GROUPS)
    in_specs = [pl.BlockSpec((1, ts, N_GROUPS, ATT_HPG, ATT_HEAD), lambda bi: (bi, 0, 0, 0, 0))]
    for g in range(N_GROUPS):
        in_specs.append(pl.BlockSpec((1, ts, 2, ATT_HPG, ATT_HEAD), lambda bi: (bi, 0, 0, 0, 0)))
    for g in range(N_GROUPS):
        w = caches[g].shape[2]
        in_specs.append(pl.BlockSpec((1, 1, w, 2, ATT_HPG, ATT_HEAD), lambda bi: (0, bi, 0, 0, 0, 0)))
    out_sd = jax.ShapeDtypeStruct((b, ts, N_GROUPS, ATT_HPG, ATT_HEAD), F32)
    out_spec = pl.BlockSpec((1, ts, N_GROUPS, ATT_HPG, ATT_HEAD), lambda bi: (bi, 0, 0, 0, 0))
    o, l = pl.pallas_call(
        functools.partial(_attn_sample_kernel, dils=dils, ts=ts),
        grid=(b,),
        in_specs=in_specs,
        out_specs=[out_spec, out_spec],
        out_shape=[out_sd, out_sd],
        compiler_params=_cparams(("parallel",)),
        name="attn_sample",
    )(q5, *new_kv, *caches)
    o_list = [o[:, :, g].reshape(b * ts, ATT_OUT) for g in range(N_GROUPS)]
    l_list = [l[:, :, g].reshape(b * ts, ATT_OUT) for g in range(N_GROUPS)]
    return o_list, l_list


ROLL_ROWS = 64
ROLL_BLOCK_BYTES = 8 << 20


def _roll_kernel(c_ref, n_ref, o_ref):
    w, ts = c_ref.shape[2], n_ref.shape[2]
    body_rows = w - ROLL_ROWS

    def move(i, carry):
        r = pl.multiple_of(i * ROLL_ROWS, ROLL_ROWS)
        o_ref[0, :, pl.ds(r, ROLL_ROWS)] = c_ref[0, :, pl.ds(r + ts, ROLL_ROWS)]
        return carry

    lax.fori_loop(0, body_rows // ROLL_ROWS, move, 0)
    o_ref[0, :, body_rows:w - ts] = c_ref[0, :, body_rows + ts:w]
    o_ref[0, :, w - ts:w] = n_ref[0]


def _cache_roll(cache, new_rows):
    _, b, w = cache.shape[:3]
    ts = new_rows.shape[2]
    assert w % ROLL_ROWS == 0 and ts < ROLL_ROWS
    row_bytes = 2 * ATT_OUT * 4
    bb = max(1, min(b, ROLL_BLOCK_BYTES // (w * row_bytes)))
    assert b % bb == 0
    tail = (2, ATT_HPG, ATT_HEAD)
    return pl.pallas_call(
        _roll_kernel,
        grid=(b // bb,),
        in_specs=[pl.BlockSpec((1, bb, w) + tail, lambda i: (0, i, 0, 0, 0, 0)),
                  pl.BlockSpec((1, bb, ts) + tail, lambda i: (0, i, 0, 0, 0, 0))],
        out_specs=pl.BlockSpec((1, bb, w) + tail, lambda i: (0, i, 0, 0, 0, 0)),
        out_shape=jax.ShapeDtypeStruct(cache.shape, cache.dtype),
        compiler_params=_cparams(("parallel",)),
        name="cache_roll",
    )(cache, new_rows)


def _merge_kernel(x_ref, yrw_ref, gate_ref, o0_ref, o1_ref, o2_ref, l0_ref, l1_ref, l2_ref,
                  wba_ref, wbb_ref, wo_ref, fnw_ref, rw_ref, rb_ref,
                  h_ref, hn_ref, ti_ref, tg_ref, cnt_ref, run_ref):
    @pl.when(pl.program_id(0) == 0)
    def _():
        run_ref[...] = jnp.zeros(run_ref.shape, F32)

    l0, l1, l2 = l0_ref[...], l1_ref[...], l2_ref[...]
    m = jnp.maximum(jnp.maximum(l0, l1), l2)
    e0, e1, e2 = jnp.exp(l0 - m), jnp.exp(l1 - m), jnp.exp(l2 - m)
    o_att = (e0 * o0_ref[...] + e1 * o1_ref[...] + e2 * o2_ref[...]) / (e0 + e1 + e2)
    br_a = jnp.dot(yrw_ref[...], wba_ref[...], preferred_element_type=F32)
    br_b = jnp.dot(o_att.astype(BF16), wbb_ref[...], preferred_element_type=F32)
    merged = gate_ref[:, 0:D_MODEL] * br_a + gate_ref[:, D_MODEL:2 * D_MODEL] * br_b
    h = x_ref[...] + jnp.dot(merged.astype(BF16), wo_ref[...], preferred_element_type=F32)
    h_ref[...] = h
    hn = h * lax.rsqrt(jnp.mean(h * h, axis=-1, keepdims=True) + NORM_EPS) * fnw_ref[...]
    hn_ref[...] = hn
    logits = jnp.dot(hn.astype(BF16), rw_ref[...], preferred_element_type=F32) + rb_ref[...]
    lane_i = lax.broadcasted_iota(jnp.int32, logits.shape, 1)
    lane = lane_i.astype(F32)
    vals, idxs = [], []
    cur = logits
    for _ in range(TOP_K):
        mx = jnp.max(cur, axis=-1, keepdims=True)
        ix = jnp.min(jnp.where(cur == mx, lane, float(LANES)), axis=-1, keepdims=True)
        vals.append(mx)
        idxs.append(ix)
        cur = jnp.where(lane == ix, -jnp.inf, cur)
    es = [jnp.exp(vv - vals[0]) for vv in vals]
    tot = es[0] + es[1] + es[2] + es[3]
    tm = logits.shape[0]
    ri = lax.broadcasted_iota(jnp.int32, (tm, tm), 0)
    rj = lax.broadcasted_iota(jnp.int32, (tm, tm), 1)
    before = jnp.where(ri > rj, 1.0, 0.0).astype(BF16)
    run = run_ref[0:1, :]
    ranks = []
    for kq in range(TOP_K):
        onehot = jnp.where(lane == idxs[kq], 1.0, 0.0)
        prior = jnp.dot(before, onehot.astype(BF16), preferred_element_type=F32) + run
        ranks.append(jnp.sum(onehot * prior, axis=-1, keepdims=True))
        run = run + jnp.sum(onehot, axis=0, keepdims=True)
    run_ref[...] = jnp.broadcast_to(run, run_ref.shape)
    cnt_ref[...] = jnp.broadcast_to(run, cnt_ref.shape)

    ti = jnp.zeros(logits.shape, F32)
    tg = jnp.zeros(logits.shape, F32)
    for kq in range(TOP_K):
        ti = jnp.where(lane_i == kq, idxs[kq], ti)
        ti = jnp.where(lane_i == TOP_K + kq, ranks[kq], ti)
        tg = jnp.where(lane_i == kq, es[kq] / tot, tg)
    ti_ref[...] = ti.astype(jnp.int32)
    tg_ref[...] = tg


def _merge(x2d, yrw, p2d, o_list, l_list, wba, wbb, wo, fnw, rw_pad, rb_pad, *, tm):
    n = x2d.shape[0]
    row = lambda wdt: pl.BlockSpec((tm, wdt), lambda i: (i, 0))
    full = lambda a: pl.BlockSpec(a.shape, lambda i: (0,) * a.ndim)
    gate_blk = (T_GATE * PROJ_TN) // (2 * D_MODEL)
    return pl.pallas_call(
        _merge_kernel,
        grid=(n // tm,),
        in_specs=[row(D_MODEL), row(D_MODEL),
                  pl.BlockSpec((tm, 2 * D_MODEL), lambda i: (i, gate_blk)),
                  row(ATT_OUT), row(ATT_OUT), row(ATT_OUT), row(ATT_OUT), row(ATT_OUT), row(ATT_OUT),
                  full(wba), full(wbb), full(wo), full(fnw), full(rw_pad), full(rb_pad)],
        out_specs=[row(D_MODEL), row(D_MODEL), row(LANES), row(LANES),
                   pl.BlockSpec((SUBLANES, LANES), lambda i: (0, 0))],
        out_shape=[jax.ShapeDtypeStruct((n, D_MODEL), F32), jax.ShapeDtypeStruct((n, D_MODEL), F32),
                   jax.ShapeDtypeStruct((n, LANES), jnp.int32), jax.ShapeDtypeStruct((n, LANES), F32),
                   jax.ShapeDtypeStruct((SUBLANES, LANES), F32)],
        scratch_shapes=[pltpu.VMEM((SUBLANES, LANES), F32)],
        compiler_params=_cparams(("arbitrary",)),
        name="merge",
    )(x2d, yrw, p2d, *o_list, *l_list, wba, wbb, wo, fnw, rw_pad, rb_pad)


def _row_copy(src_ref, s_row, dst_ref, d_row, sem):
    return pltpu.make_async_copy(src_ref.at[pl.ds(s_row, 1), :], dst_ref.at[pl.ds(d_row, 1), :], sem)


def _dispatch_kernel(pad_start_ref, pad_len_ref, pos_ref, hn_ref, xs_ref, zrow_ref, sem, *, tt):
    i = pl.program_id(0)

    @pl.when(i == 0)
    def _():
        zrow_ref[...] = jnp.zeros(zrow_ref.shape, F32)

        def per_expert(e, carry):
            s = pad_start_ref[e]
            cnt = pad_len_ref[e]

            def issue(rr, cc):
                _row_copy(zrow_ref, 0, xs_ref, s + rr, sem).start()
                return cc

            def drain(rr, cc):
                _row_copy(zrow_ref, 0, xs_ref, s + rr, sem).wait()
                return cc

            lax.fori_loop(0, cnt, issue, 0)
            lax.fori_loop(0, cnt, drain, 0)
            return carry

        lax.fori_loop(0, N_EXPERTS, per_expert, 0)

    def issue(nn, cc):
        for kq in range(TOP_K):
            _row_copy(hn_ref, nn, xs_ref, pos_ref[0, 0, nn * TOP_K + kq], sem).start(priority=kq % 2)
        return cc

    def drain(nn, cc):
        for kq in range(TOP_K):
            _row_copy(hn_ref, nn, xs_ref, pos_ref[0, 0, nn * TOP_K + kq], sem).wait()
        return cc

    lax.fori_loop(0, tt, issue, 0)
    lax.fori_loop(0, tt, drain, 0)


def _dispatch(hn, pos3, pad_start, pad_len, rows, *, tt):
    n = hn.shape[0]
    return pl.pallas_call(
        functools.partial(_dispatch_kernel, tt=tt),
        grid_spec=pltpu.PrefetchScalarGridSpec(
            num_scalar_prefetch=2,
            grid=(n // tt,),
            in_specs=[
                pl.BlockSpec((1, 1, tt * TOP_K), lambda i, ps, plen: (i, 0, 0), memory_space=pltpu.SMEM),
                pl.BlockSpec((tt, D_MODEL), lambda i, ps, plen: (i, 0)),
            ],
            out_specs=pl.BlockSpec(memory_space=pl.ANY),
            scratch_shapes=[pltpu.VMEM((SUBLANES, D_MODEL), F32), pltpu.SemaphoreType.DMA(())],
        ),
        out_shape=jax.ShapeDtypeStruct((rows, D_MODEL), F32),
        compiler_params=_cparams(("arbitrary",)),
        name="moe_dispatch",
    )(pad_start, pad_len, pos3, hn)


def _moe_kernel(be_ref, nu_ref, xs_ref, w1_ref, b1_ref, w2_ref, b2_ref, y_ref):
    i = pl.program_id(0)

    @pl.when(i < nu_ref[0])
    def _():
        x = xs_ref[...].astype(BF16)
        hdn = jnp.dot(x, w1_ref[0], preferred_element_type=F32) + b1_ref[0]
        glu = jnp.minimum(hdn[:, 0:D_MODEL], SWIGLU_LIMIT)
        lin = jnp.clip(hdn[:, D_MODEL:2 * D_MODEL], -SWIGLU_LIMIT, SWIGLU_LIMIT)
        act = glu * jax.nn.sigmoid(SWIGLU_ALPHA * glu) * (lin + 1.0)
        y_ref[...] = jnp.dot(act.astype(BF16), w2_ref[0], preferred_element_type=F32) + b2_ref[0]

    @pl.when(i >= nu_ref[0])
    def _():
        y_ref[...] = jnp.zeros(y_ref.shape, F32)


def _moe(xs, blk_expert, n_used, w1, b1, w2, b2, *, tm):
    rows = xs.shape[0]
    nb = rows // tm
    return pl.pallas_call(
        _moe_kernel,
        grid_spec=pltpu.PrefetchScalarGridSpec(
            num_scalar_prefetch=2,
            grid=(nb,),
            in_specs=[
                pl.BlockSpec((tm, D_MODEL), lambda i, be, nu: (jnp.minimum(i, nu[0] - 1), 0)),
                pl.BlockSpec((1, D_MODEL, 2 * D_MODEL), lambda i, be, nu: (be[i], 0, 0)),
                pl.BlockSpec((1, 1, 2 * D_MODEL), lambda i, be, nu: (be[i], 0, 0)),
                pl.BlockSpec((1, D_MODEL, D_MODEL), lambda i, be, nu: (be[i], 0, 0)),
                pl.BlockSpec((1, 1, D_MODEL), lambda i, be, nu: (be[i], 0, 0)),
            ],
            out_specs=pl.BlockSpec((tm, D_MODEL), lambda i, be, nu: (i, 0)),
        ),
        out_shape=jax.ShapeDtypeStruct((rows, D_MODEL), F32),
        compiler_params=_cparams(("arbitrary",)),
        name="moe_experts",
    )(blk_expert, n_used, xs, w1, b1, w2, b2)


def _combine_kernel(pos_ref, h_ref, tg_ref, yb_ref, o_ref, buf_ref, sem, *, tt):
    def issue(nn, cc):
        for kq in range(TOP_K):
            _row_copy(yb_ref, pos_ref[0, 0, nn * TOP_K + kq], buf_ref.at[kq], nn, sem).start(priority=kq % 2)
        return cc

    def drain(nn, cc):
        for kq in range(TOP_K):
            _row_copy(yb_ref, pos_ref[0, 0, nn * TOP_K + kq], buf_ref.at[kq], nn, sem).wait()
        return cc

    lax.fori_loop(0, tt, issue, 0)
    lax.fori_loop(0, tt, drain, 0)
    acc = h_ref[...]
    for kq in range(TOP_K):
        acc = acc + tg_ref[:, kq:kq + 1] * buf_ref[kq]
    o_ref[...] = acc


def _combine(h, tg, pos3, yb, *, tt):
    n = h.shape[0]
    return pl.pallas_call(
        functools.partial(_combine_kernel, tt=tt),
        grid=(n // tt,),
        in_specs=[
            pl.BlockSpec((1, 1, tt * TOP_K), lambda i: (i, 0, 0), memory_space=pltpu.SMEM),
            pl.BlockSpec((tt, D_MODEL), lambda i: (i, 0)),
            pl.BlockSpec((tt, LANES), lambda i: (i, 0)),
            pl.BlockSpec(memory_space=pl.ANY),
        ],
        out_specs=pl.BlockSpec((tt, D_MODEL), lambda i: (i, 0)),
        out_shape=jax.ShapeDtypeStruct((n, D_MODEL), F32),
        scratch_shapes=[pltpu.VMEM((TOP_K, tt, D_MODEL), F32), pltpu.SemaphoreType.DMA(())],
        compiler_params=_cparams(("arbitrary",)),
        name="moe_combine",
    )(pos3, h, tg, yb)


def _routing(top_idx, rank, counts, tm):
    n = top_idx.shape[0]
    nk = n * TOP_K
    experts = jnp.arange(N_EXPERTS, dtype=jnp.int32)
    blocks_e = (counts + tm - 1) // tm
    upper = (experts[:, None] <= experts[None, :]).astype(jnp.int32)
    block_end = jnp.sum(blocks_e[:, None] * upper, axis=0)
    row_start = (block_end - blocks_e) * tm
    onehot = (top_idx[:, :, None] == experts[None, None, :]).astype(jnp.int32)
    pos = (jnp.sum(onehot * row_start[None, None, :], axis=-1) + rank).reshape(nk).astype(jnp.int32)
    n_blocks = -(-nk // tm) + N_EXPERTS
    blk_ids = jnp.arange(n_blocks, dtype=jnp.int32)
    blk_expert = jnp.minimum(
        jnp.sum((block_end[None, :] <= blk_ids[:, None]).astype(jnp.int32), axis=1), N_EXPERTS - 1
    ).astype(jnp.int32)
    n_used = block_end[-1:].astype(jnp.int32)
    pad_start = (row_start + counts).astype(jnp.int32)
    pad_len = (blocks_e * tm - counts).astype(jnp.int32)
    return pos, blk_expert, n_used, pad_start, pad_len, n_blocks * tm


def _pad_cols(a, width):
    return jnp.pad(a, ((0, 0), (0, width - a.shape[1])))


def _pad_rows(a, height):
    return jnp.pad(a, ((0, height - a.shape[0]), (0, 0)))


def _proj_columns(a):
    d = D_MODEL
    xw = a[:, 3 * d:3 * d + DECAY_LORA]
    xa = a[:, 3 * d + DECAY_LORA:3 * d + DECAY_LORA + AAA_LORA]
    xg = a[:, 3 * d + DECAY_LORA + AAA_LORA:C_RW]
    att = a[:, C_RW:C_RW + 3 * ATT_DIM]
    q, k, v = att[:, :ATT_DIM], att[:, ATT_DIM:2 * ATT_DIM], att[:, 2 * ATT_DIM:]
    parts = [a[:, :3 * d], _pad_cols(xw, LANES), _pad_cols(xa, LANES), _pad_cols(xg, 2 * LANES), q]
    for g in range(N_GROUPS):
        parts.append(k[:, g * ATT_OUT:(g + 1) * ATT_OUT])
        parts.append(v[:, g * ATT_OUT:(g + 1) * ATT_OUT])
    parts.append(a[:, C_RW + 3 * ATT_DIM:])
    return jnp.concatenate(parts, axis=1)


def _rope_tables(pos):
    inv = ROPE_THETA ** (-jnp.arange(0, ATT_HEAD, 2, dtype=F32) / ATT_HEAD)
    ang = pos.astype(F32)[:, None] * inv[None, :]
    cos, sin = jnp.cos(ang), jnp.sin(ang)
    return jnp.concatenate([cos, cos], axis=1), jnp.concatenate([-sin, sin], axis=1)


def _prep_weights(attn_norm_w, w_in, rw_mu, rw_w0, rw_w2, rw_a0, rw_a2, rw_g2, rw_kk, rw_ka, rw_rk,
                  rw_lnx_w, rw_lnx_b, q_norm_w, k_norm_w, w_br, w_out, ffn_norm_w, router_w, router_b,
                  moe_w1, moe_b1, moe_w2, moe_b2):
    w = {}
    w['nw'] = attn_norm_w.reshape(1, D_MODEL)
    w['w_pad'] = _proj_columns(w_in).astype(BF16)
    w['mu_pad'] = _proj_columns(_pad_cols(rw_mu.reshape(1, C_RW), C_RW + 3 * ATT_DIM + 2 * D_MODEL))[:, :RW_PAD]
    w['par'] = jnp.concatenate([rw_w0.reshape(1, -1), rw_a0.reshape(1, -1), rw_kk.reshape(1, -1),
                                rw_ka.reshape(1, -1), rw_rk.reshape(1, -1), rw_lnx_w.reshape(1, -1),
                                rw_lnx_b.reshape(1, -1), jnp.zeros((1, D_MODEL), F32)], axis=0)
    w['w2p'] = _pad_rows(rw_w2, LANES).astype(BF16)
    w['a2p'] = _pad_rows(rw_a2, LANES).astype(BF16)
    w['g2p'] = _pad_rows(rw_g2, 2 * LANES).astype(BF16)
    w['qn'] = q_norm_w.reshape(1, ATT_HEAD)
    w['kn'] = k_norm_w.reshape(1, ATT_HEAD)
    w['wba'] = w_br[:D_MODEL].astype(BF16)
    w['wbb'] = w_br[D_MODEL:].astype(BF16)
    w['wo'] = w_out.astype(BF16)
    w['fnw'] = ffn_norm_w.reshape(1, D_MODEL)
    w['rw_pad'] = _pad_cols(router_w, LANES).astype(BF16)
    w['rb_pad'] = jnp.concatenate([router_b.reshape(1, N_EXPERTS),
                                   jnp.full((1, LANES - N_EXPERTS), -1e30, F32)], axis=1)
    cols = jnp.arange(2 * D_MODEL, dtype=jnp.int32)
    src = jnp.where(cols < D_MODEL, 2 * cols, 2 * (cols - D_MODEL) + 1)
    perm = (jnp.arange(2 * D_MODEL, dtype=jnp.int32)[:, None] == src[None, :]).astype(BF16)
    w['w1'] = jnp.einsum('edh,hk->edk', moe_w1.astype(BF16), perm, preferred_element_type=BF16)
    w['b1'] = jnp.concatenate([moe_b1[:, 0::2], moe_b1[:, 1::2]], axis=1).reshape(N_EXPERTS, 1, 2 * D_MODEL)
    w['w2'] = moe_w2.astype(BF16)
    w['b2'] = moe_b2.reshape(N_EXPERTS, 1, D_MODEL)
    return w


def _layer(x, pos, shift_prev, wkv0, caches, w, *, proj_tm, merge_tm, moe_tm, tok_tt):
    b, t, _ = x.shape
    n = b * t
    x2d = x.reshape(n, D_MODEL)
    is_prompt = caches is None
    rw_rows = next(g for g in (RWKV_ROWS_PER_STEP, RWKV_ROWS_TOGETHER, 1) if b % g == 0)

    cs_tab, sn_tab = _rope_tables(pos)
    if not is_prompt:
        cs_tab = jnp.tile(cs_tab, (proj_tm // t, 1))
        sn_tab = jnp.tile(sn_tab, (proj_tm // t, 1))
    p2d = _proj(x2d, w['nw'], w['w_pad'], w['qn'], w['kn'], cs_tab, sn_tab,
                tm=proj_tm, n_tiles=N_TILES, do_norm=True)
    p3 = p2d.reshape(b, t, P_PAD)
    shift_out = _rmsnorm_rows(x[:, t - 1, :], w['nw'])

    if is_prompt:
        prev0 = jnp.zeros((b, 1, RW_PAD), F32)
        s0z = jnp.zeros((b, RWKV_HEADS, RWKV_HEAD, RWKV_HEAD), F32)
        yrw, s_fin = _rwkv(p3, prev0, s0z, w['mu_pad'], w['par'], w['w2p'], w['a2p'], w['g2p'],
                          chunk=SCAN_CHUNK, n_valid=None, rows_per_step=rw_rows)
        yrw = yrw.reshape(n, D_MODEL)
    else:
        one = jnp.ones((1, ATT_HEAD), F32)
        zero_tab = jnp.zeros((b, ATT_HEAD), F32)
        prev_rw = _proj(shift_prev, w['nw'], w['w_pad'], one, one, zero_tab, zero_tab,
                        tm=b, n_tiles=T_LORA + 1, do_norm=False)
        t_pad = -(-t // SHORT_SCAN_CHUNK) * SHORT_SCAN_CHUNK
        p_rw = jnp.pad(p3[:, :, :RW_PAD], ((0, 0), (0, t_pad - t), (0, 0)))
        yrw, s_fin = _rwkv(p_rw, prev_rw.reshape(b, 1, RW_PAD), wkv0, w['mu_pad'], w['par'],
                          w['w2p'], w['a2p'], w['g2p'], chunk=SHORT_SCAN_CHUNK, n_valid=t, rows_per_step=rw_rows)
        yrw = yrw[:, :t].reshape(n, D_MODEL)

    new_kv = []
    if is_prompt:
        o_list, l_list = [], []
        for g, (window, dil) in enumerate(DILATION_GROUPS):
            o_g, l_g = _attn_prompt(p3, g, dil, window // dil)
            o_list.append(o_g)
            l_list.append(l_g)
            keep = min(window, t)
            new_kv.append(_kv_rows(p3, g, keep, min(keep, 256))[None])
    else:
        kv_col = T_KV * PROJ_TN
        q5 = p3[:, :, T_Q * PROJ_TN:T_KV * PROJ_TN].reshape(b, t, N_GROUPS, ATT_HPG, ATT_HEAD)
        kv_new = [p3[:, :, kv_col + g * 2 * ATT_OUT:kv_col + (g + 1) * 2 * ATT_OUT].reshape(b, t, 2, ATT_HPG, ATT_HEAD)
                  for g in range(N_GROUPS)]
        o_list, l_list = _attn_sample(q5, kv_new, caches)
        new_kv = [_cache_roll(caches[g], kv_new[g][None]) for g in range(N_GROUPS)]

    h, hn, ti, tg, cnt = _merge(x2d, yrw, p2d, o_list, l_list, w['wba'], w['wbb'], w['wo'], w['fnw'],
                                w['rw_pad'], w['rb_pad'], tm=merge_tm)

    counts = cnt[0, :N_EXPERTS].astype(jnp.int32)
    pos_rows, blk_expert, n_used, pad_start, pad_len, rows = _routing(
        ti[:, :TOP_K], ti[:, TOP_K:2 * TOP_K], counts, moe_tm)
    pos3 = pos_rows.reshape(n // tok_tt, 1, tok_tt * TOP_K)
    xs = _dispatch(hn, pos3, pad_start, pad_len, rows, tt=tok_tt)
    yb = _moe(xs, blk_expert, n_used, w['w1'], w['b1'], w['w2'], w['b2'], tm=moe_tm)
    y = _combine(h, tg, pos3, yb, tt=tok_tt)
    return y.reshape(b, t, D_MODEL), shift_out, s_fin, new_kv


def kernel(x_prompt, x_sample, state_wkv, state_shift, cache_kv_w128, cache_kv_w512, cache_kv_w2048,
           attn_norm_w, w_in, rw_mu, rw_w0, rw_w2, rw_a0, rw_a2, rw_g2, rw_kk, rw_ka, rw_rk,
           rw_lnx_w, rw_lnx_b, q_norm_w, k_norm_w, w_br, w_out, ffn_norm_w, router_w, router_b,
           moe_w1, moe_b1, moe_w2, moe_b2):
    depth = w_in.shape[0]
    assert depth == 1
    bp, tp, _ = x_prompt.shape
    bs, ts, _ = x_sample.shape
    l = 0
    w = _prep_weights(attn_norm_w[l], w_in[l], rw_mu[l], rw_w0[l], rw_w2[l], rw_a0[l], rw_a2[l], rw_g2[l],
                      rw_kk[l], rw_ka[l], rw_rk[l], rw_lnx_w[l], rw_lnx_b[l], q_norm_w[l], k_norm_w[l],
                      w_br[l], w_out[l], ffn_norm_w[l], router_w[l], router_b[l],
                      moe_w1[l], moe_b1[l], moe_w2[l], moe_b2[l])
    pos_p = jnp.arange(tp, dtype=jnp.int32)
    pos_s = PAST_LEN + jnp.arange(ts, dtype=jnp.int32)
    yp, sh_p, wkv_p, kv_p = _layer(x_prompt, pos_p, None, None, None, w,
                                   proj_tm=min(2048, tp), merge_tm=256, moe_tm=MOE_TM, tok_tt=256)
    ns = bs * ts
    ys, sh_s, wkv_s, kv_s = _layer(x_sample, pos_s, state_shift[l], state_wkv[l],
                                   (cache_kv_w128, cache_kv_w512, cache_kv_w2048), w,
                                   proj_tm=ns, merge_tm=min(256, ns), moe_tm=128, tok_tt=min(256, ns))
    return (yp, ys,
            wkv_p[None], sh_p[None], kv_p[0], kv_p[1], kv_p[2],
            wkv_s[None], sh_s[None], kv_s[0], kv_s[1], kv_s[2])
```

```python
import functools

import jax
import jax.numpy as jnp
from jax import lax
from jax.experimental import pallas as pl
from jax.experimental.pallas import tpu as pltpu

F32 = jnp.float32
BF16 = jnp.bfloat16

D_MODEL = 1024
NORM_EPS = 1e-5
RWKV_HEAD = 64
RWKV_HEADS = D_MODEL // RWKV_HEAD
DECAY_LORA = 64
AAA_LORA = 64
GATE_LORA = 160
GN_EPS = 64e-5
C_RW = 3 * D_MODEL + DECAY_LORA + AAA_LORA + GATE_LORA
ATT_HEAD = 128
ATT_HPG = 4
DILATION_GROUPS = ((128, 1), (512, 4), (2048, 16))
N_GROUPS = 3
ATT_DIM = N_GROUPS * ATT_HPG * ATT_HEAD
ATT_OUT = ATT_HPG * ATT_HEAD
ROPE_THETA = 10000.0
N_EXPERTS = 32
TOP_K = 4
SWIGLU_ALPHA = 1.702
SWIGLU_LIMIT = 7.0
PAST_LEN = 8192

LANES = 128
SUBLANES = 8
VMEM_LIMIT = 56 * 1024 * 1024

PROJ_TN = 512
T_RKV = 0
T_LORA = 6
T_Q = 7
T_KV = 10
T_GATE = 16
N_TILES = 20
P_PAD = N_TILES * PROJ_TN
RW_PAD = (T_LORA + 1) * PROJ_TN

SCAN_CHUNK = 64
SHORT_SCAN_CHUNK = 16
RWKV_ROWS_PER_STEP = 4
RWKV_ROWS_TOGETHER = 2
MOE_TM = 512


def _cparams(sem, vmem=VMEM_LIMIT):
    return pltpu.CompilerParams(dimension_semantics=sem, vmem_limit_bytes=vmem)


def _proj_kernel(x_ref, nw_ref, w_ref, qn_ref, kn_ref, cs_ref, sn_ref, *rest, do_norm, with_gates):
    if with_gates:
        o_ref, g_ref, xn_ref = rest
    else:
        o_ref, xn_ref = rest
    j = pl.program_id(1)

    @pl.when(j == 0)
    def _():
        x = x_ref[...]
        if do_norm:
            x = x * lax.rsqrt(jnp.mean(x * x, axis=-1, keepdims=True) + NORM_EPS) * nw_ref[...]
        xn_ref[...] = x.astype(BF16)

    acc = jnp.dot(xn_ref[...], w_ref[...], preferred_element_type=F32)
    is_q = (j >= T_Q) & (j < T_KV)
    is_k = (j >= T_KV) & (j < T_GATE) & ((j - T_KV) % 2 == 0)
    is_qk = is_q | is_k
    is_gate = j >= T_GATE

    @pl.when(is_qk)
    def _():
        nwh = jnp.where(j < T_KV, qn_ref[...], kn_ref[...])
        cs = cs_ref[...]
        sn = sn_ref[...]
        for h in range(ATT_HPG):
            xh = acc[:, h * ATT_HEAD:(h + 1) * ATT_HEAD]
            y = xh * lax.rsqrt(jnp.mean(xh * xh, axis=-1, keepdims=True) + NORM_EPS) * nwh
            o_ref[:, h * ATT_HEAD:(h + 1) * ATT_HEAD] = y * cs + pltpu.roll(y, ATT_HEAD // 2, axis=1) * sn

    if with_gates:
        @pl.when(is_gate)
        def _():
            g_ref[...] = jax.nn.sigmoid(acc).astype(g_ref.dtype)

    @pl.when(jnp.logical_not(is_qk | is_gate))
    def _():
        o_ref[...] = acc


def _proj(x2d, nw, w_pad, qn, kn, cs_tab, sn_tab, *, tm, n_tiles, do_norm):
    n = x2d.shape[0]
    tab_blocks = cs_tab.shape[0] // tm
    with_gates = n_tiles > T_GATE
    p_tiles = min(n_tiles, T_GATE)
    out_specs = [pl.BlockSpec((tm, PROJ_TN), lambda i, j: (i, jnp.minimum(j, p_tiles - 1)))]
    out_shape = [jax.ShapeDtypeStruct((n, p_tiles * PROJ_TN), F32)]
    if with_gates:
        out_specs.append(pl.BlockSpec((tm, PROJ_TN), lambda i, j: (i, jnp.maximum(j - T_GATE, 0))))
        out_shape.append(jax.ShapeDtypeStruct((n, (n_tiles - T_GATE) * PROJ_TN), BF16))
    return pl.pallas_call(
        functools.partial(_proj_kernel, do_norm=do_norm, with_gates=with_gates),
        grid=(n // tm, n_tiles),
        in_specs=[
            pl.BlockSpec((tm, D_MODEL), lambda i, j: (i, 0)),
            pl.BlockSpec((1, D_MODEL), lambda i, j: (0, 0)),
            pl.BlockSpec((D_MODEL, PROJ_TN), lambda i, j: (0, j)),
            pl.BlockSpec((1, ATT_HEAD), lambda i, j: (0, 0)),
            pl.BlockSpec((1, ATT_HEAD), lambda i, j: (0, 0)),
            pl.BlockSpec((tm, ATT_HEAD), lambda i, j: (i % tab_blocks, 0)),
            pl.BlockSpec((tm, ATT_HEAD), lambda i, j: (i % tab_blocks, 0)),
        ],
        out_specs=out_specs,
        out_shape=out_shape,
        scratch_shapes=[pltpu.VMEM((tm, D_MODEL), BF16)],
        compiler_params=_cparams(("parallel", "arbitrary")),
        name="proj",
    )(x2d, nw, w_pad, qn, kn, cs_tab, sn_tab)


def _rmsnorm_kernel(x_ref, w_ref, o_ref):
    x = x_ref[...]
    o_ref[...] = x * lax.rsqrt(jnp.mean(x * x, axis=-1, keepdims=True) + NORM_EPS) * w_ref[...]


def _rmsnorm_rows(x2d, w):
    return pl.pallas_call(
        _rmsnorm_kernel,
        out_shape=jax.ShapeDtypeStruct(x2d.shape, F32),
        name="rmsnorm_rows",
    )(x2d, w)


def _split3(x):
    h1 = x.astype(BF16)
    r1 = x - h1.astype(F32)
    h2 = r1.astype(BF16)
    h3 = (r1 - h2.astype(F32)).astype(BF16)
    return h1, h2, h3


def _seg_sum(x, seg_ones):
    rows = x.shape[0]
    hi = x.astype(BF16)
    lo = (x - hi.astype(F32)).astype(BF16)
    hl = jnp.concatenate([hi, lo], axis=0)
    outs = []
    for p in range(D_MODEL // LANES):
        s = jnp.dot(hl[:, p * LANES:(p + 1) * LANES], seg_ones, preferred_element_type=F32)
        outs.append(s[:rows] + s[rows:])
    return jnp.concatenate(outs, axis=1)


def _nt_dot(a, b):
    return lax.dot_general(a, b, (((1,), (1,)), ((), ())), preferred_element_type=F32)


def _rwkv_kernel(p_ref, prev0_ref, s0_ref, mu_ref, par_ref, w2_ref, a2_ref, g2_ref,
                 y_ref, sfin_ref, prev_ref, st_ref, *, chunk, n_valid, together):
    c = pl.program_id(1)
    C = chunk
    rows_per_step = p_ref.shape[0]

    @pl.when(c == 0)
    def _():
        zero = jnp.zeros((RWKV_HEAD, RWKV_HEAD), F32)
        for gi in range(rows_per_step):
            for pr in range(RWKV_HEADS // 2):
                top = jnp.concatenate([s0_ref[gi, 2 * pr], zero], axis=1)
                bot = jnp.concatenate([zero, s0_ref[gi, 2 * pr + 1]], axis=1)
                st_ref[gi, pr] = jnp.concatenate([top, bot], axis=0)
        prev_ref[...] = prev0_ref[...]

    w0 = par_ref[0:1, :]
    a0 = par_ref[1:2, :]
    k_k = par_ref[2:3, :]
    k_a = par_ref[3:4, :]
    r_k = par_ref[4:5, :]
    lnx_w = par_ref[5:6, :]
    lnx_b = par_ref[6:7, :]

    row = lax.broadcasted_iota(jnp.int32, (C, 1), 0)
    li = lax.broadcasted_iota(jnp.int32, (LANES, LANES), 0)
    lj = lax.broadcasted_iota(jnp.int32, (LANES, LANES), 1)
    same_head = (li < RWKV_HEAD) == (lj < RWKV_HEAD)
    seg_ones = jnp.where(same_head, 1.0, 0.0).astype(BF16)
    blk_mask = jnp.where(same_head, 1.0, 0.0)
    ti = lax.broadcasted_iota(jnp.int32, (C, C), 0)
    tj = lax.broadcasted_iota(jnp.int32, (C, C), 1)
    tril_b = jnp.where(ti >= tj, 1.0, 0.0).astype(BF16)
    ti2 = lax.broadcasted_iota(jnp.int32, (C, 2 * C), 0)
    tj2 = lax.broadcasted_iota(jnp.int32, (C, 2 * C), 1) & (C - 1)
    tril_incl2 = jnp.where(ti2 >= tj2, 1.0, 0.0)
    tril_strict2 = jnp.where(ti2 > tj2, 1.0, 0.0)
    lane = lax.broadcasted_iota(jnp.int32, (1, LANES), 1)
    first = lane < RWKV_HEAD
    first_c = lax.broadcasted_iota(jnp.int32, (1, 2 * C), 1) < C
    n_lev = max(1, (C - 1).bit_length())
    pairs = range(D_MODEL // LANES)
    sls = [slice(pr * LANES, (pr + 1) * LANES) for pr in pairs]
    dot = functools.partial(jnp.dot, preferred_element_type=F32)

    def prepare(gi, out):
        p = p_ref[gi]
        shifted = jnp.where(row == 0, prev_ref[gi], pltpu.roll(p, 1, axis=0))
        prev_ref[gi] = p[C - 1:C, :]
        ps = p + (shifted - p) * mu_ref[...]
        r = ps[:, 0:D_MODEL]
        k = ps[:, D_MODEL:2 * D_MODEL]
        v = ps[:, 2 * D_MODEL:3 * D_MODEL]
        xw = ps[:, 3 * D_MODEL:3 * D_MODEL + LANES]
        xa = ps[:, 3 * D_MODEL + LANES:3 * D_MODEL + 2 * LANES]
        xg = ps[:, 3 * D_MODEL + 2 * LANES:RW_PAD]
        yield
        zw = w0 + dot(jnp.tanh(xw).astype(BF16), w2_ref[...])
        nz = -zw
        softplus = jnp.log(1.0 + jnp.exp(-jnp.abs(nz))) + jnp.maximum(nz, 0.0)
        lw = -jnp.exp(-softplus - 0.5)
        a = jax.nn.sigmoid(a0 + dot(xa.astype(BF16), a2_ref[...]))
        g = dot(jax.nn.sigmoid(xg).astype(BF16), g2_ref[...])
        yield
        kkr = k * k_k
        kk = kkr / jnp.maximum(jnp.sqrt(_seg_sum(kkr * kkr, seg_ones)), 1e-12)
        yield
        k2 = k * (1.0 + (a - 1.0) * k_a)
        bonus = _seg_sum(r * k2 * r_k, seg_ones) * v
        yield
        if n_valid is not None:
            valid = (c * C + row) < n_valid
            lw = jnp.where(valid, lw, 0.0)
            kk = jnp.where(valid, kk, 0.0)
            k2 = jnp.where(valid, k2, 0.0)
        l1, l2, l3 = _split3(lw)
        cl = dot(tril_b, l1) + dot(tril_b, l2) + dot(tril_b, l3)
        cl_end = cl[C - 1:C, :]
        yield
        kb = kk * a
        e_inv = jnp.exp(-cl)
        e_tail = jnp.exp(cl_end - cl)
        out.update(v=v, g=g, bonus=bonus, a_t=(-kk) * jnp.exp(cl - lw), r_t=r * jnp.exp(cl))
        yield
        out.update(b_t=kb * e_inv, k_t=k2 * e_inv, b_h=kb * e_tail, k_h=k2 * e_tail, w_end=jnp.exp(cl_end))

    def finish(gi, q, y):
        inv_n = 1.0 / RWKV_HEAD
        mean = _seg_sum(y, seg_ones) * inv_n
        yc = y - mean
        yield
        var = _seg_sum(yc * yc, seg_ones) * inv_n
        yield
        yn = yc * lax.rsqrt(var + GN_EPS) * lnx_w + lnx_b
        y_ref[gi] = ((yn + q['bonus']) * q['g']).astype(y_ref.dtype)

    def scan_chunk(gis, qs, fillers):
        def fill():
            for f in fillers:
                next(f, None)

        units = [(n, pr) for n in range(len(gis)) for pr in pairs]
        def by_head(x, is_first=first):
            return jnp.concatenate([jnp.where(is_first, x, 0.0), jnp.where(is_first, 0.0, x)], axis=0).astype(BF16)

        sel_c = lambda x0, x1: jnp.where(first_c, x0, x1)
        zs = {(n, pr): st_ref[gis[n], pr] for n, pr in units}
        ars = {(n, pr): jnp.concatenate([qs[n]['a_t'][:, sls[pr]], qs[n]['r_t'][:, sls[pr]]], axis=0)
               for n, pr in units}
        q0s = {u: _nt_dot(ars[u].astype(BF16), zs[u].astype(BF16)) for u in units}
        fill()
        xx, lk, mb, mk, vsw = {}, {}, {}, {}, {}
        for u in units:
            n, pr = u
            btb = qs[n]['b_t'][:, sls[pr]].astype(BF16)
            ktb = qs[n]['k_t'][:, sls[pr]].astype(BF16)
            g0 = _nt_dot(jnp.where(first, ars[u], 0.0).astype(BF16), jnp.concatenate([btb, ktb], axis=0))
            g1 = _nt_dot(jnp.where(first, 0.0, ars[u]).astype(BF16), jnp.concatenate([ktb, btb], axis=0))
            xx[u] = sel_c(g0[:C], g1[:C]) * tril_strict2
            lk[u] = (sel_c(g1[:C], g0[:C]) * tril_strict2).astype(BF16)
            mb[u] = (sel_c(g0[C:], g1[C:]) * tril_incl2).astype(BF16)
            mk[u] = (sel_c(g1[C:], g0[C:]) * tril_incl2).astype(BF16)
            v_p = qs[n]['v'][:, sls[pr]]
            vsw[u] = jnp.concatenate([jnp.where(first, 0.0, v_p), jnp.where(first, v_p, 0.0)],
                                     axis=0).astype(BF16)
        fill()
        us = {u: q0s[u][:C] + dot(lk[u], vsw[u]) for u in units}
        for lev in range(n_lev):
            us = {u: us[u] + dot(xx[u].astype(BF16), by_head(us[u])) for u in units}
            if lev < n_lev - 1:
                xx = {u: dot(xx[u].astype(BF16), by_head(xx[u], first_c)) for u in units}
            fill()
        ys = {u: q0s[u][C:] + dot(jnp.concatenate([mb[u], mk[u]], axis=1),
                                  jnp.concatenate([by_head(us[u]), vsw[u]], axis=0)) for u in units}
        for u in units:
            n, pr = u
            sl = sls[pr]
            uv_t = jnp.concatenate([us[u], qs[n]['v'][:, sl]], axis=0).T.astype(BF16)
            bk_h = jnp.concatenate([qs[n]['b_h'][:, sl], qs[n]['k_h'][:, sl]], axis=0).astype(BF16)
            st_ref[gis[n], pr] = zs[u] * qs[n]['w_end'][:, sl] + blk_mask * dot(uv_t, bk_h)
        for f in fillers:
            for _ in f:
                pass
        return [jnp.concatenate([ys[n, pr] for pr in pairs], axis=1) for n in range(len(gis))]

    def run_all(gens):
        for gen in gens:
            for _ in gen:
                pass

    def chain(gens):
        for gen in gens:
            yield from gen

    groups = [list(range(s0, min(s0 + together, rows_per_step))) for s0 in range(0, rows_per_step, together)]
    prepared = [dict() for _ in groups[0]]
    run_all([prepare(gi, prepared[n]) for n, gi in enumerate(groups[0])])
    finishing = []
    for gidx, gis in enumerate(groups):
        following, fillers = [], list(finishing)
        if gidx + 1 < len(groups):
            following = [dict() for _ in groups[gidx + 1]]
            fillers.append(chain([prepare(gi, following[n]) for n, gi in enumerate(groups[gidx + 1])]))
        ys = scan_chunk(gis, prepared, fillers)
        finishing = [chain([finish(gi, prepared[n], ys[n]) for n, gi in enumerate(gis)])]
        prepared = following
    run_all(finishing)

    @pl.when(c == pl.num_programs(1) - 1)
    def _():
        for gi in range(rows_per_step):
            for pr in range(RWKV_HEADS // 2):
                z = st_ref[gi, pr]
                sfin_ref[gi, 2 * pr] = z[:RWKV_HEAD, :RWKV_HEAD]
                sfin_ref[gi, 2 * pr + 1] = z[RWKV_HEAD:, RWKV_HEAD:]


def _rwkv(p3, prev0, s0z, mu_pad, par, w2p, a2p, g2p, *, chunk, n_valid, rows_per_step):
    b, t, _ = p3.shape
    n_pairs = D_MODEL // LANES
    g = rows_per_step
    assert b % g == 0
    full = lambda shape: pl.BlockSpec(shape, lambda i, c: (0,) * len(shape))
    return pl.pallas_call(
        functools.partial(_rwkv_kernel, chunk=chunk, n_valid=n_valid, together=RWKV_ROWS_TOGETHER),
        grid=(b // g, t // chunk),
        in_specs=[
            pl.BlockSpec((g, chunk, RW_PAD), lambda i, c: (i, c, 0)),
            pl.BlockSpec((g, 1, RW_PAD), lambda i, c: (i, 0, 0)),
            pl.BlockSpec((g, RWKV_HEADS, RWKV_HEAD, RWKV_HEAD), lambda i, c: (i, 0, 0, 0)),
            full((1, RW_PAD)),
            full((SUBLANES, D_MODEL)),
            full((LANES, D_MODEL)),
            full((LANES, D_MODEL)),
            full((2 * LANES, D_MODEL)),
        ],
        out_specs=[
            pl.BlockSpec((g, chunk, D_MODEL), lambda i, c: (i, c, 0)),
            pl.BlockSpec((g, RWKV_HEADS, RWKV_HEAD, RWKV_HEAD), lambda i, c: (i, 0, 0, 0)),
        ],
        out_shape=[
            jax.ShapeDtypeStruct((b, t, D_MODEL), BF16),
            jax.ShapeDtypeStruct((b, RWKV_HEADS, RWKV_HEAD, RWKV_HEAD), F32),
        ],
        scratch_shapes=[pltpu.VMEM((g, 1, RW_PAD), F32), pltpu.VMEM((g, n_pairs, LANES, LANES), F32)],
        compiler_params=_cparams(("parallel", "arbitrary")),
        name="rwkv",
    )(p3, prev0, s0z, mu_pad, par, w2p, a2p, g2p)


def _attn_kernel(*refs, blk, dil, subs, has_prev):
    if has_prev:
        q_ref, ko_ref, vo_ref, kp_ref, vp_ref, o_ref, l_ref = refs
    else:
        q_ref, ko_ref, vo_ref, o_ref, l_ref = refs
    n = pl.program_id(1)
    unit = dil * blk
    scale = ATT_HEAD ** -0.5
    head = pl.program_id(2)
    my_lanes = (lax.broadcasted_iota(jnp.int32, (1, LANES), 1) >> (LSE_LANES.bit_length() - 1)) == head

    @pl.when(head == 0)
    def _():
        l_ref[...] = jnp.zeros(l_ref.shape, F32)

    def class_rows(j, r):
        return pl.ds(j * unit + r, blk, stride=dil) if dil > 1 else pl.ds(j * unit, blk)

    def window_mask(j):
        two = has_prev or j > 0
        nk = 2 * blk if two else blk
        qi = lax.broadcasted_iota(jnp.int32, (blk, nk), 0)
        ki = lax.broadcasted_iota(jnp.int32, (blk, nk), 1)
        if not two:
            return ki <= qi
        valid = (ki >= qi) & (ki <= qi + blk)
        if j == 0:
            valid = valid & (ki >= jnp.where(n == 0, blk, 0))
        return valid

    def keys_values(j, r):
        rows = class_rows(j, r)
        kc = ko_ref[0, rows, :]
        vc = vo_ref[0, rows, :]
        if j > 0:
            kc = jnp.concatenate([ko_ref[0, class_rows(j - 1, r), :], kc], axis=0)
            vc = jnp.concatenate([vo_ref[0, class_rows(j - 1, r), :], vc], axis=0)
        elif has_prev:
            kc = jnp.concatenate([kp_ref[0, class_rows(0, r), :], kc], axis=0)
            vc = jnp.concatenate([vp_ref[0, class_rows(0, r), :], vc], axis=0)
        return kc.astype(BF16), vc.astype(BF16)

    windows = [(j, r) for j in range(subs) for r in range(dil)]
    for w0 in range(0, len(windows), ATTN_WINDOWS_TOGETHER):
        group = windows[w0:w0 + ATTN_WINDOWS_TOGETHER]
        kvs = [keys_values(j, r) for j, r in group]
        scores = [_nt_dot(q_ref[0, class_rows(j, r), :].astype(BF16), kvs[i][0]) * scale
                  for i, (j, r) in enumerate(group)]
        probs, dens, lses = [], [], []
        for i, (j, r) in enumerate(group):
            s = jnp.where(window_mask(j), scores[i], -jnp.inf)
            m = jnp.max(s, axis=-1, keepdims=True)
            pexp = jnp.exp(s - m)
            den = jnp.sum(pexp, axis=-1, keepdims=True)
            probs.append(pexp.astype(BF16))
            dens.append(den)
            lses.append(m + jnp.log(den))
        for i, (j, r) in enumerate(group):
            rows = class_rows(j, r)
            o = jnp.dot(probs[i], kvs[i][1], preferred_element_type=F32) / dens[i]
            o_ref[0, rows, :] = o.astype(o_ref.dtype)
            l_ref[0, rows, :] = jnp.where(my_lanes, lses[i], l_ref[0, rows, :])


ATTN_STEP_ROWS = 1024
ATTN_WINDOWS_TOGETHER = 8


def _attn_prompt(p3, g, dil, blk):
    b, t, _ = p3.shape
    unit = dil * blk
    assert t % unit == 0
    subs = max(1, min(ATTN_STEP_ROWS, t) // unit)
    span = unit * subs
    assert t % span == 0
    nb = t // span
    has_prev = nb > 1
    q_col = (T_Q * PROJ_TN + g * ATT_OUT) // ATT_HEAD
    k_col = (T_KV * PROJ_TN + g * 2 * ATT_OUT) // ATT_HEAD
    v_col = k_col + ATT_HPG
    cur = lambda col: pl.BlockSpec((1, span, ATT_HEAD), lambda bi, n, h: (bi, n, col + h))
    prev = lambda col: pl.BlockSpec((1, unit, ATT_HEAD),
                                    lambda bi, n, h: (bi, jnp.maximum(n * subs - 1, 0), col + h))
    in_specs = [cur(q_col), cur(k_col), cur(v_col)]
    if has_prev:
        in_specs += [prev(k_col), prev(v_col)]
    out_sd = jax.ShapeDtypeStruct((b, t, ATT_OUT), F32)
    out_spec = pl.BlockSpec((1, span, ATT_HEAD), lambda bi, n, h: (bi, n, h))
    o, l = pl.pallas_call(
        functools.partial(_attn_kernel, blk=blk, dil=dil, subs=subs, has_prev=has_prev),
        grid=(b, nb, ATT_HPG),
        in_specs=in_specs,
        out_specs=[out_spec, pl.BlockSpec((1, span, LANES), lambda bi, n, h: (bi, n, 0))],
        out_shape=[out_sd, jax.ShapeDtypeStruct((b, t, LANES), F32)],
        compiler_params=_cparams(("parallel", "parallel", "arbitrary")),
        name=f"attn_prompt_g{g}",
    )(*([p3] * len(in_specs)))
    return o.reshape(b * t, ATT_OUT), l.reshape(b * t, LANES)


def _kv_rows_kernel(x_ref, o_ref):
    for s in range(2):
        for h in range(ATT_HPG):
            c0 = (s * ATT_HPG + h) * ATT_HEAD
            o_ref[0, :, s, h, :] = x_ref[0, :, c0:c0 + ATT_HEAD]


def _kv_rows(p3, g, keep, tr):
    b, t, _ = p3.shape
    col = (T_KV * PROJ_TN) // (2 * ATT_OUT) + g
    r0 = (t - keep) // tr
    return pl.pallas_call(
        _kv_rows_kernel,
        grid=(b, keep // tr),
        in_specs=[pl.BlockSpec((1, tr, 2 * ATT_OUT), lambda bi, i: (bi, r0 + i, col))],
        out_specs=pl.BlockSpec((1, tr, 2, ATT_HPG, ATT_HEAD), lambda bi, i: (bi, i, 0, 0, 0)),
        out_shape=jax.ShapeDtypeStruct((b, keep, 2, ATT_HPG, ATT_HEAD), F32),
        compiler_params=_cparams(("parallel", "parallel")),
        name=f"kv_rows_g{g}",
    )(p3)


def _attn_sample_kernel(q_ref, n0_ref, n1_ref, n2_ref, c0_ref, c1_ref, c2_ref, o_ref, l_ref, *, dils, ts):
    n_refs = (n0_ref, n1_ref, n2_ref)
    c_refs = (c0_ref, c1_ref, c2_ref)
    scale = ATT_HEAD ** -0.5
    for g in range(N_GROUPS):
        dil = dils[g]
        new = n_refs[g][0]
        kn, vn = new[:, 0], new[:, 1]
        rows = c_refs[g].shape[2] // dil
        wi = lax.broadcasted_iota(jnp.int32, (rows, ATT_HPG, 1), 0)
        ni = lax.broadcasted_iota(jnp.int32, (ts, ATT_HPG, 1), 0)
        for t in range(ts):
            cls = pl.ds(t % dil, rows, stride=dil) if dil > 1 else pl.ds(0, rows)
            cache = c_refs[g][0, 0, cls]
            kc, vc = cache[:, 0], cache[:, 1]
            q = q_ref[0, t, g]
            sc = jnp.sum(kc * q[None], axis=-1, keepdims=True) * scale
            sn = jnp.sum(kn * q[None], axis=-1, keepdims=True) * scale
            c_ok = wi >= (t // dil)
            n_ok = functools.reduce(jnp.logical_or, [ni == tn for tn in range(t + 1) if (t - tn) % dil == 0])
            sc = jnp.where(c_ok, sc, -jnp.inf)
            sn = jnp.where(n_ok, sn, -jnp.inf)
            m = jnp.maximum(jnp.max(sc, axis=0, keepdims=True), jnp.max(sn, axis=0, keepdims=True))
            pc = jnp.exp(sc - m)
            pn = jnp.exp(sn - m)
            den = jnp.sum(pc, axis=0, keepdims=True) + jnp.sum(pn, axis=0, keepdims=True)
            o = (jnp.sum(pc * vc, axis=0, keepdims=True) + jnp.sum(pn * vn, axis=0, keepdims=True)) / den
            o_ref[0, t, g] = o[0]
            l_ref[0, t, g] = jnp.broadcast_to((m + jnp.log(den))[0], (ATT_HPG, ATT_HEAD))


def _attn_sample(q5, new_kv, caches):
    b, ts = q5.shape[:2]
    dils = tuple(d for _, d in DILATION_GROUPS)
    in_specs = [pl.BlockSpec((1, ts, N_GROUPS, ATT_HPG, ATT_HEAD), lambda bi: (bi, 0, 0, 0, 0))]
    for g in range(N_GROUPS):
        in_specs.append(pl.BlockSpec((1, ts, 2, ATT_HPG, ATT_HEAD), lambda bi: (bi, 0, 0, 0, 0)))
    for g in range(N_GROUPS):
        w = caches[g].shape[2]
        in_specs.append(pl.BlockSpec((1, 1, w, 2, ATT_HPG, ATT_HEAD), lambda bi: (0, bi, 0, 0, 0, 0)))
    out_sd = jax.ShapeDtypeStruct((b, ts, N_GROUPS, ATT_HPG, ATT_HEAD), F32)
    out_spec = pl.BlockSpec((1, ts, N_GROUPS, ATT_HPG, ATT_HEAD), lambda bi: (bi, 0, 0, 0, 0))
    o, l = pl.pallas_call(
        functools.partial(_attn_sample_kernel, dils=dils, ts=ts),
        grid=(b,),
        in_specs=in_specs,
        out_specs=[out_spec, out_spec],
        out_shape=[out_sd, out_sd],
        compiler_params=_cparams(("parallel",)),
        name="attn_sample",
    )(q5, *new_kv, *caches)
    o_list = [o[:, :, g].reshape(b * ts, ATT_OUT) for g in range(N_GROUPS)]
    l_list = [l[:, :, g, :, :LSE_LANES].reshape(b * ts, LANES) for g in range(N_GROUPS)]
    return o_list, l_list


ROLL_ROWS = 64
ROLL_BLOCK_BYTES = 8 << 20


def _roll_kernel(c_ref, n_ref, o_ref):
    w, ts = c_ref.shape[2], n_ref.shape[2]
    body_rows = w - ROLL_ROWS

    def move(i, carry):
        r = pl.multiple_of(i * ROLL_ROWS, ROLL_ROWS)
        o_ref[0, :, pl.ds(r, ROLL_ROWS)] = c_ref[0, :, pl.ds(r + ts, ROLL_ROWS)]
        return carry

    lax.fori_loop(0, body_rows // ROLL_ROWS, move, 0)
    o_ref[0, :, body_rows:w - ts] = c_ref[0, :, body_rows + ts:w]
    o_ref[0, :, w - ts:w] = n_ref[0]


def _cache_roll(cache, new_rows):
    _, b, w = cache.shape[:3]
    ts = new_rows.shape[2]
    assert w % ROLL_ROWS == 0 and ts < ROLL_ROWS
    row_bytes = 2 * ATT_OUT * 4
    bb = max(1, min(b, ROLL_BLOCK_BYTES // (w * row_bytes)))
    assert b % bb == 0
    tail = (2, ATT_HPG, ATT_HEAD)
    return pl.pallas_call(
        _roll_kernel,
        grid=(b // bb,),
        in_specs=[pl.BlockSpec((1, bb, w) + tail, lambda i: (0, i, 0, 0, 0, 0)),
                  pl.BlockSpec((1, bb, ts) + tail, lambda i: (0, i, 0, 0, 0, 0))],
        out_specs=pl.BlockSpec((1, bb, w) + tail, lambda i: (0, i, 0, 0, 0, 0)),
        out_shape=jax.ShapeDtypeStruct(cache.shape, cache.dtype),
        compiler_params=_cparams(("parallel",)),
        name="cache_roll",
    )(cache, new_rows)


def _merge_kernel(x_ref, yrw_ref, gate_ref, o0_ref, o1_ref, o2_ref, l0_ref, l1_ref, l2_ref,
                  wba_ref, wbb_ref, wo_ref, fnw_ref, rw_ref, rb_ref,
                  h_ref, hn_ref, ti_ref, tg_ref, cnt_ref, run_ref):
    @pl.when(pl.program_id(0) == 0)
    def _():
        run_ref[...] = jnp.zeros(run_ref.shape, F32)

    rows_here = x_ref.shape[0]
    heads_out = []
    for hd in range(ATT_HPG):
        hs = slice(hd * ATT_HEAD, (hd + 1) * ATT_HEAD)
        lses = [jnp.broadcast_to(l_ref[:, hd * LSE_LANES:hd * LSE_LANES + 1], (rows_here, ATT_HEAD))
                for l_ref in (l0_ref, l1_ref, l2_ref)]
        m = jnp.maximum(jnp.maximum(lses[0], lses[1]), lses[2])
        e0, e1, e2 = jnp.exp(lses[0] - m), jnp.exp(lses[1] - m), jnp.exp(lses[2] - m)
        heads_out.append((e0 * o0_ref[:, hs] + e1 * o1_ref[:, hs] + e2 * o2_ref[:, hs]) / (e0 + e1 + e2))
    o_att = jnp.concatenate(heads_out, axis=1)
    br_a = jnp.dot(yrw_ref[...], wba_ref[...], preferred_element_type=F32)
    br_b = jnp.dot(o_att.astype(BF16), wbb_ref[...], preferred_element_type=F32)
    merged = gate_ref[:, 0:D_MODEL] * br_a + gate_ref[:, D_MODEL:2 * D_MODEL] * br_b
    h = x_ref[...] + jnp.dot(merged.astype(BF16), wo_ref[...], preferred_element_type=F32)
    h_ref[...] = h
    hn = h * lax.rsqrt(jnp.mean(h * h, axis=-1, keepdims=True) + NORM_EPS) * fnw_ref[...]
    hn_ref[...] = hn
    logits = jnp.dot(hn.astype(BF16), rw_ref[...], preferred_element_type=F32) + rb_ref[...]
    lane_i = lax.broadcasted_iota(jnp.int32, logits.shape, 1)
    lane = lane_i.astype(F32)
    vals, idxs = [], []
    cur = logits
    for _ in range(TOP_K):
        mx = jnp.max(cur, axis=-1, keepdims=True)
        ix = jnp.min(jnp.where(cur == mx, lane, float(LANES)), axis=-1, keepdims=True)
        vals.append(mx)
        idxs.append(ix)
        cur = jnp.where(lane == ix, -jnp.inf, cur)
    es = [jnp.exp(vv - vals[0]) for vv in vals]
    tot = es[0] + es[1] + es[2] + es[3]
    tm = logits.shape[0]
    ri = lax.broadcasted_iota(jnp.int32, (tm, tm), 0)
    rj = lax.broadcasted_iota(jnp.int32, (tm, tm), 1)
    before = jnp.where(ri > rj, 1.0, 0.0).astype(BF16)
    run = run_ref[0:1, :]
    ranks = []
    for kq in range(TOP_K):
        onehot = jnp.where(lane == idxs[kq], 1.0, 0.0)
        prior = jnp.dot(before, onehot.astype(BF16), preferred_element_type=F32) + run
        ranks.append(jnp.sum(onehot * prior, axis=-1, keepdims=True))
        run = run + jnp.sum(onehot, axis=0, keepdims=True)
    run_ref[...] = jnp.broadcast_to(run, run_ref.shape)
    cnt_ref[...] = jnp.broadcast_to(run, cnt_ref.shape)

    ti = jnp.zeros(logits.shape, F32)
    tg = jnp.zeros(logits.shape, F32)
    for kq in range(TOP_K):
        ti = jnp.where(lane_i == kq, idxs[kq], ti)
        ti = jnp.where(lane_i == TOP_K + kq, ranks[kq], ti)
        tg = jnp.where(lane_i == kq, es[kq] / tot, tg)
    ti_ref[...] = ti.astype(jnp.int32)
    tg_ref[...] = tg


LSE_LANES = LANES // ATT_HPG


def _merge(x2d, yrw, gates, o_list, l_list, wba, wbb, wo, fnw, rw_pad, rb_pad, *, tm):
    n = x2d.shape[0]
    row = lambda wdt: pl.BlockSpec((tm, wdt), lambda i: (i, 0))
    full = lambda a: pl.BlockSpec(a.shape, lambda i: (0,) * a.ndim)
    return pl.pallas_call(
        _merge_kernel,
        grid=(n // tm,),
        in_specs=[row(D_MODEL), row(D_MODEL), row(2 * D_MODEL),
                  row(ATT_OUT), row(ATT_OUT), row(ATT_OUT), row(LANES), row(LANES), row(LANES),
                  full(wba), full(wbb), full(wo), full(fnw), full(rw_pad), full(rb_pad)],
        out_specs=[row(D_MODEL), row(D_MODEL), row(LANES), row(LANES),
                   pl.BlockSpec((SUBLANES, LANES), lambda i: (0, 0))],
        out_shape=[jax.ShapeDtypeStruct((n, D_MODEL), F32), jax.ShapeDtypeStruct((n, D_MODEL), F32),
                   jax.ShapeDtypeStruct((n, LANES), jnp.int32), jax.ShapeDtypeStruct((n, LANES), F32),
                   jax.ShapeDtypeStruct((SUBLANES, LANES), F32)],
        scratch_shapes=[pltpu.VMEM((SUBLANES, LANES), F32)],
        compiler_params=_cparams(("arbitrary",)),
        name="merge",
    )(x2d, yrw, gates, *o_list, *l_list, wba, wbb, wo, fnw, rw_pad, rb_pad)


def _row_copy(src_ref, s_row, dst_ref, d_row, sem):
    return pltpu.make_async_copy(src_ref.at[pl.ds(s_row, 1), :], dst_ref.at[pl.ds(d_row, 1), :], sem)


def _dispatch_kernel(pad_start_ref, pad_len_ref, pos_ref, hn_ref, xs_ref, zrow_ref, sem, *, tt):
    i = pl.program_id(0)

    @pl.when(i == 0)
    def _():
        zrow_ref[...] = jnp.zeros(zrow_ref.shape, F32)

        def per_expert(e, carry):
            s = pad_start_ref[e]
            cnt = pad_len_ref[e]

            def issue(rr, cc):
                _row_copy(zrow_ref, 0, xs_ref, s + rr, sem).start()
                return cc

            def drain(rr, cc):
                _row_copy(zrow_ref, 0, xs_ref, s + rr, sem).wait()
                return cc

            lax.fori_loop(0, cnt, issue, 0)
            lax.fori_loop(0, cnt, drain, 0)
            return carry

        lax.fori_loop(0, N_EXPERTS, per_expert, 0)

    def issue(nn, cc):
        for kq in range(TOP_K):
            _row_copy(hn_ref, nn, xs_ref, pos_ref[0, 0, nn * TOP_K + kq], sem).start(priority=kq % 2)
        return cc

    def drain(nn, cc):
        for kq in range(TOP_K):
            _row_copy(hn_ref, nn, xs_ref, pos_ref[0, 0, nn * TOP_K + kq], sem).wait()
        return cc

    lax.fori_loop(0, tt, issue, 0)
    lax.fori_loop(0, tt, drain, 0)


def _dispatch(hn, pos3, pad_start, pad_len, rows, *, tt):
    n = hn.shape[0]
    return pl.pallas_call(
        functools.partial(_dispatch_kernel, tt=tt),
        grid_spec=pltpu.PrefetchScalarGridSpec(
            num_scalar_prefetch=2,
            grid=(n // tt,),
            in_specs=[
                pl.BlockSpec((1, 1, tt * TOP_K), lambda i, ps, plen: (i, 0, 0), memory_space=pltpu.SMEM),
                pl.BlockSpec((tt, D_MODEL), lambda i, ps, plen: (i, 0)),
            ],
            out_specs=pl.BlockSpec(memory_space=pl.ANY),
            scratch_shapes=[pltpu.VMEM((SUBLANES, D_MODEL), F32), pltpu.SemaphoreType.DMA(())],
        ),
        out_shape=jax.ShapeDtypeStruct((rows, D_MODEL), F32),
        compiler_params=_cparams(("arbitrary",)),
        name="moe_dispatch",
    )(pad_start, pad_len, pos3, hn)


def _moe_kernel(be_ref, nu_ref, xs_ref, w1_ref, b1_ref, w2_ref, b2_ref, y_ref):
    i = pl.program_id(0)

    @pl.when(i < nu_ref[0])
    def _():
        x = xs_ref[...].astype(BF16)
        hdn = jnp.dot(x, w1_ref[0], preferred_element_type=F32) + b1_ref[0]
        glu = jnp.minimum(hdn[:, 0:D_MODEL], SWIGLU_LIMIT)
        lin = jnp.clip(hdn[:, D_MODEL:2 * D_MODEL], -SWIGLU_LIMIT, SWIGLU_LIMIT)
        act = glu * jax.nn.sigmoid(SWIGLU_ALPHA * glu) * (lin + 1.0)
        y_ref[...] = jnp.dot(act.astype(BF16), w2_ref[0], preferred_element_type=F32) + b2_ref[0]

    @pl.when(i >= nu_ref[0])
    def _():
        y_ref[...] = jnp.zeros(y_ref.shape, F32)


def _moe(xs, blk_expert, n_used, w1, b1, w2, b2, *, tm):
    rows = xs.shape[0]
    nb = rows // tm
    return pl.pallas_call(
        _moe_kernel,
        grid_spec=pltpu.PrefetchScalarGridSpec(
            num_scalar_prefetch=2,
            grid=(nb,),
            in_specs=[
                pl.BlockSpec((tm, D_MODEL), lambda i, be, nu: (jnp.minimum(i, nu[0] - 1), 0)),
                pl.BlockSpec((1, D_MODEL, 2 * D_MODEL), lambda i, be, nu: (be[i], 0, 0)),
                pl.BlockSpec((1, 1, 2 * D_MODEL), lambda i, be, nu: (be[i], 0, 0)),
                pl.BlockSpec((1, D_MODEL, D_MODEL), lambda i, be, nu: (be[i], 0, 0)),
                pl.BlockSpec((1, 1, D_MODEL), lambda i, be, nu: (be[i], 0, 0)),
            ],
            out_specs=pl.BlockSpec((tm, D_MODEL), lambda i, be, nu: (i, 0)),
        ),
        out_shape=jax.ShapeDtypeStruct((rows, D_MODEL), F32),
        compiler_params=_cparams(("arbitrary",)),
        name="moe_experts",
    )(blk_expert, n_used, xs, w1, b1, w2, b2)


def _combine_kernel(pos_ref, h_ref, tg_ref, yb_ref, o_ref, buf_ref, sem, *, tt):
    def issue(nn, cc):
        for kq in range(TOP_K):
            _row_copy(yb_ref, pos_ref[0, 0, nn * TOP_K + kq], buf_ref.at[kq], nn, sem).start(priority=kq % 2)
        return cc

    def drain(nn, cc):
        for kq in range(TOP_K):
            _row_copy(yb_ref, pos_ref[0, 0, nn * TOP_K + kq], buf_ref.at[kq], nn, sem).wait()
        return cc

    lax.fori_loop(0, tt, issue, 0)
    lax.fori_loop(0, tt, drain, 0)
    acc = h_ref[...]
    for kq in range(TOP_K):
        acc = acc + tg_ref[:, kq:kq + 1] * buf_ref[kq]
    o_ref[...] = acc


def _combine(h, tg, pos3, yb, *, tt):
    n = h.shape[0]
    return pl.pallas_call(
        functools.partial(_combine_kernel, tt=tt),
        grid=(n // tt,),
        in_specs=[
            pl.BlockSpec((1, 1, tt * TOP_K), lambda i: (i, 0, 0), memory_space=pltpu.SMEM),
            pl.BlockSpec((tt, D_MODEL), lambda i: (i, 0)),
            pl.BlockSpec((tt, LANES), lambda i: (i, 0)),
            pl.BlockSpec(memory_space=pl.ANY),
        ],
        out_specs=pl.BlockSpec((tt, D_MODEL), lambda i: (i, 0)),
        out_shape=jax.ShapeDtypeStruct((n, D_MODEL), F32),
        scratch_shapes=[pltpu.VMEM((TOP_K, tt, D_MODEL), F32), pltpu.SemaphoreType.DMA(())],
        compiler_params=_cparams(("arbitrary",)),
        name="moe_combine",
    )(pos3, h, tg, yb)


def _routing(top_idx, rank, counts, tm):
    n = top_idx.shape[0]
    nk = n * TOP_K
    experts = jnp.arange(N_EXPERTS, dtype=jnp.int32)
    blocks_e = (counts + tm - 1) // tm
    upper = (experts[:, None] <= experts[None, :]).astype(jnp.int32)
    block_end = jnp.sum(blocks_e[:, None] * upper, axis=0)
    row_start = (block_end - blocks_e) * tm
    onehot = (top_idx[:, :, None] == experts[None, None, :]).astype(jnp.int32)
    pos = (jnp.sum(onehot * row_start[None, None, :], axis=-1) + rank).reshape(nk).astype(jnp.int32)
    n_blocks = -(-nk // tm) + N_EXPERTS
    blk_ids = jnp.arange(n_blocks, dtype=jnp.int32)
    blk_expert = jnp.minimum(
        jnp.sum((block_end[None, :] <= blk_ids[:, None]).astype(jnp.int32), axis=1), N_EXPERTS - 1
    ).astype(jnp.int32)
    n_used = block_end[-1:].astype(jnp.int32)
    pad_start = (row_start + counts).astype(jnp.int32)
    pad_len = (blocks_e * tm - counts).astype(jnp.int32)
    return pos, blk_expert, n_used, pad_start, pad_len, n_blocks * tm


def _pad_cols(a, width):
    return jnp.pad(a, ((0, 0), (0, width - a.shape[1])))


def _pad_rows(a, height):
    return jnp.pad(a, ((0, height - a.shape[0]), (0, 0)))


def _proj_columns(a):
    d = D_MODEL
    xw = a[:, 3 * d:3 * d + DECAY_LORA]
    xa = a[:, 3 * d + DECAY_LORA:3 * d + DECAY_LORA + AAA_LORA]
    xg = a[:, 3 * d + DECAY_LORA + AAA_LORA:C_RW]
    att = a[:, C_RW:C_RW + 3 * ATT_DIM]
    q, k, v = att[:, :ATT_DIM], att[:, ATT_DIM:2 * ATT_DIM], att[:, 2 * ATT_DIM:]
    parts = [a[:, :3 * d], _pad_cols(xw, LANES), _pad_cols(xa, LANES), _pad_cols(xg, 2 * LANES), q]
    for g in range(N_GROUPS):
        parts.append(k[:, g * ATT_OUT:(g + 1) * ATT_OUT])
        parts.append(v[:, g * ATT_OUT:(g + 1) * ATT_OUT])
    parts.append(a[:, C_RW + 3 * ATT_DIM:])
    return jnp.concatenate(parts, axis=1)


def _rope_tables(pos):
    inv = ROPE_THETA ** (-jnp.arange(0, ATT_HEAD, 2, dtype=F32) / ATT_HEAD)
    ang = pos.astype(F32)[:, None] * inv[None, :]
    cos, sin = jnp.cos(ang), jnp.sin(ang)
    return jnp.concatenate([cos, cos], axis=1), jnp.concatenate([-sin, sin], axis=1)


def _prep_weights(attn_norm_w, w_in, rw_mu, rw_w0, rw_w2, rw_a0, rw_a2, rw_g2, rw_kk, rw_ka, rw_rk,
                  rw_lnx_w, rw_lnx_b, q_norm_w, k_norm_w, w_br, w_out, ffn_norm_w, router_w, router_b,
                  moe_w1, moe_b1, moe_w2, moe_b2):
    w = {}
    w['nw'] = attn_norm_w.reshape(1, D_MODEL)
    w['w_pad'] = _proj_columns(w_in).astype(BF16)
    w['mu_pad'] = _proj_columns(_pad_cols(rw_mu.reshape(1, C_RW), C_RW + 3 * ATT_DIM + 2 * D_MODEL))[:, :RW_PAD]
    w['par'] = jnp.concatenate([rw_w0.reshape(1, -1), rw_a0.reshape(1, -1), rw_kk.reshape(1, -1),
                                rw_ka.reshape(1, -1), rw_rk.reshape(1, -1), rw_lnx_w.reshape(1, -1),
                                rw_lnx_b.reshape(1, -1), jnp.zeros((1, D_MODEL), F32)], axis=0)
    w['w2p'] = _pad_rows(rw_w2, LANES).astype(BF16)
    w['a2p'] = _pad_rows(rw_a2, LANES).astype(BF16)
    w['g2p'] = _pad_rows(rw_g2, 2 * LANES).astype(BF16)
    w['qn'] = q_norm_w.reshape(1, ATT_HEAD)
    w['kn'] = k_norm_w.reshape(1, ATT_HEAD)
    w['wba'] = w_br[:D_MODEL].astype(BF16)
    w['wbb'] = w_br[D_MODEL:].astype(BF16)
    w['wo'] = w_out.astype(BF16)
    w['fnw'] = ffn_norm_w.reshape(1, D_MODEL)
    w['rw_pad'] = _pad_cols(router_w, LANES).astype(BF16)
    w['rb_pad'] = jnp.concatenate([router_b.reshape(1, N_EXPERTS),
                                   jnp.full((1, LANES - N_EXPERTS), -1e30, F32)], axis=1)
    cols = jnp.arange(2 * D_MODEL, dtype=jnp.int32)
    src = jnp.where(cols < D_MODEL, 2 * cols, 2 * (cols - D_MODEL) + 1)
    perm = (jnp.arange(2 * D_MODEL, dtype=jnp.int32)[:, None] == src[None, :]).astype(BF16)
    w['w1'] = jnp.einsum('edh,hk->edk', moe_w1.astype(BF16), perm, preferred_element_type=BF16)
    w['b1'] = jnp.concatenate([moe_b1[:, 0::2], moe_b1[:, 1::2]], axis=1).reshape(N_EXPERTS, 1, 2 * D_MODEL)
    w['w2'] = moe_w2.astype(BF16)
    w['b2'] = moe_b2.reshape(N_EXPERTS, 1, D_MODEL)
    return w


def _layer(x, pos, shift_prev, wkv0, caches, w, *, proj_tm, merge_tm, moe_tm, tok_tt):
    b, t, _ = x.shape
    n = b * t
    x2d = x.reshape(n, D_MODEL)
    is_prompt = caches is None
    rw_rows = next(g for g in (RWKV_ROWS_PER_STEP, RWKV_ROWS_TOGETHER, 1) if b % g == 0)

    cs_tab, sn_tab = _rope_tables(pos)
    if not is_prompt:
        cs_tab = jnp.tile(cs_tab, (proj_tm // t, 1))
        sn_tab = jnp.tile(sn_tab, (proj_tm // t, 1))
    p2d, gates = _proj(x2d, w['nw'], w['w_pad'], w['qn'], w['kn'], cs_tab, sn_tab,
                       tm=proj_tm, n_tiles=N_TILES, do_norm=True)
    p3 = p2d.reshape(b, t, p2d.shape[1])
    shift_out = _rmsnorm_rows(x[:, t - 1, :], w['nw'])

    if is_prompt:
        prev0 = jnp.zeros((b, 1, RW_PAD), F32)
        s0z = jnp.zeros((b, RWKV_HEADS, RWKV_HEAD, RWKV_HEAD), F32)
        yrw, s_fin = _rwkv(p3, prev0, s0z, w['mu_pad'], w['par'], w['w2p'], w['a2p'], w['g2p'],
                          chunk=SCAN_CHUNK, n_valid=None, rows_per_step=rw_rows)
        yrw = yrw.reshape(n, D_MODEL)
    else:
        one = jnp.ones((1, ATT_HEAD), F32)
        zero_tab = jnp.zeros((b, ATT_HEAD), F32)
        prev_rw, = _proj(shift_prev, w['nw'], w['w_pad'], one, one, zero_tab, zero_tab,
                         tm=b, n_tiles=T_LORA + 1, do_norm=False)
        t_pad = -(-t // SHORT_SCAN_CHUNK) * SHORT_SCAN_CHUNK
        p_rw = jnp.pad(p3[:, :, :RW_PAD], ((0, 0), (0, t_pad - t), (0, 0)))
        yrw, s_fin = _rwkv(p_rw, prev_rw.reshape(b, 1, RW_PAD), wkv0, w['mu_pad'], w['par'],
                          w['w2p'], w['a2p'], w['g2p'], chunk=SHORT_SCAN_CHUNK, n_valid=t, rows_per_step=rw_rows)
        yrw = yrw[:, :t].reshape(n, D_MODEL)

    new_kv = []
    if is_prompt:
        o_list, l_list = [], []
        for g, (window, dil) in enumerate(DILATION_GROUPS):
            o_g, l_g = _attn_prompt(p3, g, dil, window // dil)
            o_list.append(o_g)
            l_list.append(l_g)
            keep = min(window, t)
            new_kv.append(_kv_rows(p3, g, keep, min(keep, 256))[None])
    else:
        kv_col = T_KV * PROJ_TN
        q5 = p3[:, :, T_Q * PROJ_TN:T_KV * PROJ_TN].reshape(b, t, N_GROUPS, ATT_HPG, ATT_HEAD)
        kv_new = [p3[:, :, kv_col + g * 2 * ATT_OUT:kv_col + (g + 1) * 2 * ATT_OUT].reshape(b, t, 2, ATT_HPG, ATT_HEAD)
                  for g in range(N_GROUPS)]
        o_list, l_list = _attn_sample(q5, kv_new, caches)
        new_kv = [_cache_roll(caches[g], kv_new[g][None]) for g in range(N_GROUPS)]

    h, hn, ti, tg, cnt = _merge(x2d, yrw, gates, o_list, l_list, w['wba'], w['wbb'], w['wo'], w['fnw'],
                                w['rw_pad'], w['rb_pad'], tm=merge_tm)

    counts = cnt[0, :N_EXPERTS].astype(jnp.int32)
    pos_rows, blk_expert, n_used, pad_start, pad_len, rows = _routing(
        ti[:, :TOP_K], ti[:, TOP_K:2 * TOP_K], counts, moe_tm)
    pos3 = pos_rows.reshape(n // tok_tt, 1, tok_tt * TOP_K)
    xs = _dispatch(hn, pos3, pad_start, pad_len, rows, tt=tok_tt)
    yb = _moe(xs, blk_expert, n_used, w['w1'], w['b1'], w['w2'], w['b2'], tm=moe_tm)
    y = _combine(h, tg, pos3, yb, tt=tok_tt)
    return y.reshape(b, t, D_MODEL), shift_out, s_fin, new_kv


def kernel(x_prompt, x_sample, state_wkv, state_shift, cache_kv_w128, cache_kv_w512, cache_kv_w2048,
           attn_norm_w, w_in, rw_mu, rw_w0, rw_w2, rw_a0, rw_a2, rw_g2, rw_kk, rw_ka, rw_rk,
           rw_lnx_w, rw_lnx_b, q_norm_w, k_norm_w, w_br, w_out, ffn_norm_w, router_w, router_b,
           moe_w1, moe_b1, moe_w2, moe_b2):
    depth = w_in.shape[0]
    assert depth == 1
    bp, tp, _ = x_prompt.shape
    bs, ts, _ = x_sample.shape
    l = 0
    w = _prep_weights(attn_norm_w[l], w_in[l], rw_mu[l], rw_w0[l], rw_w2[l], rw_a0[l], rw_a2[l], rw_g2[l],
                      rw_kk[l], rw_ka[l], rw_rk[l], rw_lnx_w[l], rw_lnx_b[l], q_norm_w[l], k_norm_w[l],
                      w_br[l], w_out[l], ffn_norm_w[l], router_w[l], router_b[l],
                      moe_w1[l], moe_b1[l], moe_w2[l], moe_b2[l])
    pos_p = jnp.arange(tp, dtype=jnp.int32)
    pos_s = PAST_LEN + jnp.arange(ts, dtype=jnp.int32)
    yp, sh_p, wkv_p, kv_p = _layer(x_prompt, pos_p, None, None, None, w,
                                   proj_tm=min(2048, tp), merge_tm=256, moe_tm=MOE_TM, tok_tt=256)
    ns = bs * ts
    ys, sh_s, wkv_s, kv_s = _layer(x_sample, pos_s, state_shift[l], state_wkv[l],
                                   (cache_kv_w128, cache_kv_w512, cache_kv_w2048), w,
                                   proj_tm=ns, merge_tm=min(256, ns), moe_tm=128, tok_tt=min(256, ns))
    return (yp, ys,
            wkv_p[None], sh_p[None], kv_p[0], kv_p[1], kv_p[2],
            wkv_s[None], sh_s[None], kv_s[0], kv_s[1], kv_s[2])
```

```python
import functools

import jax
import jax.numpy as jnp
from jax import lax
from jax.experimental import pallas as pl
from jax.experimental.pallas import tpu as pltpu

F32 = jnp.float32
BF16 = jnp.bfloat16

D_MODEL = 1024
NORM_EPS = 1e-5
RWKV_HEAD = 64
RWKV_HEADS = D_MODEL // RWKV_HEAD
DECAY_LORA = 64
AAA_LORA = 64
GATE_LORA = 160
GN_EPS = 64e-5
C_RW = 3 * D_MODEL + DECAY_LORA + AAA_LORA + GATE_LORA
ATT_HEAD = 128
ATT_HPG = 4
DILATION_GROUPS = ((128, 1), (512, 4), (2048, 16))
N_GROUPS = 3
ATT_DIM = N_GROUPS * ATT_HPG * ATT_HEAD
ATT_OUT = ATT_HPG * ATT_HEAD
ROPE_THETA = 10000.0
N_EXPERTS = 32
TOP_K = 4
SWIGLU_ALPHA = 1.702
SWIGLU_LIMIT = 7.0
PAST_LEN = 8192

LANES = 128
SUBLANES = 8
VMEM_LIMIT = 56 * 1024 * 1024

PROJ_TN = 512
T_RKV = 0
T_LORA = 6
T_Q = 7
T_KV = 10
T_GATE = 16
N_TILES = 20
P_PAD = N_TILES * PROJ_TN
RW_PAD = (T_LORA + 1) * PROJ_TN

SCAN_CHUNK = 64
SHORT_SCAN_CHUNK = 16
RWKV_ROWS_PER_STEP = 4
RWKV_ROWS_TOGETHER = 2
MOE_TM = 512


def _cparams(sem, vmem=VMEM_LIMIT):
    return pltpu.CompilerParams(dimension_semantics=sem, vmem_limit_bytes=vmem)


def _proj_kernel(x_ref, nw_ref, w_ref, qn_ref, kn_ref, cs_ref, sn_ref, *rest, do_norm, with_gates):
    if with_gates:
        o_ref, g_ref, xn_ref = rest
    else:
        o_ref, xn_ref = rest
    j = pl.program_id(1)

    @pl.when(j == 0)
    def _():
        x = x_ref[...]
        if do_norm:
            x = x * lax.rsqrt(jnp.mean(x * x, axis=-1, keepdims=True) + NORM_EPS) * nw_ref[...]
        xn_ref[...] = x.astype(BF16)

    acc = jnp.dot(xn_ref[...], w_ref[...], preferred_element_type=F32)
    is_q = (j >= T_Q) & (j < T_KV)
    is_k = (j >= T_KV) & (j < T_GATE) & ((j - T_KV) % 2 == 0)
    is_qk = is_q | is_k
    is_gate = j >= T_GATE

    @pl.when(is_qk)
    def _():
        nwh = jnp.where(j < T_KV, qn_ref[...], kn_ref[...])
        cs = cs_ref[...]
        sn = sn_ref[...]
        for h in range(ATT_HPG):
            xh = acc[:, h * ATT_HEAD:(h + 1) * ATT_HEAD]
            y = xh * lax.rsqrt(jnp.mean(xh * xh, axis=-1, keepdims=True) + NORM_EPS) * nwh
            o_ref[:, h * ATT_HEAD:(h + 1) * ATT_HEAD] = y * cs + pltpu.roll(y, ATT_HEAD // 2, axis=1) * sn

    if with_gates:
        @pl.when(is_gate)
        def _():
            g_ref[...] = jax.nn.sigmoid(acc).astype(g_ref.dtype)

    @pl.when(jnp.logical_not(is_qk | is_gate))
    def _():
        o_ref[...] = acc


def _proj(x2d, nw, w_pad, qn, kn, cs_tab, sn_tab, *, tm, n_tiles, do_norm):
    n = x2d.shape[0]
    tab_blocks = cs_tab.shape[0] // tm
    with_gates = n_tiles > T_GATE
    p_tiles = min(n_tiles, T_GATE)
    out_specs = [pl.BlockSpec((tm, PROJ_TN), lambda i, j: (i, jnp.minimum(j, p_tiles - 1)))]
    out_shape = [jax.ShapeDtypeStruct((n, p_tiles * PROJ_TN), F32)]
    if with_gates:
        out_specs.append(pl.BlockSpec((tm, PROJ_TN), lambda i, j: (i, jnp.maximum(j - T_GATE, 0))))
        out_shape.append(jax.ShapeDtypeStruct((n, (n_tiles - T_GATE) * PROJ_TN), BF16))
    return pl.pallas_call(
        functools.partial(_proj_kernel, do_norm=do_norm, with_gates=with_gates),
        grid=(n // tm, n_tiles),
        in_specs=[
            pl.BlockSpec((tm, D_MODEL), lambda i, j: (i, 0)),
            pl.BlockSpec((1, D_MODEL), lambda i, j: (0, 0)),
            pl.BlockSpec((D_MODEL, PROJ_TN), lambda i, j: (0, j)),
            pl.BlockSpec((1, ATT_HEAD), lambda i, j: (0, 0)),
            pl.BlockSpec((1, ATT_HEAD), lambda i, j: (0, 0)),
            pl.BlockSpec((tm, ATT_HEAD), lambda i, j: (i % tab_blocks, 0)),
            pl.BlockSpec((tm, ATT_HEAD), lambda i, j: (i % tab_blocks, 0)),
        ],
        out_specs=out_specs,
        out_shape=out_shape,
        scratch_shapes=[pltpu.VMEM((tm, D_MODEL), BF16)],
        compiler_params=_cparams(("parallel", "arbitrary")),
        name="proj",
    )(x2d, nw, w_pad, qn, kn, cs_tab, sn_tab)


def _rmsnorm_kernel(x_ref, w_ref, o_ref):
    x = x_ref[...]
    o_ref[...] = x * lax.rsqrt(jnp.mean(x * x, axis=-1, keepdims=True) + NORM_EPS) * w_ref[...]


def _rmsnorm_rows(x2d, w):
    return pl.pallas_call(
        _rmsnorm_kernel,
        out_shape=jax.ShapeDtypeStruct(x2d.shape, F32),
        name="rmsnorm_rows",
    )(x2d, w)


def _split3(x):
    h1 = x.astype(BF16)
    r1 = x - h1.astype(F32)
    h2 = r1.astype(BF16)
    h3 = (r1 - h2.astype(F32)).astype(BF16)
    return h1, h2, h3


def _seg_sum(x, seg_ones):
    rows = x.shape[0]
    hi = x.astype(BF16)
    lo = (x - hi.astype(F32)).astype(BF16)
    hl = jnp.concatenate([hi, lo], axis=0)
    outs = []
    for p in range(D_MODEL // LANES):
        s = jnp.dot(hl[:, p * LANES:(p + 1) * LANES], seg_ones, preferred_element_type=F32)
        outs.append(s[:rows] + s[rows:])
    return jnp.concatenate(outs, axis=1)


def _nt_dot(a, b):
    return lax.dot_general(a, b, (((1,), (1,)), ((), ())), preferred_element_type=F32)


def _rwkv_kernel(p_ref, prev0_ref, s0_ref, mu_ref, par_ref, w2_ref, a2_ref, g2_ref,
                 y_ref, sfin_ref, prev_ref, st_ref, *, chunk, n_valid, together):
    c = pl.program_id(1)
    C = chunk
    rows_per_step = p_ref.shape[0]

    @pl.when(c == 0)
    def _():
        zero = jnp.zeros((RWKV_HEAD, RWKV_HEAD), F32)
        for gi in range(rows_per_step):
            for pr in range(RWKV_HEADS // 2):
                top = jnp.concatenate([s0_ref[gi, 2 * pr], zero], axis=1)
                bot = jnp.concatenate([zero, s0_ref[gi, 2 * pr + 1]], axis=1)
                st_ref[gi, pr] = jnp.concatenate([top, bot], axis=0)
        prev_ref[...] = prev0_ref[...]

    w0 = par_ref[0:1, :]
    a0 = par_ref[1:2, :]
    k_k = par_ref[2:3, :]
    k_a = par_ref[3:4, :]
    r_k = par_ref[4:5, :]
    lnx_w = par_ref[5:6, :]
    lnx_b = par_ref[6:7, :]

    row = lax.broadcasted_iota(jnp.int32, (C, 1), 0)
    li = lax.broadcasted_iota(jnp.int32, (LANES, LANES), 0)
    lj = lax.broadcasted_iota(jnp.int32, (LANES, LANES), 1)
    same_head = (li < RWKV_HEAD) == (lj < RWKV_HEAD)
    seg_ones = jnp.where(same_head, 1.0, 0.0).astype(BF16)
    blk_mask = jnp.where(same_head, 1.0, 0.0)
    ti = lax.broadcasted_iota(jnp.int32, (C, C), 0)
    tj = lax.broadcasted_iota(jnp.int32, (C, C), 1)
    tril_b = jnp.where(ti >= tj, 1.0, 0.0).astype(BF16)
    ti2 = lax.broadcasted_iota(jnp.int32, (C, 2 * C), 0)
    tj2 = lax.broadcasted_iota(jnp.int32, (C, 2 * C), 1) & (C - 1)
    tril_incl2 = jnp.where(ti2 >= tj2, 1.0, 0.0)
    tril_strict2 = jnp.where(ti2 > tj2, 1.0, 0.0)
    lane = lax.broadcasted_iota(jnp.int32, (1, LANES), 1)
    first = lane < RWKV_HEAD
    first_c = lax.broadcasted_iota(jnp.int32, (1, 2 * C), 1) < C
    n_lev = max(1, (C - 1).bit_length())
    pairs = range(D_MODEL // LANES)
    sls = [slice(pr * LANES, (pr + 1) * LANES) for pr in pairs]
    dot = functools.partial(jnp.dot, preferred_element_type=F32)

    def prepare(gi, out):
        p = p_ref[gi]
        shifted = jnp.where(row == 0, prev_ref[gi], pltpu.roll(p, 1, axis=0))
        prev_ref[gi] = p[C - 1:C, :]
        ps = p + (shifted - p) * mu_ref[...]
        r = ps[:, 0:D_MODEL]
        k = ps[:, D_MODEL:2 * D_MODEL]
        v = ps[:, 2 * D_MODEL:3 * D_MODEL]
        xw = ps[:, 3 * D_MODEL:3 * D_MODEL + LANES]
        xa = ps[:, 3 * D_MODEL + LANES:3 * D_MODEL + 2 * LANES]
        xg = ps[:, 3 * D_MODEL + 2 * LANES:RW_PAD]
        yield
        zw = w0 + dot(jnp.tanh(xw).astype(BF16), w2_ref[...])
        nz = -zw
        softplus = jnp.log(1.0 + jnp.exp(-jnp.abs(nz))) + jnp.maximum(nz, 0.0)
        lw = -jnp.exp(-softplus - 0.5)
        a = jax.nn.sigmoid(a0 + dot(xa.astype(BF16), a2_ref[...]))
        g = dot(jax.nn.sigmoid(xg).astype(BF16), g2_ref[...])
        yield
        kkr = k * k_k
        kk = kkr / jnp.maximum(jnp.sqrt(_seg_sum(kkr * kkr, seg_ones)), 1e-12)
        yield
        k2 = k * (1.0 + (a - 1.0) * k_a)
        bonus = _seg_sum(r * k2 * r_k, seg_ones) * v
        yield
        if n_valid is not None:
            valid = (c * C + row) < n_valid
            lw = jnp.where(valid, lw, 0.0)
            kk = jnp.where(valid, kk, 0.0)
            k2 = jnp.where(valid, k2, 0.0)
        l1, l2, l3 = _split3(lw)
        cl = dot(tril_b, l1) + dot(tril_b, l2) + dot(tril_b, l3)
        cl_end = cl[C - 1:C, :]
        yield
        kb = kk * a
        e_inv = jnp.exp(-cl)
        e_tail = jnp.exp(cl_end - cl)
        out.update(v=v, g=g, bonus=bonus, a_t=(-kk) * jnp.exp(cl - lw), r_t=r * jnp.exp(cl))
        yield
        out.update(b_t=kb * e_inv, k_t=k2 * e_inv, b_h=kb * e_tail, k_h=k2 * e_tail, w_end=jnp.exp(cl_end))

    def finish(gi, q, y):
        inv_n = 1.0 / RWKV_HEAD
        mean = _seg_sum(y, seg_ones) * inv_n
        yc = y - mean
        yield
        var = _seg_sum(yc * yc, seg_ones) * inv_n
        yield
        yn = yc * lax.rsqrt(var + GN_EPS) * lnx_w + lnx_b
        y_ref[gi] = ((yn + q['bonus']) * q['g']).astype(y_ref.dtype)

    def scan_chunk(gis, qs, fillers):
        def fill():
            for f in fillers:
                next(f, None)

        units = [(n, pr) for n in range(len(gis)) for pr in pairs]
        def by_head(x, is_first=first):
            return jnp.concatenate([jnp.where(is_first, x, 0.0), jnp.where(is_first, 0.0, x)], axis=0).astype(BF16)

        sel_c = lambda x0, x1: jnp.where(first_c, x0, x1)
        zs = {(n, pr): st_ref[gis[n], pr] for n, pr in units}
        ars = {(n, pr): jnp.concatenate([qs[n]['a_t'][:, sls[pr]], qs[n]['r_t'][:, sls[pr]]], axis=0)
               for n, pr in units}
        q0s = {u: _nt_dot(ars[u].astype(BF16), zs[u].astype(BF16)) for u in units}
        fill()
        xx, lk, mb, mk, vsw = {}, {}, {}, {}, {}
        for u in units:
            n, pr = u
            btb = qs[n]['b_t'][:, sls[pr]].astype(BF16)
            ktb = qs[n]['k_t'][:, sls[pr]].astype(BF16)
            g0 = _nt_dot(jnp.where(first, ars[u], 0.0).astype(BF16), jnp.concatenate([btb, ktb], axis=0))
            g1 = _nt_dot(jnp.where(first, 0.0, ars[u]).astype(BF16), jnp.concatenate([ktb, btb], axis=0))
            xx[u] = sel_c(g0[:C], g1[:C]) * tril_strict2
            lk[u] = (sel_c(g1[:C], g0[:C]) * tril_strict2).astype(BF16)
            mb[u] = (sel_c(g0[C:], g1[C:]) * tril_incl2).astype(BF16)
            mk[u] = (sel_c(g1[C:], g0[C:]) * tril_incl2).astype(BF16)
            v_p = qs[n]['v'][:, sls[pr]]
            vsw[u] = jnp.concatenate([jnp.where(first, 0.0, v_p), jnp.where(first, v_p, 0.0)],
                                     axis=0).astype(BF16)
        fill()
        us = {u: q0s[u][:C] + dot(lk[u], vsw[u]) for u in units}
        for lev in range(n_lev):
            us = {u: us[u] + dot(xx[u].astype(BF16), by_head(us[u])) for u in units}
            if lev < n_lev - 1:
                xx = {u: dot(xx[u].astype(BF16), by_head(xx[u], first_c)) for u in units}
            fill()
        ys = {u: q0s[u][C:] + dot(jnp.concatenate([mb[u], mk[u]], axis=1),
                                  jnp.concatenate([by_head(us[u]), vsw[u]], axis=0)) for u in units}
        for u in units:
            n, pr = u
            sl = sls[pr]
            uv_t = jnp.concatenate([us[u], qs[n]['v'][:, sl]], axis=0).T.astype(BF16)
            bk_h = jnp.concatenate([qs[n]['b_h'][:, sl], qs[n]['k_h'][:, sl]], axis=0).astype(BF16)
            st_ref[gis[n], pr] = zs[u] * qs[n]['w_end'][:, sl] + blk_mask * dot(uv_t, bk_h)
        for f in fillers:
            for _ in f:
                pass
        return [jnp.concatenate([ys[n, pr] for pr in pairs], axis=1) for n in range(len(gis))]

    def run_all(gens):
        for gen in gens:
            for _ in gen:
                pass

    def chain(gens):
        for gen in gens:
            yield from gen

    groups = [list(range(s0, min(s0 + together, rows_per_step))) for s0 in range(0, rows_per_step, together)]
    prepared = [dict() for _ in groups[0]]
    run_all([prepare(gi, prepared[n]) for n, gi in enumerate(groups[0])])
    finishing = []
    for gidx, gis in enumerate(groups):
        following, fillers = [], list(finishing)
        if gidx + 1 < len(groups):
            following = [dict() for _ in groups[gidx + 1]]
            fillers.append(chain([prepare(gi, following[n]) for n, gi in enumerate(groups[gidx + 1])]))
        ys = scan_chunk(gis, prepared, fillers)
        finishing = [chain([finish(gi, prepared[n], ys[n]) for n, gi in enumerate(gis)])]
        prepared = following
    run_all(finishing)

    @pl.when(c == pl.num_programs(1) - 1)
    def _():
        for gi in range(rows_per_step):
            for pr in range(RWKV_HEADS // 2):
                z = st_ref[gi, pr]
                sfin_ref[gi, 2 * pr] = z[:RWKV_HEAD, :RWKV_HEAD]
                sfin_ref[gi, 2 * pr + 1] = z[RWKV_HEAD:, RWKV_HEAD:]


def _rwkv(p3, prev0, s0z, mu_pad, par, w2p, a2p, g2p, *, chunk, n_valid, rows_per_step):
    b, t, _ = p3.shape
    n_pairs = D_MODEL // LANES
    g = rows_per_step
    assert b % g == 0
    full = lambda shape: pl.BlockSpec(shape, lambda i, c: (0,) * len(shape))
    return pl.pallas_call(
        functools.partial(_rwkv_kernel, chunk=chunk, n_valid=n_valid, together=RWKV_ROWS_TOGETHER),
        grid=(b // g, t // chunk),
        in_specs=[
            pl.BlockSpec((g, chunk, RW_PAD), lambda i, c: (i, c, 0)),
            pl.BlockSpec((g, 1, RW_PAD), lambda i, c: (i, 0, 0)),
            pl.BlockSpec((g, RWKV_HEADS, RWKV_HEAD, RWKV_HEAD), lambda i, c: (i, 0, 0, 0)),
            full((1, RW_PAD)),
            full((SUBLANES, D_MODEL)),
            full((LANES, D_MODEL)),
            full((LANES, D_MODEL)),
            full((2 * LANES, D_MODEL)),
        ],
        out_specs=[
            pl.BlockSpec((g, chunk, D_MODEL), lambda i, c: (i, c, 0)),
            pl.BlockSpec((g, RWKV_HEADS, RWKV_HEAD, RWKV_HEAD), lambda i, c: (i, 0, 0, 0)),
        ],
        out_shape=[
            jax.ShapeDtypeStruct((b, t, D_MODEL), BF16),
            jax.ShapeDtypeStruct((b, RWKV_HEADS, RWKV_HEAD, RWKV_HEAD), F32),
        ],
        scratch_shapes=[pltpu.VMEM((g, 1, RW_PAD), F32), pltpu.VMEM((g, n_pairs, LANES, LANES), F32)],
        compiler_params=_cparams(("parallel", "arbitrary")),
        name="rwkv",
    )(p3, prev0, s0z, mu_pad, par, w2p, a2p, g2p)


def _attn_kernel(*refs, blk, dil, subs, has_prev):
    if has_prev:
        q_ref, ko_ref, vo_ref, kp_ref, vp_ref, o_ref, l_ref = refs
    else:
        q_ref, ko_ref, vo_ref, o_ref, l_ref = refs
    n = pl.program_id(1)
    unit = dil * blk
    scale = ATT_HEAD ** -0.5
    head = pl.program_id(2)
    my_lanes = (lax.broadcasted_iota(jnp.int32, (1, LANES), 1) >> (LSE_LANES.bit_length() - 1)) == head

    @pl.when(head == 0)
    def _():
        l_ref[...] = jnp.zeros(l_ref.shape, F32)

    def class_rows(j, r):
        return pl.ds(j * unit + r, blk, stride=dil) if dil > 1 else pl.ds(j * unit, blk)

    def window_mask(j):
        two = has_prev or j > 0
        nk = 2 * blk if two else blk
        qi = lax.broadcasted_iota(jnp.int32, (blk, nk), 0)
        ki = lax.broadcasted_iota(jnp.int32, (blk, nk), 1)
        if not two:
            return ki <= qi
        valid = (ki >= qi) & (ki <= qi + blk)
        if j == 0:
            valid = valid & (ki >= jnp.where(n == 0, blk, 0))
        return valid

    def keys_values(j, r):
        rows = class_rows(j, r)
        kc = ko_ref[0, rows, :]
        vc = vo_ref[0, rows, :]
        if j > 0:
            kc = jnp.concatenate([ko_ref[0, class_rows(j - 1, r), :], kc], axis=0)
            vc = jnp.concatenate([vo_ref[0, class_rows(j - 1, r), :], vc], axis=0)
        elif has_prev:
            kc = jnp.concatenate([kp_ref[0, class_rows(0, r), :], kc], axis=0)
            vc = jnp.concatenate([vp_ref[0, class_rows(0, r), :], vc], axis=0)
        return kc.astype(BF16), vc.astype(BF16)

    windows = [(j, r) for j in range(subs) for r in range(dil)]
    for w0 in range(0, len(windows), ATTN_WINDOWS_TOGETHER):
        group = windows[w0:w0 + ATTN_WINDOWS_TOGETHER]
        kvs = [keys_values(j, r) for j, r in group]
        scores = [_nt_dot(q_ref[0, class_rows(j, r), :].astype(BF16), kvs[i][0]) * scale
                  for i, (j, r) in enumerate(group)]
        probs, dens, lses = [], [], []
        for i, (j, r) in enumerate(group):
            s = jnp.where(window_mask(j), scores[i], -jnp.inf)
            m = jnp.max(s, axis=-1, keepdims=True)
            pexp = jnp.exp(s - m)
            den = jnp.sum(pexp, axis=-1, keepdims=True)
            probs.append(pexp.astype(BF16))
            dens.append(den)
            lses.append(m + jnp.log(den))
        for i, (j, r) in enumerate(group):
            rows = class_rows(j, r)
            o = jnp.dot(probs[i], kvs[i][1], preferred_element_type=F32) / dens[i]
            o_ref[0, rows, :] = o.astype(o_ref.dtype)
            l_ref[0, rows, :] = jnp.where(my_lanes, lses[i], l_ref[0, rows, :])


ATTN_STEP_ROWS = 1024
ATTN_WINDOWS_TOGETHER = 8


def _attn_prompt(p3, g, dil, blk):
    b, t, _ = p3.shape
    unit = dil * blk
    assert t % unit == 0
    subs = max(1, min(ATTN_STEP_ROWS, t) // unit)
    span = unit * subs
    assert t % span == 0
    nb = t // span
    has_prev = nb > 1
    q_col = (T_Q * PROJ_TN + g * ATT_OUT) // ATT_HEAD
    k_col = (T_KV * PROJ_TN + g * 2 * ATT_OUT) // ATT_HEAD
    v_col = k_col + ATT_HPG
    cur = lambda col: pl.BlockSpec((1, span, ATT_HEAD), lambda bi, n, h: (bi, n, col + h))
    prev = lambda col: pl.BlockSpec((1, unit, ATT_HEAD),
                                    lambda bi, n, h: (bi, jnp.maximum(n * subs - 1, 0), col + h))
    in_specs = [cur(q_col), cur(k_col), cur(v_col)]
    if has_prev:
        in_specs += [prev(k_col), prev(v_col)]
    out_sd = jax.ShapeDtypeStruct((b, t, ATT_OUT), F32)
    out_spec = pl.BlockSpec((1, span, ATT_HEAD), lambda bi, n, h: (bi, n, h))
    o, l = pl.pallas_call(
        functools.partial(_attn_kernel, blk=blk, dil=dil, subs=subs, has_prev=has_prev),
        grid=(b, nb, ATT_HPG),
        in_specs=in_specs,
        out_specs=[out_spec, pl.BlockSpec((1, span, LANES), lambda bi, n, h: (bi, n, 0))],
        out_shape=[out_sd, jax.ShapeDtypeStruct((b, t, LANES), F32)],
        compiler_params=_cparams(("parallel", "parallel", "arbitrary")),
        name=f"attn_prompt_g{g}",
    )(*([p3] * len(in_specs)))
    return o.reshape(b * t, ATT_OUT), l.reshape(b * t, LANES)


def _kv_rows_kernel(x_ref, o_ref):
    for s in range(2):
        for h in range(ATT_HPG):
            c0 = (s * ATT_HPG + h) * ATT_HEAD
            o_ref[0, :, s, h, :] = x_ref[0, :, c0:c0 + ATT_HEAD]


def _kv_rows(p3, g, keep, tr):
    b, t, _ = p3.shape
    col = (T_KV * PROJ_TN) // (2 * ATT_OUT) + g
    r0 = (t - keep) // tr
    return pl.pallas_call(
        _kv_rows_kernel,
        grid=(b, keep // tr),
        in_specs=[pl.BlockSpec((1, tr, 2 * ATT_OUT), lambda bi, i: (bi, r0 + i, col))],
        out_specs=pl.BlockSpec((1, tr, 2, ATT_HPG, ATT_HEAD), lambda bi, i: (bi, i, 0, 0, 0)),
        out_shape=jax.ShapeDtypeStruct((b, keep, 2, ATT_HPG, ATT_HEAD), F32),
        compiler_params=_cparams(("parallel", "parallel")),
        name=f"kv_rows_g{g}",
    )(p3)


def _attn_sample_kernel(q_ref, n0_ref, n1_ref, n2_ref, c0_ref, c1_ref, c2_ref, o_ref, l_ref, *, dils, ts):
    n_refs = (n0_ref, n1_ref, n2_ref)
    c_refs = (c0_ref, c1_ref, c2_ref)
    scale = ATT_HEAD ** -0.5
    for g in range(N_GROUPS):
        dil = dils[g]
        new = n_refs[g][0]
        kn, vn = new[:, 0], new[:, 1]
        rows = c_refs[g].shape[2] // dil
        wi = lax.broadcasted_iota(jnp.int32, (rows, ATT_HPG, 1), 0)
        ni = lax.broadcasted_iota(jnp.int32, (ts, ATT_HPG, 1), 0)
        for t in range(ts):
            cls = pl.ds(t % dil, rows, stride=dil) if dil > 1 else pl.ds(0, rows)
            cache = c_refs[g][0, 0, cls]
            kc, vc = cache[:, 0], cache[:, 1]
            q = q_ref[0, t, g]
            sc = jnp.sum(kc * q[None], axis=-1, keepdims=True) * scale
            sn = jnp.sum(kn * q[None], axis=-1, keepdims=True) * scale
            c_ok = wi >= (t // dil)
            n_ok = functools.reduce(jnp.logical_or, [ni == tn for tn in range(t + 1) if (t - tn) % dil == 0])
            sc = jnp.where(c_ok, sc, -jnp.inf)
            sn = jnp.where(n_ok, sn, -jnp.inf)
            m = jnp.maximum(jnp.max(sc, axis=0, keepdims=True), jnp.max(sn, axis=0, keepdims=True))
            pc = jnp.exp(sc - m)
            pn = jnp.exp(sn - m)
            den = jnp.sum(pc, axis=0, keepdims=True) + jnp.sum(pn, axis=0, keepdims=True)
            o = (jnp.sum(pc * vc, axis=0, keepdims=True) + jnp.sum(pn * vn, axis=0, keepdims=True)) / den
            o_ref[0, t, g] = o[0]
            l_ref[0, t, g] = jnp.broadcast_to((m + jnp.log(den))[0], (ATT_HPG, ATT_HEAD))


def _attn_sample(q5, new_kv, caches):
    b, ts = q5.shape[:2]
    dils = tuple(d for _, d in DILATION_GROUPS)
    in_specs = [pl.BlockSpec((1, ts, N_GROUPS, ATT_HPG, ATT_HEAD), lambda bi: (bi, 0, 0, 0, 0))]
    for g in range(N_GROUPS):
        in_specs.append(pl.BlockSpec((1, ts, 2, ATT_HPG, ATT_HEAD), lambda bi: (bi, 0, 0, 0, 0)))
    for g in range(N_GROUPS):
        w = caches[g].shape[2]
        in_specs.append(pl.BlockSpec((1, 1, w, 2, ATT_HPG, ATT_HEAD), lambda bi: (0, bi, 0, 0, 0, 0)))
    out_sd = jax.ShapeDtypeStruct((b, ts, N_GROUPS, ATT_HPG, ATT_HEAD), F32)
    out_spec = pl.BlockSpec((1, ts, N_GROUPS, ATT_HPG, ATT_HEAD), lambda bi: (bi, 0, 0, 0, 0))
    o, l = pl.pallas_call(
        functools.partial(_attn_sample_kernel, dils=dils, ts=ts),
        grid=(b,),
        in_specs=in_specs,
        out_specs=[out_spec, out_spec],
        out_shape=[out_sd, out_sd],
        compiler_params=_cparams(("parallel",)),
        name="attn_sample",
    )(q5, *new_kv, *caches)
    o_list = [o[:, :, g].reshape(b * ts, ATT_OUT) for g in range(N_GROUPS)]
    l_list = [l[:, :, g, :, :LSE_LANES].reshape(b * ts, LANES) for g in range(N_GROUPS)]
    return o_list, l_list


ROLL_ROWS = 64
ROLL_BLOCK_BYTES = 8 << 20


def _roll_kernel(c_ref, n_ref, o_ref):
    w, ts = c_ref.shape[2], n_ref.shape[2]
    body_rows = w - ROLL_ROWS

    def move(i, carry):
        r = pl.multiple_of(i * ROLL_ROWS, ROLL_ROWS)
        o_ref[0, :, pl.ds(r, ROLL_ROWS)] = c_ref[0, :, pl.ds(r + ts, ROLL_ROWS)]
        return carry

    lax.fori_loop(0, body_rows // ROLL_ROWS, move, 0)
    o_ref[0, :, body_rows:w - ts] = c_ref[0, :, body_rows + ts:w]
    o_ref[0, :, w - ts:w] = n_ref[0]


def _cache_roll(cache, new_rows):
    _, b, w = cache.shape[:3]
    ts = new_rows.shape[2]
    assert w % ROLL_ROWS == 0 and ts < ROLL_ROWS
    row_bytes = 2 * ATT_OUT * 4
    bb = max(1, min(b, ROLL_BLOCK_BYTES // (w * row_bytes)))
    assert b % bb == 0
    tail = (2, ATT_HPG, ATT_HEAD)
    return pl.pallas_call(
        _roll_kernel,
        grid=(b // bb,),
        in_specs=[pl.BlockSpec((1, bb, w) + tail, lambda i: (0, i, 0, 0, 0, 0)),
                  pl.BlockSpec((1, bb, ts) + tail, lambda i: (0, i, 0, 0, 0, 0))],
        out_specs=pl.BlockSpec((1, bb, w) + tail, lambda i: (0, i, 0, 0, 0, 0)),
        out_shape=jax.ShapeDtypeStruct(cache.shape, cache.dtype),
        compiler_params=_cparams(("parallel",)),
        name="cache_roll",
    )(cache, new_rows)


def _merge_kernel(x_ref, yrw_ref, gate_ref, o0_ref, o1_ref, o2_ref, l0_ref, l1_ref, l2_ref,
                  wba_ref, wbb_ref, wo_ref, fnw_ref, rw_ref, rb_ref,
                  h_ref, hn_ref, ti_ref, tg_ref, cnt_ref, run_ref):
    @pl.when(pl.program_id(0) == 0)
    def _():
        run_ref[...] = jnp.zeros(run_ref.shape, F32)

    rows_here = x_ref.shape[0]
    heads_out = []
    for hd in range(ATT_HPG):
        hs = slice(hd * ATT_HEAD, (hd + 1) * ATT_HEAD)
        lses = [jnp.broadcast_to(l_ref[:, hd * LSE_LANES:hd * LSE_LANES + 1], (rows_here, ATT_HEAD))
                for l_ref in (l0_ref, l1_ref, l2_ref)]
        m = jnp.maximum(jnp.maximum(lses[0], lses[1]), lses[2])
        e0, e1, e2 = jnp.exp(lses[0] - m), jnp.exp(lses[1] - m), jnp.exp(lses[2] - m)
        heads_out.append((e0 * o0_ref[:, hs] + e1 * o1_ref[:, hs] + e2 * o2_ref[:, hs]) / (e0 + e1 + e2))
    o_att = jnp.concatenate(heads_out, axis=1)
    br_a = jnp.dot(yrw_ref[...], wba_ref[...], preferred_element_type=F32)
    br_b = jnp.dot(o_att.astype(BF16), wbb_ref[...], preferred_element_type=F32)
    merged = gate_ref[:, 0:D_MODEL] * br_a + gate_ref[:, D_MODEL:2 * D_MODEL] * br_b
    h = x_ref[...] + jnp.dot(merged.astype(BF16), wo_ref[...], preferred_element_type=F32)
    h_ref[...] = h
    hn = h * lax.rsqrt(jnp.mean(h * h, axis=-1, keepdims=True) + NORM_EPS) * fnw_ref[...]
    hn_ref[...] = hn
    logits = jnp.dot(hn.astype(BF16), rw_ref[...], preferred_element_type=F32) + rb_ref[...]
    lane_i = lax.broadcasted_iota(jnp.int32, logits.shape, 1)
    lane = lane_i.astype(F32)
    vals, idxs = [], []
    cur = logits
    for _ in range(TOP_K):
        mx = jnp.max(cur, axis=-1, keepdims=True)
        ix = jnp.min(jnp.where(cur == mx, lane, float(LANES)), axis=-1, keepdims=True)
        vals.append(mx)
        idxs.append(ix)
        cur = jnp.where(lane == ix, -jnp.inf, cur)
    es = [jnp.exp(vv - vals[0]) for vv in vals]
    tot = es[0] + es[1] + es[2] + es[3]
    tm = logits.shape[0]
    ri = lax.broadcasted_iota(jnp.int32, (tm, tm), 0)
    rj = lax.broadcasted_iota(jnp.int32, (tm, tm), 1)
    before = jnp.where(ri > rj, 1.0, 0.0).astype(BF16)
    run = run_ref[0:1, :]
    ranks = []
    for kq in range(TOP_K):
        onehot = jnp.where(lane == idxs[kq], 1.0, 0.0)
        prior = jnp.dot(before, onehot.astype(BF16), preferred_element_type=F32) + run
        ranks.append(jnp.sum(onehot * prior, axis=-1, keepdims=True))
        run = run + jnp.sum(onehot, axis=0, keepdims=True)
    run_ref[...] = jnp.broadcast_to(run, run_ref.shape)
    cnt_ref[...] = jnp.broadcast_to(run, cnt_ref.shape)

    ti = jnp.zeros(logits.shape, F32)
    tg = jnp.zeros(logits.shape, F32)
    for kq in range(TOP_K):
        ti = jnp.where(lane_i == kq, idxs[kq], ti)
        ti = jnp.where(lane_i == TOP_K + kq, ranks[kq], ti)
        tg = jnp.where(lane_i == kq, es[kq] / tot, tg)
    ti_ref[...] = ti.astype(jnp.int32)
    tg_ref[...] = tg


LSE_LANES = LANES // ATT_HPG


def _merge(x2d, yrw, gates, o_list, l_list, wba, wbb, wo, fnw, rw_pad, rb_pad, *, tm):
    n = x2d.shape[0]
    row = lambda wdt: pl.BlockSpec((tm, wdt), lambda i: (i, 0))
    full = lambda a: pl.BlockSpec(a.shape, lambda i: (0,) * a.ndim)
    return pl.pallas_call(
        _merge_kernel,
        grid=(n // tm,),
        in_specs=[row(D_MODEL), row(D_MODEL), row(2 * D_MODEL),
                  row(ATT_OUT), row(ATT_OUT), row(ATT_OUT), row(LANES), row(LANES), row(LANES),
                  full(wba), full(wbb), full(wo), full(fnw), full(rw_pad), full(rb_pad)],
        out_specs=[row(D_MODEL), row(D_MODEL), row(LANES), row(LANES),
                   pl.BlockSpec((SUBLANES, LANES), lambda i: (0, 0))],
        out_shape=[jax.ShapeDtypeStruct((n, D_MODEL), F32), jax.ShapeDtypeStruct((n, D_MODEL), F32),
                   jax.ShapeDtypeStruct((n, LANES), jnp.int32), jax.ShapeDtypeStruct((n, LANES), F32),
                   jax.ShapeDtypeStruct((SUBLANES, LANES), F32)],
        scratch_shapes=[pltpu.VMEM((SUBLANES, LANES), F32)],
        compiler_params=_cparams(("arbitrary",)),
        name="merge",
    )(x2d, yrw, gates, *o_list, *l_list, wba, wbb, wo, fnw, rw_pad, rb_pad)


def _row_copy(src_ref, s_row, dst_ref, d_row, sem):
    return pltpu.make_async_copy(src_ref.at[pl.ds(s_row, 1), :], dst_ref.at[pl.ds(d_row, 1), :], sem)


def _dispatch_kernel(pad_start_ref, pad_len_ref, pos_ref, hn_ref, xs_ref, zrow_ref, sem, *, tt):
    i = pl.program_id(0)

    @pl.when(i == 0)
    def _():
        zrow_ref[...] = jnp.zeros(zrow_ref.shape, F32)

        def per_expert(e, carry):
            s = pad_start_ref[e]
            cnt = pad_len_ref[e]

            def issue(rr, cc):
                _row_copy(zrow_ref, 0, xs_ref, s + rr, sem).start()
                return cc

            def drain(rr, cc):
                _row_copy(zrow_ref, 0, xs_ref, s + rr, sem).wait()
                return cc

            lax.fori_loop(0, cnt, issue, 0)
            lax.fori_loop(0, cnt, drain, 0)
            return carry

        lax.fori_loop(0, N_EXPERTS, per_expert, 0)

    def issue(nn, cc):
        for kq in range(TOP_K):
            _row_copy(hn_ref, nn, xs_ref, pos_ref[0, 0, nn * TOP_K + kq], sem).start(priority=kq % 2)
        return cc

    def drain(nn, cc):
        for kq in range(TOP_K):
            _row_copy(hn_ref, nn, xs_ref, pos_ref[0, 0, nn * TOP_K + kq], sem).wait()
        return cc

    lax.fori_loop(0, tt, issue, 0)
    lax.fori_loop(0, tt, drain, 0)


def _dispatch(hn, pos3, pad_start, pad_len, rows, *, tt):
    n = hn.shape[0]
    return pl.pallas_call(
        functools.partial(_dispatch_kernel, tt=tt),
        grid_spec=pltpu.PrefetchScalarGridSpec(
            num_scalar_prefetch=2,
            grid=(n // tt,),
            in_specs=[
                pl.BlockSpec((1, 1, tt * TOP_K), lambda i, ps, plen: (i, 0, 0), memory_space=pltpu.SMEM),
                pl.BlockSpec((tt, D_MODEL), lambda i, ps, plen: (i, 0)),
            ],
            out_specs=pl.BlockSpec(memory_space=pl.ANY),
            scratch_shapes=[pltpu.VMEM((SUBLANES, D_MODEL), F32), pltpu.SemaphoreType.DMA(())],
        ),
        out_shape=jax.ShapeDtypeStruct((rows, D_MODEL), F32),
        compiler_params=_cparams(("arbitrary",)),
        name="moe_dispatch",
    )(pad_start, pad_len, pos3, hn)


def _moe_kernel(be_ref, nu_ref, xs_ref, w1_ref, b1_ref, w2_ref, b2_ref, y_ref):
    i = pl.program_id(0)

    @pl.when(i < nu_ref[0])
    def _():
        x = xs_ref[...].astype(BF16)
        hdn = jnp.dot(x, w1_ref[0], preferred_element_type=F32) + b1_ref[0]
        glu = jnp.minimum(hdn[:, 0:D_MODEL], SWIGLU_LIMIT)
        lin = jnp.clip(hdn[:, D_MODEL:2 * D_MODEL], -SWIGLU_LIMIT, SWIGLU_LIMIT)
        act = glu * jax.nn.sigmoid(SWIGLU_ALPHA * glu) * (lin + 1.0)
        y_ref[...] = jnp.dot(act.astype(BF16), w2_ref[0], preferred_element_type=F32) + b2_ref[0]

    @pl.when(i >= nu_ref[0])
    def _():
        y_ref[...] = jnp.zeros(y_ref.shape, F32)


def _moe(xs, blk_expert, n_used, w1, b1, w2, b2, *, tm):
    rows = xs.shape[0]
    nb = rows // tm
    return pl.pallas_call(
        _moe_kernel,
        grid_spec=pltpu.PrefetchScalarGridSpec(
            num_scalar_prefetch=2,
            grid=(nb,),
            in_specs=[
                pl.BlockSpec((tm, D_MODEL), lambda i, be, nu: (jnp.minimum(i, nu[0] - 1), 0)),
                pl.BlockSpec((1, D_MODEL, 2 * D_MODEL), lambda i, be, nu: (be[i], 0, 0)),
                pl.BlockSpec((1, 1, 2 * D_MODEL), lambda i, be, nu: (be[i], 0, 0)),
                pl.BlockSpec((1, D_MODEL, D_MODEL), lambda i, be, nu: (be[i], 0, 0)),
                pl.BlockSpec((1, 1, D_MODEL), lambda i, be, nu: (be[i], 0, 0)),
            ],
            out_specs=pl.BlockSpec((tm, D_MODEL), lambda i, be, nu: (i, 0)),
        ),
        out_shape=jax.ShapeDtypeStruct((rows, D_MODEL), F32),
        compiler_params=_cparams(("arbitrary",)),
        name="moe_experts",
    )(blk_expert, n_used, xs, w1, b1, w2, b2)


def _combine_kernel(pos_ref, h_ref, tg_ref, yb_ref, o_ref, buf_ref, sem, *, tt):
    def issue(nn, cc):
        for kq in range(TOP_K):
            _row_copy(yb_ref, pos_ref[0, 0, nn * TOP_K + kq], buf_ref.at[kq], nn, sem).start(priority=kq % 2)
        return cc

    def drain(nn, cc):
        for kq in range(TOP_K):
            _row_copy(yb_ref, pos_ref[0, 0, nn * TOP_K + kq], buf_ref.at[kq], nn, sem).wait()
        return cc

    lax.fori_loop(0, tt, issue, 0)
    lax.fori_loop(0, tt, drain, 0)
    acc = h_ref[...]
    for kq in range(TOP_K):
        acc = acc + tg_ref[:, kq:kq + 1] * buf_ref[kq]
    o_ref[...] = acc


def _combine(h, tg, pos3, yb, *, tt):
    n = h.shape[0]
    return pl.pallas_call(
        functools.partial(_combine_kernel, tt=tt),
        grid=(n // tt,),
        in_specs=[
            pl.BlockSpec((1, 1, tt * TOP_K), lambda i: (i, 0, 0), memory_space=pltpu.SMEM),
            pl.BlockSpec((tt, D_MODEL), lambda i: (i, 0)),
            pl.BlockSpec((tt, LANES), lambda i: (i, 0)),
            pl.BlockSpec(memory_space=pl.ANY),
        ],
        out_specs=pl.BlockSpec((tt, D_MODEL), lambda i: (i, 0)),
        out_shape=jax.ShapeDtypeStruct((n, D_MODEL), F32),
        scratch_shapes=[pltpu.VMEM((TOP_K, tt, D_MODEL), F32), pltpu.SemaphoreType.DMA(())],
        compiler_params=_cparams(("arbitrary",)),
        name="moe_combine",
    )(pos3, h, tg, yb)


def _routing(top_idx, rank, counts, tm):
    n = top_idx.shape[0]
    nk = n * TOP_K
    experts = jnp.arange(N_EXPERTS, dtype=jnp.int32)
    blocks_e = (counts + tm - 1) // tm
    upper = (experts[:, None] <= experts[None, :]).astype(jnp.int32)
    block_end = jnp.sum(blocks_e[:, None] * upper, axis=0)
    row_start = (block_end - blocks_e) * tm
    onehot = (top_idx[:, :, None] == experts[None, None, :]).astype(jnp.int32)
    pos = (jnp.sum(onehot * row_start[None, None, :], axis=-1) + rank).reshape(nk).astype(jnp.int32)
    n_blocks = -(-nk // tm) + N_EXPERTS
    blk_ids = jnp.arange(n_blocks, dtype=jnp.int32)
    blk_expert = jnp.minimum(
        jnp.sum((block_end[None, :] <= blk_ids[:, None]).astype(jnp.int32), axis=1), N_EXPERTS - 1
    ).astype(jnp.int32)
    n_used = block_end[-1:].astype(jnp.int32)
    pad_start = (row_start + counts).astype(jnp.int32)
    pad_len = (blocks_e * tm - counts).astype(jnp.int32)
    return pos, blk_expert, n_used, pad_start, pad_len, n_blocks * tm


def _pad_cols(a, width):
    return jnp.pad(a, ((0, 0), (0, width - a.shape[1])))


def _pad_rows(a, height):
    return jnp.pad(a, ((0, height - a.shape[0]), (0, 0)))


def _proj_columns(a):
    d = D_MODEL
    xw = a[:, 3 * d:3 * d + DECAY_LORA]
    xa = a[:, 3 * d + DECAY_LORA:3 * d + DECAY_LORA + AAA_LORA]
    xg = a[:, 3 * d + DECAY_LORA + AAA_LORA:C_RW]
    att = a[:, C_RW:C_RW + 3 * ATT_DIM]
    q, k, v = att[:, :ATT_DIM], att[:, ATT_DIM:2 * ATT_DIM], att[:, 2 * ATT_DIM:]
    parts = [a[:, :3 * d], _pad_cols(xw, LANES), _pad_cols(xa, LANES), _pad_cols(xg, 2 * LANES), q]
    for g in range(N_GROUPS):
        parts.append(k[:, g * ATT_OUT:(g + 1) * ATT_OUT])
        parts.append(v[:, g * ATT_OUT:(g + 1) * ATT_OUT])
    parts.append(a[:, C_RW + 3 * ATT_DIM:])
    return jnp.concatenate(parts, axis=1)


def _rope_tables(pos):
    inv = ROPE_THETA ** (-jnp.arange(0, ATT_HEAD, 2, dtype=F32) / ATT_HEAD)
    ang = pos.astype(F32)[:, None] * inv[None, :]
    cos, sin = jnp.cos(ang), jnp.sin(ang)
    return jnp.concatenate([cos, cos], axis=1), jnp.concatenate([-sin, sin], axis=1)


def _prep_weights(attn_norm_w, w_in, rw_mu, rw_w0, rw_w2, rw_a0, rw_a2, rw_g2, rw_kk, rw_ka, rw_rk,
                  rw_lnx_w, rw_lnx_b, q_norm_w, k_norm_w, w_br, w_out, ffn_norm_w, router_w, router_b,
                  moe_w1, moe_b1, moe_w2, moe_b2):
    w = {}
    w['nw'] = attn_norm_w.reshape(1, D_MODEL)
    w['w_pad'] = _proj_columns(w_in).astype(BF16)
    w['mu_pad'] = _proj_columns(_pad_cols(rw_mu.reshape(1, C_RW), C_RW + 3 * ATT_DIM + 2 * D_MODEL))[:, :RW_PAD]
    w['par'] = jnp.concatenate([rw_w0.reshape(1, -1), rw_a0.reshape(1, -1), rw_kk.reshape(1, -1),
                                rw_ka.reshape(1, -1), rw_rk.reshape(1, -1), rw_lnx_w.reshape(1, -1),
                                rw_lnx_b.reshape(1, -1), jnp.zeros((1, D_MODEL), F32)], axis=0)
    w['w2p'] = _pad_rows(rw_w2, LANES).astype(BF16)
    w['a2p'] = _pad_rows(rw_a2, LANES).astype(BF16)
    w['g2p'] = _pad_rows(rw_g2, 2 * LANES).astype(BF16)
    w['qn'] = q_norm_w.reshape(1, ATT_HEAD)
    w['kn'] = k_norm_w.reshape(1, ATT_HEAD)
    w['wba'] = w_br[:D_MODEL].astype(BF16)
    w['wbb'] = w_br[D_MODEL:].astype(BF16)
    w['wo'] = w_out.astype(BF16)
    w['fnw'] = ffn_norm_w.reshape(1, D_MODEL)
    w['rw_pad'] = _pad_cols(router_w, LANES).astype(BF16)
    w['rb_pad'] = jnp.concatenate([router_b.reshape(1, N_EXPERTS),
                                   jnp.full((1, LANES - N_EXPERTS), -1e30, F32)], axis=1)
    cols = jnp.arange(2 * D_MODEL, dtype=jnp.int32)
    src = jnp.where(cols < D_MODEL, 2 * cols, 2 * (cols - D_MODEL) + 1)
    perm = (jnp.arange(2 * D_MODEL, dtype=jnp.int32)[:, None] == src[None, :]).astype(BF16)
    w['w1'] = jnp.einsum('edh,hk->edk', moe_w1.astype(BF16), perm, preferred_element_type=BF16)
    w['b1'] = jnp.concatenate([moe_b1[:, 0::2], moe_b1[:, 1::2]], axis=1).reshape(N_EXPERTS, 1, 2 * D_MODEL)
    w['w2'] = moe_w2.astype(BF16)
    w['b2'] = moe_b2.reshape(N_EXPERTS, 1, D_MODEL)
    return w


def _layer(x, pos, shift_prev, wkv0, caches, w, *, proj_tm, merge_tm, moe_tm, tok_tt):
    b, t, _ = x.shape
    n = b * t
    x2d = x.reshape(n, D_MODEL)
    is_prompt = caches is None
    rw_rows = next(g for g in (RWKV_ROWS_PER_STEP, RWKV_ROWS_TOGETHER, 1) if b % g == 0)

    cs_tab, sn_tab = _rope_tables(pos)
    if not is_prompt:
        cs_tab = jnp.tile(cs_tab, (proj_tm // t, 1))
        sn_tab = jnp.tile(sn_tab, (proj_tm // t, 1))
    p2d, gates = _proj(x2d, w['nw'], w['w_pad'], w['qn'], w['kn'], cs_tab, sn_tab,
                       tm=proj_tm, n_tiles=N_TILES, do_norm=True)
    p3 = p2d.reshape(b, t, p2d.shape[1])
    shift_out = _rmsnorm_rows(x[:, t - 1, :], w['nw'])

    if is_prompt:
        prev0 = jnp.zeros((b, 1, RW_PAD), F32)
        s0z = jnp.zeros((b, RWKV_HEADS, RWKV_HEAD, RWKV_HEAD), F32)
        yrw, s_fin = _rwkv(p3, prev0, s0z, w['mu_pad'], w['par'], w['w2p'], w['a2p'], w['g2p'],
                          chunk=SCAN_CHUNK, n_valid=None, rows_per_step=rw_rows)
        yrw = yrw.reshape(n, D_MODEL)
    else:
        one = jnp.ones((1, ATT_HEAD), F32)
        zero_tab = jnp.zeros((b, ATT_HEAD), F32)
        prev_rw, = _proj(shift_prev, w['nw'], w['w_pad'], one, one, zero_tab, zero_tab,
                         tm=b, n_tiles=T_LORA + 1, do_norm=False)
        t_pad = -(-t // SHORT_SCAN_CHUNK) * SHORT_SCAN_CHUNK
        p_rw = jnp.pad(p3[:, :, :RW_PAD], ((0, 0), (0, t_pad - t), (0, 0)))
        yrw, s_fin = _rwkv(p_rw, prev_rw.reshape(b, 1, RW_PAD), wkv0, w['mu_pad'], w['par'],
                          w['w2p'], w['a2p'], w['g2p'], chunk=SHORT_SCAN_CHUNK, n_valid=t, rows_per_step=rw_rows)
        yrw = yrw[:, :t].reshape(n, D_MODEL)

    new_kv = []
    if is_prompt:
        o_list, l_list = [], []
        for g, (window, dil) in enumerate(DILATION_GROUPS):
            o_g, l_g = _attn_prompt(p3, g, dil, window // dil)
            o_list.append(o_g)
            l_list.append(l_g)
            keep = min(window, t)
            new_kv.append(_kv_rows(p3, g, keep, min(keep, 256))[None])
    else:
        kv_col = T_KV * PROJ_TN
        q5 = p3[:, :, T_Q * PROJ_TN:T_KV * PROJ_TN].reshape(b, t, N_GROUPS, ATT_HPG, ATT_HEAD)
        kv_new = [p3[:, :, kv_col + g * 2 * ATT_OUT:kv_col + (g + 1) * 2 * ATT_OUT].reshape(b, t, 2, ATT_HPG, ATT_HEAD)
                  for g in range(N_GROUPS)]
        o_list, l_list = _attn_sample(q5, kv_new, caches)
        new_kv = [_cache_roll(caches[g], kv_new[g][None]) for g in range(N_GROUPS)]

    h, hn, ti, tg, cnt = _merge(x2d, yrw, gates, o_list, l_list, w['wba'], w['wbb'], w['wo'], w['fnw'],
                                w['rw_pad'], w['rb_pad'], tm=merge_tm)

    counts = cnt[0, :N_EXPERTS].astype(jnp.int32)
    pos_rows, blk_expert, n_used, pad_start, pad_len, rows = _routing(
        ti[:, :TOP_K], ti[:, TOP_K:2 * TOP_K], counts, moe_tm)
    pos3 = pos_rows.reshape(n // tok_tt, 1, tok_tt * TOP_K)
    xs = _dispatch(hn, pos3, pad_start, pad_len, rows, tt=tok_tt)
    yb = _moe(xs, blk_expert, n_used, w['w1'], w['b1'], w['w2'], w['b2'], tm=moe_tm)
    y = _combine(h, tg, pos3, yb, tt=tok_tt)
    return y.reshape(b, t, D_MODEL), shift_out, s_fin, new_kv


def kernel(x_prompt, x_sample, state_wkv, state_shift, cache_kv_w128, cache_kv_w512, cache_kv_w2048,
           attn_norm_w, w_in, rw_mu, rw_w0, rw_w2, rw_a0, rw_a2, rw_g2, rw_kk, rw_ka, rw_rk,
           rw_lnx_w, rw_lnx_b, q_norm_w, k_norm_w, w_br, w_out, ffn_norm_w, router_w, router_b,
           moe_w1, moe_b1, moe_w2, moe_b2):
    depth = w_in.shape[0]
    assert depth == 1
    bp, tp, _ = x_prompt.shape
    bs, ts, _ = x_sample.shape
    l = 0
    w = _prep_weights(attn_norm_w[l], w_in[l], rw_mu[l], rw_w0[l], rw_w2[l], rw_a0[l], rw_a2[l], rw_g2[l],
                      rw_kk[l], rw_ka[l], rw_rk[l], rw_lnx_w[l], rw_lnx_b[l], q_norm_w[l], k_norm_w[l],
                      w_br[l], w_out[l], ffn_norm_w[l], router_w[l], router_b[l],
                      moe_w1[l], moe_b1[l], moe_w2[l], moe_b2[l])
    pos_p = jnp.arange(tp, dtype=jnp.int32)
    pos_s = PAST_LEN + jnp.arange(ts, dtype=jnp.int32)
    yp, sh_p, wkv_p, kv_p = _layer(x_prompt, pos_p, None, None, None, w,
                                   proj_tm=min(2048, tp), merge_tm=256, moe_tm=MOE_TM, tok_tt=1024)
    ns = bs * ts
    ys, sh_s, wkv_s, kv_s = _layer(x_sample, pos_s, state_shift[l], state_wkv[l],
                                   (cache_kv_w128, cache_kv_w512, cache_kv_w2048), w,
                                   proj_tm=ns, merge_tm=min(256, ns), moe_tm=128, tok_tt=min(256, ns))
    return (yp, ys,
            wkv_p[None], sh_p[None], kv_p[0], kv_p[1], kv_p[2],
            wkv_s[None], sh_s[None], kv_s[0], kv_s[1], kv_s[2])
```
